```python
import math, functools
import jax, jax.numpy as jnp
from jax import lax
import numpy as np

D_MODEL = 2048
BATCH = 32
SEQ = 256
DEPTH = 2
DEC_BATCH = 4
DEC_SEQ = 4096
PAST_LEN = 512

GRID_W = 64
MIX_WIDTH = D_MODEL
GDN_DK = 128
GDN_DV = 128
GDN_HEADS = (MIX_WIDTH // 2) // GDN_DV
GDN_QK = GDN_HEADS * GDN_DK
GDN_VW = GDN_HEADS * GDN_DV
LRU_WIDTH = MIX_WIDTH - GDN_VW
LRU_BLOCKS = 8
LRU_BW = LRU_WIDTH // LRU_BLOCKS
CONV_W = 4
CONV_LEFT = 2
CHUNK = 64
RG_C = 8.0
D_FF = 5504
N_EXPERTS = 8
TOP_K = 2
D_FF_EXPERT = 2816
N_DENSE = (DEPTH + 1) // 2
N_MOE = DEPTH // 2
EPS = 1e-6
OFF_Q = 0
OFF_K = OFF_Q + GDN_QK
OFF_V = OFF_K + GDN_QK
OFF_Z = OFF_V + GDN_VW
OFF_A = OFF_Z + GDN_VW
OFF_B = OFF_A + 2 * GDN_HEADS
OFF_LX = OFF_B + 2 * GDN_HEADS
OFF_LG = OFF_LX + LRU_WIDTH
IN_COLS = OFF_LG + LRU_WIDTH

kernel_name = 'hybrid_gdn_rglru_dit_step'


def rmsnorm(x, g):
    x32 = x.astype(jnp.float32)
    y = x32 * lax.rsqrt(jnp.mean(x32 * x32, axis=-1, keepdims=True) + EPS)
    return (y * g.astype(jnp.float32)).astype(x.dtype)


def l2norm(x):
    return x * lax.rsqrt(jnp.sum(x * x, axis=-1, keepdims=True) + EPS)


def grid_pos_embed(rows, dim):
    quarter = dim // 4
    freqs = jnp.exp(-math.log(10000.0) * jnp.arange(quarter, dtype=jnp.float32) / quarter)
    r = jnp.repeat(jnp.arange(rows, dtype=jnp.float32), GRID_W)
    col = jnp.tile(jnp.arange(GRID_W, dtype=jnp.float32), rows)
    er = r[:, None] * freqs
    ec = col[:, None] * freqs
    return jnp.concatenate([jnp.sin(er), jnp.cos(er), jnp.sin(ec), jnp.cos(ec)], axis=-1)


def short_conv(x, w, b=None):
    T = x.shape[1]
    xp = jnp.pad(x, ((0, 0), (CONV_LEFT, CONV_W - 1 - CONV_LEFT), (0, 0)))
    y = sum(xp[:, j:j + T] * w[j] for j in range(CONV_W))
    return y if b is None else y + b


def gdn_chunked(q, k, v, g, beta, S0):
    Bn, T, H, DK = q.shape
    DV = v.shape[-1]
    N = T // CHUNK

    def chunks(t):
        return t.reshape(Bn, N, CHUNK, H, t.shape[-1]).transpose(0, 3, 1, 2, 4)

    q = chunks(q * (DK ** -0.5))
    k = chunks(k)
    v = chunks(v)
    beta = chunks(beta[..., None])[..., 0]
    gc = jnp.cumsum(chunks(g[..., None])[..., 0], axis=-1)
    idx = jnp.arange(CHUNK)
    incl = idx[:, None] >= idx[None, :]
    strict = idx[:, None] > idx[None, :]
    decay = jnp.exp(jnp.where(incl, gc[..., :, None] - gc[..., None, :], -jnp.inf))
    kb = k * beta[..., None]
    a_low = jnp.where(strict, jnp.einsum('bhncd,bhnsd->bhncs', kb, k) * decay, 0.0)
    eye = jnp.eye(CHUNK, dtype=a_low.dtype)
    t_inv = lax.linalg.triangular_solve(eye + a_low, jnp.broadcast_to(eye, a_low.shape),
                                        left_side=True, lower=True)
    u = jnp.matmul(t_inv, v * beta[..., None])
    w = jnp.matmul(t_inv, kb * jnp.exp(gc)[..., None])
    qk = jnp.einsum('bhncd,bhnsd->bhncs', q, k) * decay
    qg = q * jnp.exp(gc)[..., None]
    g_last = gc[..., -1]
    kd = k * jnp.exp(g_last[..., None] - gc)[..., None]

    def step(S, xs):
        qg_i, qk_i, u_i, w_i, kd_i, gl_i = xs
        v_new = u_i - jnp.einsum('bhcd,bhde->bhce', w_i, S)
        o_i = jnp.einsum('bhcd,bhde->bhce', qg_i, S) + jnp.einsum('bhcs,bhse->bhce', qk_i, v_new)
        S = S * jnp.exp(gl_i)[..., None, None] + jnp.einsum('bhcd,bhce->bhde', kd_i, v_new)
        return S, o_i

    xs = tuple(jnp.moveaxis(t, 2, 0) for t in (qg, qk, u, w, kd, g_last))
    S_fin, o = lax.scan(step, S0, xs)
    o = o.transpose(1, 0, 3, 2, 4).reshape(Bn, T, H, DV)
    return o, S_fin


def gdn_dir(q, k, v, g, beta, S0, reverse):
    if reverse:
        q, k, v, g, beta = (t[:, ::-1] for t in (q, k, v, g, beta))
    o, S_fin = gdn_chunked(q, k, v, g, beta, S0)
    if reverse:
        o = o[:, ::-1]
    return o, S_fin


def _lru_combine(e1, e2):
    a1, b1 = e1
    a2, b2 = e2
    return a1 * a2, a2 * b1 + b2


def rglru_dir(xl, wa, ba, wi, bi, lam, h0, reverse):
    Bn, T, W = xl.shape
    xb = xl.reshape(Bn, T, LRU_BLOCKS, LRU_BW)
    r = jax.nn.sigmoid(jnp.einsum('btnk,nkj->btnj', xb, wa).reshape(Bn, T, W) + ba)
    i = jax.nn.sigmoid(jnp.einsum('btnk,nkj->btnj', xb, wi).reshape(Bn, T, W) + bi)
    log_a = -RG_C * r * jax.nn.softplus(-lam)
    a = jnp.exp(log_a)
    b = jnp.sqrt(-jnp.expm1(2.0 * log_a)) * (i * xl)
    if reverse:
        a, b = a[:, ::-1], b[:, ::-1]
    a_cum, h = lax.associative_scan(_lru_combine, (a, b), axis=1)
    h = h + a_cum * h0[:, None]
    h_fin = h[:, -1]
    if reverse:
        h = h[:, ::-1]
    return h, h_fin


def mixer(h, lp, S0, h0):
    Bn, T, _ = h.shape
    proj = jnp.matmul(h, lp['w_in']).astype(jnp.float32)
    qkv = jax.nn.silu(short_conv(proj[..., OFF_Q:OFF_Z], lp['gdn_conv_w']))
    q = l2norm(qkv[..., :GDN_QK].reshape(Bn, T, GDN_HEADS, GDN_DK))
    k = l2norm(qkv[..., GDN_QK:2 * GDN_QK].reshape(Bn, T, GDN_HEADS, GDN_DK))
    v = qkv[..., 2 * GDN_QK:].reshape(Bn, T, GDN_HEADS, GDN_DV)
    z = proj[..., OFF_Z:OFF_A].reshape(Bn, T, GDN_HEADS, GDN_DV)
    alpha = proj[..., OFF_A:OFF_B].reshape(Bn, T, 2, GDN_HEADS)
    beta = jax.nn.sigmoid(proj[..., OFF_B:OFF_LX].reshape(Bn, T, 2, GDN_HEADS))
    g = -jnp.exp(lp['gdn_A_log']) * jax.nn.softplus(alpha + lp['gdn_dt_bias'])
    o_f, S_f = gdn_dir(q, k, v, g[:, :, 0], beta[:, :, 0], S0[:, 0], False)
    o_b, S_b = gdn_dir(q, k, v, g[:, :, 1], beta[:, :, 1], S0[:, 1], True)
    o = rmsnorm(o_f + o_b, lp['gdn_norm_g']) * jax.nn.silu(z)
    o = o.reshape(Bn, T, GDN_VW)
    xl = short_conv(proj[..., OFF_LX:OFF_LG], lp['lru_conv_w'], lp['lru_conv_b'])
    gate = jax.nn.gelu(proj[..., OFF_LG:])
    h_f, l_f = rglru_dir(xl, lp['lru_wa'][0], lp['lru_ba'][0], lp['lru_wi'][0], lp['lru_bi'][0],
                         lp['lru_lambda'][0], h0[:, 0], False)
    h_b, l_b = rglru_dir(xl, lp['lru_wa'][1], lp['lru_ba'][1], lp['lru_wi'][1], lp['lru_bi'][1],
                         lp['lru_lambda'][1], h0[:, 1], True)
    yl = (h_f + h_b) * gate
    out = jnp.matmul(jnp.concatenate([o, yl], axis=-1).astype(h.dtype), lp['w_out'])
    return out, jnp.stack([S_f, S_b], axis=1), jnp.stack([l_f, l_b], axis=1)


def swiglu(h, w1, w3, w2):
    return jnp.matmul(jax.nn.silu(jnp.matmul(h, w1)) * jnp.matmul(h, w3), w2)


def moe_swiglu(h, wr, br, w1, w3, w2):
    logits = (jnp.matmul(h, wr) + br).astype(jnp.float32)
    probs = jax.nn.softmax(logits, axis=-1)
    topv, topi = lax.top_k(probs, TOP_K)
    topv = topv / jnp.sum(topv, axis=-1, keepdims=True)
    gates = jnp.sum(jax.nn.one_hot(topi, N_EXPERTS, dtype=jnp.float32) * topv[..., None], axis=-2)
    out = jnp.zeros_like(h)
    for e in range(N_EXPERTS):
        out = out + gates[..., e:e + 1].astype(h.dtype) * swiglu(h, w1[e], w3[e], w2[e])
    return out


def trunk_layer(x, mod, S0, h0, lp, ffn):
    shift1, scale1, gate1, shift2, scale2, gate2 = jnp.split(mod, 6, axis=-1)
    h = rmsnorm(x, lp['norm1_g']) * (1.0 + scale1) + shift1
    m, S_fin, h_fin = mixer(h, lp, S0, h0)
    x = x + gate1 * m
    h = rmsnorm(x, lp['norm2_g']) * (1.0 + scale2) + shift2
    x = x + gate2 * ffn(h)
    return x, S_fin, h_fin


def setup_inputs(seed: int = 0) -> dict:
    key = jax.random.key(seed)
    ks = jax.random.split(key, 40)
    f32 = jnp.float32

    def nrm(k, shape, scale):
        return jax.random.normal(k, shape, f32) * scale

    def unif(k, shape, lo, hi):
        return jax.random.uniform(k, shape, f32, lo, hi)

    D = D_MODEL
    dt = jnp.exp(unif(ks[10], (DEPTH, 2, GDN_HEADS), math.log(1e-3), math.log(1e-1)))
    u = unif(ks[17], (DEPTH, 2, LRU_WIDTH), 0.9, 0.999)
    p = u ** (1.0 / RG_C)
    return {
        'x_prompt': nrm(ks[0], (BATCH, SEQ, D), 1.0),
        'x_sample': nrm(ks[1], (DEC_BATCH, DEC_SEQ, D), 1.0),
        'state_gdn': nrm(ks[2], (DEC_BATCH, DEPTH, 2, GDN_HEADS, GDN_DK, GDN_DV), 0.5),
        'state_lru': nrm(ks[3], (DEC_BATCH, DEPTH, 2, LRU_WIDTH), 1.0),
        'c': nrm(ks[4], (DEC_BATCH, D), 1.0),
        'c_ctx': nrm(ks[5], (D,), 1.0),
        'w_mod': nrm(ks[6], (DEPTH, D, 6 * D), 0.5 * D ** -0.5),
        'b_mod': nrm(ks[7], (DEPTH, 6 * D), 0.02),
        'norm1_g': 1.0 + nrm(ks[8], (DEPTH, D), 0.02),
        'norm2_g': 1.0 + nrm(ks[9], (DEPTH, D), 0.02),
        'w_in': nrm(ks[11], (DEPTH, D, IN_COLS), D ** -0.5),
        'gdn_conv_w': nrm(ks[12], (DEPTH, CONV_W, 2 * GDN_QK + GDN_VW), CONV_W ** -0.5),
        'gdn_A_log': jnp.log(unif(ks[13], (DEPTH, 2, GDN_HEADS), 1.0, 16.0)),
        'gdn_dt_bias': dt + jnp.log(-jnp.expm1(-dt)),
        'gdn_norm_g': 1.0 + nrm(ks[14], (DEPTH, GDN_DV), 0.02),
        'lru_conv_w': nrm(ks[15], (DEPTH, CONV_W, LRU_WIDTH), CONV_W ** -0.5),
        'lru_conv_b': nrm(ks[16], (DEPTH, LRU_WIDTH), 0.02),
        'lru_wa': nrm(ks[18], (DEPTH, 2, LRU_BLOCKS, LRU_BW, LRU_BW), LRU_BW ** -0.5),
        'lru_ba': nrm(ks[19], (DEPTH, 2, LRU_WIDTH), 0.02),
        'lru_wi': nrm(ks[20], (DEPTH, 2, LRU_BLOCKS, LRU_BW, LRU_BW), LRU_BW ** -0.5),
        'lru_bi': nrm(ks[21], (DEPTH, 2, LRU_WIDTH), 0.02),
        'lru_lambda': jnp.log(p) - jnp.log1p(-p),
        'w_out': nrm(ks[22], (DEPTH, MIX_WIDTH, D), MIX_WIDTH ** -0.5),
        'ffd_w1': nrm(ks[23], (N_DENSE, D, D_FF), D ** -0.5),
        'ffd_w3': nrm(ks[24], (N_DENSE, D, D_FF), D ** -0.5),
        'ffd_w2': nrm(ks[25], (N_DENSE, D_FF, D), D_FF ** -0.5),
        'moe_wr': nrm(ks[26], (N_MOE, D, N_EXPERTS), D ** -0.5),
        'moe_br': nrm(ks[27], (N_MOE, N_EXPERTS), 0.01),
        'moe_w1': nrm(ks[28], (N_MOE, N_EXPERTS, D, D_FF_EXPERT), D ** -0.5),
        'moe_w3': nrm(ks[29], (N_MOE, N_EXPERTS, D, D_FF_EXPERT), D ** -0.5),
        'moe_w2': nrm(ks[30], (N_MOE, N_EXPERTS, D_FF_EXPERT, D), D_FF_EXPERT ** -0.5),
        'final_g': 1.0 + nrm(ks[31], (D,), 0.02),
    }


def reference(x_prompt, x_sample, state_gdn, state_lru, c, c_ctx, w_mod, b_mod, norm1_g, norm2_g, w_in,
              gdn_conv_w, gdn_A_log, gdn_dt_bias, gdn_norm_g, lru_conv_w, lru_conv_b, lru_wa, lru_ba,
              lru_wi, lru_bi, lru_lambda, w_out, ffd_w1, ffd_w3, ffd_w2, moe_wr, moe_br, moe_w1, moe_w3,
              moe_w2, final_g):
    rows = x_sample.shape[1] // GRID_W
    xs = x_sample + grid_pos_embed(rows, D_MODEL).astype(x_sample.dtype)[None]
    xp = x_prompt
    bp = x_prompt.shape[0]
    zero_gdn = jnp.zeros((bp, 2, GDN_HEADS, GDN_DK, GDN_DV), jnp.float32)
    zero_lru = jnp.zeros((bp, 2, LRU_WIDTH), jnp.float32)
    new_gdn, new_lru = [], []
    for l in range(DEPTH):
        lp = {
            'norm1_g': norm1_g[l], 'norm2_g': norm2_g[l], 'w_in': w_in[l],
            'gdn_conv_w': gdn_conv_w[l], 'gdn_A_log': gdn_A_log[l], 'gdn_dt_bias': gdn_dt_bias[l],
            'gdn_norm_g': gdn_norm_g[l], 'lru_conv_w': lru_conv_w[l], 'lru_conv_b': lru_conv_b[l],
            'lru_wa': lru_wa[l], 'lru_ba': lru_ba[l], 'lru_wi': lru_wi[l], 'lru_bi': lru_bi[l],
            'lru_lambda': lru_lambda[l], 'w_out': w_out[l],
        }
        j = l // 2
        if l % 2 == 0:
            ffn = functools.partial(swiglu, w1=ffd_w1[j], w3=ffd_w3[j], w2=ffd_w2[j])
        else:
            ffn = functools.partial(moe_swiglu, wr=moe_wr[j], br=moe_br[j], w1=moe_w1[j], w3=moe_w3[j], w2=moe_w2[j])
        mod_ctx = (jnp.matmul(jax.nn.silu(c_ctx), w_mod[l]) + b_mod[l])[None, None]
        mod_lat = (jnp.matmul(jax.nn.silu(c), w_mod[l]) + b_mod[l])[:, None]
        xp, s_g, s_l = trunk_layer(xp, mod_ctx, zero_gdn, zero_lru, lp, ffn)
        new_gdn.append(s_g.astype(x_prompt.dtype))
        new_lru.append(s_l.astype(x_prompt.dtype))
        xs, _, _ = trunk_layer(xs, mod_lat, state_gdn[:, l].astype(jnp.float32),
                               state_lru[:, l].astype(jnp.float32), lp, ffn)
    y_prompt = rmsnorm(xp, final_g)
    y_sample = rmsnorm(xs, final_g)
    new_state_gdn = jnp.stack(new_gdn, axis=1)
    new_state_lru = jnp.stack(new_lru, axis=1)
    return (y_prompt, y_sample, new_state_gdn, new_state_lru)
```

```python
import functools
import math

import jax
import jax.numpy as jnp
from jax import lax
from jax.experimental import pallas as pl
from jax.experimental.pallas import tpu as pltpu

f32 = jnp.float32
bf16 = jnp.bfloat16

D_MODEL = 2048
GRID_W = 64
N_HEADS = 8
HEAD_D = 128
GDN_W = N_HEADS * HEAD_D
LRU_W = 1024
LRU_BLOCKS = 8
LRU_BW = LRU_W // LRU_BLOCKS
CONV_W = 4
CONV_LEFT = 2
CHUNK = 64
RG_C = 8.0
N_EXPERTS = 8
EPS = 1e-6
LANES = 128
SUBLANES = 8

P_Q = 0
P_K = P_Q + GDN_W
P_V = P_K + GDN_W
P_Z = P_V + GDN_W
P_AB = P_Z + GDN_W
P_LX = P_AB + LANES
P_LG = P_LX + LRU_W
P_COLS = P_LG + LRU_W

SEQ_TILE = 256
VMEM_LIMIT = 48 * 1024 * 1024


def _cparams(sem):
    return pltpu.CompilerParams(dimension_semantics=sem, vmem_limit_bytes=VMEM_LIMIT)


def _dot(a, b):
    return jnp.dot(a, b, preferred_element_type=f32)


def _dot_nt(a, b, precision=None):
    return lax.dot_general(a, b, (((1,), (1,)), ((), ())), precision=precision, preferred_element_type=f32)


def _dot_tn(a, b):
    return lax.dot_general(a, b, (((0,), (0,)), ((), ())), preferred_element_type=f32)


def _mod_kernel(c_ref, w_ref, b_ref, o_ref):
    a = jax.nn.silu(c_ref[...]).astype(bf16)
    o_ref[...] = _dot(a, w_ref[...].astype(bf16)) + b_ref[...]


def _modulation(cvec, w_mod, b_mod):
    depth, d, n = w_mod.shape
    tn = 1024
    return pl.pallas_call(
        _mod_kernel,
        grid=(depth, n // tn),
        in_specs=[
            pl.BlockSpec((cvec.shape[0], d), lambda l, j: (0, 0)),
            pl.BlockSpec((None, d, tn), lambda l, j: (l, 0, j)),
            pl.BlockSpec((None, 1, tn), lambda l, j: (l, 0, j)),
        ],
        out_specs=pl.BlockSpec((None, cvec.shape[0], tn), lambda l, j: (l, 0, j)),
        out_shape=jax.ShapeDtypeStruct((depth, cvec.shape[0], n), f32),
        compiler_params=_cparams(("arbitrary", "arbitrary")),
        name="modulation",
    )(cvec, w_mod, b_mod.reshape(depth, 1, n))


class _Path:
    def __init__(self, nseq, seq_len, mod_row0, per_seq_mod):
        self.nseq = nseq
        self.seq_len = seq_len
        self.tokens = nseq * seq_len
        self.mod_row0 = mod_row0
        self.per_seq_mod = per_seq_mod

    def mod_row(self, tile_idx, tile):
        if self.per_seq_mod:
            return self.mod_row0 + tile_idx // (self.seq_len // tile)
        return self.mod_row0


def _mod_spec(path, layer, rows, part, tile, width, col_of=None, grid_rank=1):
    def imap(*idx):
        i = idx[0]
        j = idx[col_of] if col_of is not None else 0
        return ((layer * rows + path.mod_row(i, tile)) * 6 + part, 0, j)
    return pl.BlockSpec((None, 1, width), imap)


def _rms(x, g):
    return x * lax.rsqrt(jnp.mean(x * x, axis=-1, keepdims=True) + EPS) * g


def _norm_mod_kernel(*refs, with_pos, with_router, tile, tiles_per_seq):
    it = iter(refs)
    x_ref, g_ref, scale_ref, shift_ref = next(it), next(it), next(it), next(it)
    pos_ref = next(it) if with_pos else None
    wr_ref, br_ref = (next(it), next(it)) if with_router else (None, None)
    h_ref = next(it)
    xo_ref = next(it) if with_pos else None
    gates_ref = next(it) if with_router else None

    x = x_ref[...]
    if with_pos:
        rows_per_tile = tile // GRID_W
        row0 = (pl.program_id(0) % tiles_per_seq) * rows_per_tile
        half = D_MODEL // 2
        col_part = pos_ref[...]
        pieces = []
        for r in range(rows_per_tile):
            row_part = jnp.broadcast_to(pos_ref[pl.ds(row0 + r, 1), :], (GRID_W, half))
            pieces.append(jnp.concatenate([row_part, col_part], axis=1))
        x = x + jnp.concatenate(pieces, axis=0)
        xo_ref[...] = x
    h = _rms(x, g_ref[...]) * (1.0 + scale_ref[...]) + shift_ref[...]
    h_ref[...] = h.astype(bf16)
    if with_router:
        logits = jnp.dot(h, wr_ref[...], precision=lax.Precision.HIGHEST, preferred_element_type=f32) + br_ref[...]
        lane = lax.broadcasted_iota(jnp.int32, logits.shape, 1)
        valid = lane < N_EXPERTS
        logits = jnp.where(valid, logits, -jnp.inf)
        e = jnp.exp(logits - jnp.max(logits, axis=-1, keepdims=True))
        p = e / jnp.sum(e, axis=-1, keepdims=True)
        p = jnp.where(valid, p, -1.0)
        m1 = jnp.max(p, axis=-1, keepdims=True)
        i1 = jnp.min(jnp.where(p == m1, lane, LANES), axis=-1, keepdims=True)
        p2 = jnp.where(lane == i1, -1.0, p)
        m2 = jnp.max(p2, axis=-1, keepdims=True)
        i2 = jnp.min(jnp.where(p2 == m2, lane, LANES), axis=-1, keepdims=True)
        den = m1 + m2
        gates_ref[...] = jnp.where(lane == i1, m1 / den, 0.0) + jnp.where(lane == i2, m2 / den, 0.0)


def _norm_mod(path, x, g, mod3, layer, rows, part_scale, part_shift, pos=None, router=None):
    tile = SEQ_TILE
    n = path.tokens // tile
    d = D_MODEL
    in_specs = [
        pl.BlockSpec((tile, d), lambda i: (i, 0)),
        pl.BlockSpec((1, d), lambda i: (0, 0)),
        _mod_spec(path, layer, rows, part_scale, tile, d),
        _mod_spec(path, layer, rows, part_shift, tile, d),
    ]
    args = [x, g.reshape(1, d), mod3, mod3]
    out_specs = [pl.BlockSpec((tile, d), lambda i: (i, 0))]
    out_shape = [jax.ShapeDtypeStruct((path.tokens, d), bf16)]
    if pos is not None:
        in_specs.append(pl.BlockSpec(pos.shape, lambda i: (0, 0)))
        args.append(pos)
        out_specs.append(pl.BlockSpec((tile, d), lambda i: (i, 0)))
        out_shape.append(jax.ShapeDtypeStruct((path.tokens, d), f32))
    if router is not None:
        wr, br = router
        in_specs += [pl.BlockSpec((d, LANES), lambda i: (0, 0)), pl.BlockSpec((1, LANES), lambda i: (0, 0))]
        args += [wr, br]
        out_specs.append(pl.BlockSpec((tile, LANES), lambda i: (i, 0)))
        out_shape.append(jax.ShapeDtypeStruct((path.tokens, LANES), f32))
    kern = functools.partial(_norm_mod_kernel, with_pos=pos is not None, with_router=router is not None,
                             tile=tile, tiles_per_seq=path.seq_len // tile)
    return pl.pallas_call(kern, grid=(n,), in_specs=in_specs, out_specs=out_specs, out_shape=out_shape,
                          compiler_params=_cparams(("arbitrary",)), name="norm_mod")(*args)


def _final_norm_kernel(x_ref, g_ref, o_ref):
    o_ref[...] = _rms(x_ref[...], g_ref[...])


def _final_norm(x, g):
    t, d = x.shape
    tile = 512
    return pl.pallas_call(
        _final_norm_kernel, grid=(t // tile,),
        in_specs=[pl.BlockSpec((tile, d), lambda i: (i, 0)), pl.BlockSpec((1, d), lambda i: (0, 0))],
        out_specs=pl.BlockSpec((tile, d), lambda i: (i, 0)),
        out_shape=jax.ShapeDtypeStruct((t, d), f32),
        compiler_params=_cparams(("arbitrary",)), name="final_norm")(x, g.reshape(1, d))


def _mm_kernel(a_ref, w_ref, o_ref):
    o_ref[...] = _dot(a_ref[...], w_ref[...]).astype(o_ref.dtype)


def _matmul(a, w, tm, tn, out_dtype):
    m, k = a.shape
    n = w.shape[1]
    return pl.pallas_call(
        _mm_kernel, grid=(m // tm, n // tn),
        in_specs=[pl.BlockSpec((tm, k), lambda i, j: (i, 0)), pl.BlockSpec((k, tn), lambda i, j: (0, j))],
        out_specs=pl.BlockSpec((tm, tn), lambda i, j: (i, j)),
        out_shape=jax.ShapeDtypeStruct((m, n), out_dtype),
        compiler_params=_cparams(("arbitrary", "arbitrary")), name="matmul")(a, w)


def _out_proj_kernel(a1_ref, a2_ref, w1_ref, w2_ref, x_ref, gate_ref, o_ref):
    acc = _dot(a1_ref[...], w1_ref[...]) + _dot(a2_ref[...], w2_ref[...])
    o_ref[...] = x_ref[...] + gate_ref[...] * acc


def _out_proj(path, a1, a2, w, x, mod3, layer, rows):
    tm, tn = 1024, 512
    m, d = x.shape
    k1, k2 = a1.shape[1], a2.shape[1]
    return pl.pallas_call(
        _out_proj_kernel, grid=(m // tm, d // tn),
        in_specs=[
            pl.BlockSpec((tm, k1), lambda i, j: (i, 0)),
            pl.BlockSpec((tm, k2), lambda i, j: (i, 0)),
            pl.BlockSpec((k1, tn), lambda i, j: (0, j)),
            pl.BlockSpec((k2, tn), lambda i, j: (k1 // k2, j)),
            pl.BlockSpec((tm, tn), lambda i, j: (i, j)),
            _mod_spec(path, layer, rows, 2, tm, tn, col_of=1),
        ],
        out_specs=pl.BlockSpec((tm, tn), lambda i, j: (i, j)),
        out_shape=jax.ShapeDtypeStruct((m, d), f32),
        compiler_params=_cparams(("arbitrary", "arbitrary")), name="out_proj")(a1, a2, w, w, x, mod3)


def _gate_up_kernel(h_ref, w1_ref, w3_ref, o_ref):
    h = h_ref[...]
    a = _dot(h, w1_ref[...])
    b = _dot(h, w3_ref[...])
    o_ref[...] = (jax.nn.silu(a) * b).astype(o_ref.dtype)


def _gate_up(h, w1, w3, tm, tn):
    m, k = h.shape
    n = w1.shape[1]
    return pl.pallas_call(
        _gate_up_kernel, grid=(m // tm, n // tn),
        in_specs=[pl.BlockSpec((tm, k), lambda i, j: (i, 0)),
                  pl.BlockSpec((k, tn), lambda i, j: (0, j)),
                  pl.BlockSpec((k, tn), lambda i, j: (0, j))],
        out_specs=pl.BlockSpec((tm, tn), lambda i, j: (i, j)),
        out_shape=jax.ShapeDtypeStruct((m, n), bf16),
        compiler_params=_cparams(("arbitrary", "arbitrary")), name="gate_up")(h, w1, w3)


def _moe_gate_up_kernel(h_ref, w1_ref, w3_ref, g_ref, o_ref):
    h = h_ref[...]
    a = _dot(h, w1_ref[...])
    b = _dot(h, w3_ref[...])
    o_ref[...] = (g_ref[...] * (jax.nn.silu(a) * b)).astype(o_ref.dtype)


def _moe_gate_up(h, w1, w3, gates_t, tm, tn):
    m, k = h.shape
    ne, _, f = w1.shape
    nj = f // tn
    return pl.pallas_call(
        _moe_gate_up_kernel, grid=(m // tm, ne, nj),
        in_specs=[pl.BlockSpec((tm, k), lambda i, e, j: (i, 0)),
                  pl.BlockSpec((None, k, tn), lambda i, e, j: (e, 0, j)),
                  pl.BlockSpec((None, k, tn), lambda i, e, j: (e, 0, j)),
                  pl.BlockSpec((None, tm, 1), lambda i, e, j: (e, i, 0))],
        out_specs=pl.BlockSpec((tm, tn), lambda i, e, j: (i, e * nj + j)),
        out_shape=jax.ShapeDtypeStruct((m, ne * f), bf16),
        compiler_params=_cparams(("arbitrary", "arbitrary", "arbitrary")), name="moe_gate_up")(h, w1, w3, gates_t)


def _down_kernel(a_ref, w_ref, x_ref, gate_ref, o_ref, acc_ref, *, nk):
    kk = pl.program_id(1)

    @pl.when(kk == 0)
    def _():
        acc_ref[...] = jnp.zeros_like(acc_ref)

    acc_ref[...] += _dot(a_ref[...], w_ref[...])

    @pl.when(kk == nk - 1)
    def _():
        o_ref[...] = x_ref[...] + gate_ref[...] * acc_ref[...]


def _down_proj(path, act, w2, x, mod3, layer, rows, tm, tk):
    m, d = x.shape
    kdim = act.shape[1]
    nk = kdim // tk
    return pl.pallas_call(
        functools.partial(_down_kernel, nk=nk), grid=(m // tm, nk),
        in_specs=[pl.BlockSpec((tm, tk), lambda i, k: (i, k)),
                  pl.BlockSpec((tk, d), lambda i, k: (k, 0)),
                  pl.BlockSpec((tm, d), lambda i, k: (i, 0)),
                  _mod_spec(path, layer, rows, 5, tm, d)],
        out_specs=pl.BlockSpec((tm, d), lambda i, k: (i, 0)),
        out_shape=jax.ShapeDtypeStruct((m, d), f32),
        scratch_shapes=[pltpu.VMEM((tm, d), f32)],
        compiler_params=_cparams(("arbitrary", "arbitrary")), name="down_proj")(act, w2, x, mod3)


def _conv_window(win, w):
    n = win.shape[0] - 2 * SUBLANES
    acc = None
    for j in range(CONV_W):
        off = SUBLANES + j - CONV_LEFT
        term = win[off:off + n] * w[j:j + 1]
        acc = term if acc is None else acc + term
    return acc


def _qkv_prep_kernel(prev_ref, cur_ref, next_ref, w_ref, o_ref, *, tiles_per_seq):
    t = pl.program_id(0) % tiles_per_seq
    kind = pl.program_id(1)
    has_prev = (t > 0).astype(f32)
    has_next = (t < tiles_per_seq - 1).astype(f32)
    scale = jnp.where(kind == 0, HEAD_D ** -0.5, 1.0).astype(f32)
    tile = cur_ref.shape[0]
    for c in range(tile // CHUNK):
        r0 = c * CHUNK
        for hd in range(N_HEADS):
            cols = slice(hd * HEAD_D, (hd + 1) * HEAD_D)
            top = prev_ref[:, cols] * has_prev if c == 0 else cur_ref[r0 - SUBLANES:r0, cols]
            bot = (next_ref[:, cols] * has_next if r0 + CHUNK == tile
                   else cur_ref[r0 + CHUNK:r0 + CHUNK + SUBLANES, cols])
            win = jnp.concatenate([top, cur_ref[r0:r0 + CHUNK, cols], bot], axis=0)
            y = jax.nn.silu(_conv_window(win, w_ref[:, cols]))
            nrm = lax.rsqrt(jnp.sum(y * y, axis=-1, keepdims=True) + EPS) * scale
            y = y * jnp.where(kind < 2, nrm, 1.0)
            o_ref[r0:r0 + CHUNK, cols] = y


def _qkv_prep(path, proj, conv_w):
    tile = SEQ_TILE
    n = path.tokens // tile
    tps = path.seq_len // tile
    rb = tile // SUBLANES
    last = path.tokens // SUBLANES - 1
    return pl.pallas_call(
        functools.partial(_qkv_prep_kernel, tiles_per_seq=tps), grid=(n, 3),
        in_specs=[
            pl.BlockSpec((SUBLANES, GDN_W), lambda i, j: (jnp.maximum(i * rb - 1, 0), j)),
            pl.BlockSpec((tile, GDN_W), lambda i, j: (i, j)),
            pl.BlockSpec((SUBLANES, GDN_W), lambda i, j: (jnp.minimum((i + 1) * rb, last), j)),
            pl.BlockSpec((CONV_W, GDN_W), lambda i, j: (0, j)),
        ],
        out_specs=pl.BlockSpec((tile, GDN_W), lambda i, j: (i, j)),
        out_shape=jax.ShapeDtypeStruct((path.tokens, 3 * GDN_W), f32),
        compiler_params=_cparams(("arbitrary", "arbitrary")), name="qkv_prep")(proj, proj, proj, conv_w)


INV_BLOCK = 16


def _hdot(a, b):
    return jnp.dot(a, b, precision=lax.Precision.HIGHEST, preferred_element_type=f32)


def _unit_tri_inverse(a, eye, diag_blk):
    p = -jnp.where(diag_blk, a, 0.0)
    x = eye + p
    span = 2
    while span < INV_BLOCK:
        p = _hdot(p, p)
        x = x + _hdot(x, p)
        span *= 2
    ia = eye + a
    span = INV_BLOCK
    while span < a.shape[0]:
        x = x + _hdot(x, eye - _hdot(ia, x))
        span *= 2
    return x


def _gdn_kernel(*refs, reverse, tiles_per_seq, zero_init, emit_state):
    it = iter(refs)
    q_ref, k_ref, v_ref, ab_ref, prm_ref = next(it), next(it), next(it), next(it), next(it)
    s0_ref = None if zero_init else next(it)
    o_ref = next(it)
    sfin_ref = next(it) if emit_state else None
    s_scr = next(it)

    t = pl.program_id(1)
    nchunks = q_ref.shape[0] // CHUNK

    @pl.when(t == 0)
    def _():
        if zero_init:
            s_scr[...] = jnp.zeros_like(s_scr)
        else:
            s_scr[...] = s0_ref[...]

    ri = lax.broadcasted_iota(jnp.int32, (CHUNK, CHUNK), 0)
    ci = lax.broadcasted_iota(jnp.int32, (CHUNK, CHUNK), 1)
    incl = (ri <= ci) if reverse else (ri >= ci)
    strict = (ri < ci) if reverse else (ri > ci)
    eye = (ri == ci).astype(f32)
    diag_blk = (ri // INV_BLOCK) == (ci // INV_BLOCK)
    tri = incl.astype(f32)
    last_row = 0 if reverse else CHUNK - 1
    dir_off = N_HEADS if reverse else 0
    neg_a = -jnp.exp(prm_ref[0:1, :])
    dt_bias = prm_ref[1:2, :]

    def chunk_body(c, carry):
        cc = (nchunks - 1 - c) if reverse else c
        r0 = pl.multiple_of(cc * CHUNK, CHUNK)
        ab = ab_ref[pl.ds(r0, CHUNK), :]
        g = neg_a * jax.nn.softplus(ab + dt_bias)
        beta = jax.nn.sigmoid(ab)
        gc = jnp.dot(tri, g, precision=lax.Precision.HIGHEST, preferred_element_type=f32)
        g_last = gc[last_row:last_row + 1, :]
        e_gc = jnp.exp(gc)
        e_rest = jnp.exp(g_last - gc)
        e_last = jnp.exp(g_last)
        gc_t = gc.T
        for hd in range(N_HEADS):
            cols = slice(hd * HEAD_D, (hd + 1) * HEAD_D)
            ga = dir_off + hd
            gb = 2 * N_HEADS + dir_off + hd
            gcol = gc[:, ga:ga + 1]
            grow = gc_t[ga:ga + 1, :]
            decay = jnp.exp(jnp.where(incl, gcol - grow, -jnp.inf))
            b_col = beta[:, gb:gb + 1]
            qh = q_ref[pl.ds(r0, CHUNK), cols]
            kh = k_ref[pl.ds(r0, CHUNK), cols]
            vh = v_ref[pl.ds(r0, CHUNK), cols]
            kb = kh * b_col
            khb = kh.astype(bf16)
            a_low = jnp.where(strict, _dot_nt(kb.astype(bf16), khb) * decay, 0.0)
            tb = _unit_tri_inverse(a_low, eye, diag_blk).astype(bf16)
            u = _dot(tb, (vh * b_col).astype(bf16))
            w = _dot(tb, (kb * e_gc[:, ga:ga + 1]).astype(bf16))
            qk = _dot_nt(qh.astype(bf16), khb) * decay
            qg = qh * e_gc[:, ga:ga + 1]
            kd = kh * e_rest[:, ga:ga + 1]
            s_h = s_scr[hd]
            sb = s_h.astype(bf16)
            v_new = u - _dot(w.astype(bf16), sb)
            vnb = v_new.astype(bf16)
            o_ref[pl.ds(r0, CHUNK), cols] = _dot(qg.astype(bf16), sb) + _dot(qk.astype(bf16), vnb)
            s_scr[hd] = s_h * e_last[:, ga:ga + 1] + _dot_tn(kd.astype(bf16), vnb)
        return carry

    lax.fori_loop(0, nchunks, chunk_body, 0)

    if emit_state:
        @pl.when(t == tiles_per_seq - 1)
        def _():
            sfin_ref[...] = s_scr[...]


def _gdn_scan(path, qkv, proj, prm, s0, layer, reverse, emit_state):
    tile = SEQ_TILE
    tps = path.seq_len // tile
    d = 1 if reverse else 0

    def tok(s, t):
        return s * tps + ((tps - 1 - t) if reverse else t)

    in_specs = [
        pl.BlockSpec((tile, GDN_W), lambda s, t: (tok(s, t), 0)),
        pl.BlockSpec((tile, GDN_W), lambda s, t: (tok(s, t), 1)),
        pl.BlockSpec((tile, GDN_W), lambda s, t: (tok(s, t), 2)),
        pl.BlockSpec((tile, LANES), lambda s, t: (tok(s, t), P_AB // LANES)),
        pl.BlockSpec((2, LANES), lambda s, t: (0, 0)),
    ]
    args = [qkv, qkv, qkv, proj, prm]
    if s0 is not None:
        in_specs.append(pl.BlockSpec((None, None, None, N_HEADS, HEAD_D, HEAD_D),
                                     lambda s, t: (s, layer, d, 0, 0, 0)))
        args.append(s0)
    out_specs = [pl.BlockSpec((tile, GDN_W), lambda s, t: (tok(s, t), 0))]
    out_shape = [jax.ShapeDtypeStruct((path.tokens, GDN_W), f32)]
    if emit_state:
        out_specs.append(pl.BlockSpec((None, N_HEADS, HEAD_D, HEAD_D), lambda s, t: (s, 0, 0, 0)))
        out_shape.append(jax.ShapeDtypeStruct((path.nseq, N_HEADS, HEAD_D, HEAD_D), f32))
    kern = functools.partial(_gdn_kernel, reverse=reverse, tiles_per_seq=tps, zero_init=s0 is None,
                             emit_state=emit_state)
    return pl.pallas_call(
        kern, grid=(path.nseq, tps), in_specs=in_specs, out_specs=out_specs, out_shape=out_shape,
        scratch_shapes=[pltpu.VMEM((N_HEADS, HEAD_D, HEAD_D), f32)],
        compiler_params=_cparams(("arbitrary", "arbitrary")), name="gdn_scan")(*args)


def _gdn_combine_kernel(of_ref, ob_ref, z_ref, g_ref, o_ref):
    g = g_ref[...]
    for hd in range(N_HEADS):
        cols = slice(hd * HEAD_D, (hd + 1) * HEAD_D)
        y = _rms(of_ref[:, cols] + ob_ref[:, cols], g)
        o_ref[:, cols] = (y * jax.nn.silu(z_ref[:, cols])).astype(o_ref.dtype)


def _gdn_combine(o_f, o_b, proj, g):
    t = o_f.shape[0]
    tile = 512
    return pl.pallas_call(
        _gdn_combine_kernel, grid=(t // tile,),
        in_specs=[pl.BlockSpec((tile, GDN_W), lambda i: (i, 0)),
                  pl.BlockSpec((tile, GDN_W), lambda i: (i, 0)),
                  pl.BlockSpec((tile, GDN_W), lambda i: (i, P_Z // GDN_W)),
                  pl.BlockSpec((1, HEAD_D), lambda i: (0, 0))],
        out_specs=pl.BlockSpec((tile, GDN_W), lambda i: (i, 0)),
        out_shape=jax.ShapeDtypeStruct((t, GDN_W), bf16),
        compiler_params=_cparams(("arbitrary",)), name="gdn_combine")(o_f, o_b, proj, g.reshape(1, HEAD_D))


def _lru_kernel(*refs, zero_init, emit_state):
    it = iter(refs)
    lx_ref, lg_ref, cw_ref, cb_ref = next(it), next(it), next(it), next(it)
    wa_ref, wi_ref, ba_ref, bi_ref, lam_ref = next(it), next(it), next(it), next(it), next(it)
    h0_ref = None if zero_init else next(it)
    y_ref = next(it)
    hfin_ref = next(it) if emit_state else None
    a_scr, b_scr = next(it), next(it)

    seq = lx_ref.shape[0]
    blk = 256
    nblk = seq // blk
    w_cat = jnp.concatenate([wa_ref[0], wi_ref[0], wa_ref[1], wi_ref[1]], axis=1).astype(bf16)
    bias_cat = jnp.concatenate([ba_ref[0:1], bi_ref[0:1], ba_ref[1:2], bi_ref[1:2]], axis=1)
    sp = jax.nn.softplus(-lam_ref[...])
    cw = cw_ref[...]
    cb = cb_ref[...]

    def gates_body(i, carry):
        r0 = pl.multiple_of(i * blk, blk)
        r_top = pl.multiple_of(jnp.maximum(r0 - SUBLANES, 0), SUBLANES)
        r_bot = pl.multiple_of(jnp.minimum(r0 + blk, seq - SUBLANES), SUBLANES)
        top = lx_ref[pl.ds(r_top, SUBLANES), :] * (i > 0).astype(f32)
        bot = lx_ref[pl.ds(r_bot, SUBLANES), :] * (i < nblk - 1).astype(f32)
        win = jnp.concatenate([top, lx_ref[pl.ds(r0, blk), :], bot], axis=0)
        xl = _conv_window(win, cw) + cb
        pre = _dot(xl.astype(bf16), w_cat) + bias_cat
        for d in range(2):
            r = jax.nn.sigmoid(pre[:, (2 * d) * LRU_BW:(2 * d + 1) * LRU_BW])
            ig = jax.nn.sigmoid(pre[:, (2 * d + 1) * LRU_BW:(2 * d + 2) * LRU_BW])
            a = jnp.exp(-RG_C * r * sp[d:d + 1])
            a_scr[d, pl.ds(r0, blk), :] = a
            b_scr[d, pl.ds(r0, blk), :] = jnp.sqrt(1.0 - a * a) * (ig * xl)
        return carry

    lax.fori_loop(0, nblk, gates_body, 0)

    row = lax.broadcasted_iota(jnp.int32, (CHUNK, LRU_BW), 0)
    nchunks = seq // CHUNK

    def chunk_scan(a, b, h_prev, reverse):
        s = 1
        while s < CHUNK:
            if reverse:
                keep = row < CHUNK - s
                a_sh = jnp.where(keep, pltpu.roll(a, CHUNK - s, 0), 1.0)
                b_sh = jnp.where(keep, pltpu.roll(b, CHUNK - s, 0), 0.0)
            else:
                keep = row >= s
                a_sh = jnp.where(keep, pltpu.roll(a, s, 0), 1.0)
                b_sh = jnp.where(keep, pltpu.roll(b, s, 0), 0.0)
            b = a * b_sh + b
            a = a * a_sh
            s *= 2
        return b + a * h_prev

    def scan_body(c, carry):
        hf, hb = carry
        rf = pl.multiple_of(c * CHUNK, CHUNK)
        rb = pl.multiple_of((nchunks - 1 - c) * CHUNK, CHUNK)
        h_f = chunk_scan(a_scr[0, pl.ds(rf, CHUNK), :], b_scr[0, pl.ds(rf, CHUNK), :], hf, False)
        h_b = chunk_scan(a_scr[1, pl.ds(rb, CHUNK), :], b_scr[1, pl.ds(rb, CHUNK), :], hb, True)
        b_scr[0, pl.ds(rf, CHUNK), :] = h_f
        b_scr[1, pl.ds(rb, CHUNK), :] = h_b
        return h_f[CHUNK - 1:CHUNK, :], h_b[0:1, :]

    if zero_init:
        init = (jnp.zeros((1, LRU_BW), f32), jnp.zeros((1, LRU_BW), f32))
    else:
        init = (h0_ref[0:1, :], h0_ref[1:2, :])
    hf, hb = lax.fori_loop(0, nchunks, scan_body, init)
    if emit_state:
        hfin_ref[0:1, :] = hf
        hfin_ref[1:2, :] = hb

    def out_body(i, carry):
        r0 = pl.multiple_of(i * blk, blk)
        h = b_scr[0, pl.ds(r0, blk), :] + b_scr[1, pl.ds(r0, blk), :]
        y_ref[pl.ds(r0, blk), :] = (h * jax.nn.gelu(lg_ref[pl.ds(r0, blk), :])).astype(y_ref.dtype)
        return carry

    lax.fori_loop(0, nblk, out_body, 0)


def _lru(path, proj, lp, h0, layer, emit_state):
    seq = path.seq_len
    proj3 = proj.reshape(path.nseq, seq, P_COLS)
    bx, bg = P_LX // LRU_BW, P_LG // LRU_BW
    in_specs = [
        pl.BlockSpec((None, seq, LRU_BW), lambda s, b: (s, 0, bx + b)),
        pl.BlockSpec((None, seq, LRU_BW), lambda s, b: (s, 0, bg + b)),
        pl.BlockSpec((CONV_W, LRU_BW), lambda s, b: (0, b)),
        pl.BlockSpec((1, LRU_BW), lambda s, b: (0, b)),
        pl.BlockSpec((2, None, LRU_BW, LRU_BW), lambda s, b: (0, b, 0, 0)),
        pl.BlockSpec((2, None, LRU_BW, LRU_BW), lambda s, b: (0, b, 0, 0)),
        pl.BlockSpec((2, LRU_BW), lambda s, b: (0, b)),
        pl.BlockSpec((2, LRU_BW), lambda s, b: (0, b)),
        pl.BlockSpec((2, LRU_BW), lambda s, b: (0, b)),
    ]
    args = [proj3, proj3, lp['lru_conv_w'], lp['lru_conv_b'].reshape(1, LRU_W), lp['lru_wa'], lp['lru_wi'],
            lp['lru_ba'], lp['lru_bi'], lp['lru_lambda']]
    if h0 is not None:
        in_specs.append(pl.BlockSpec((None, None, 2, LRU_BW), lambda s, b: (s, layer, 0, b)))
        args.append(h0)
    out_specs = [pl.BlockSpec((None, seq, LRU_BW), lambda s, b: (s, 0, b))]
    out_shape = [jax.ShapeDtypeStruct((path.nseq, seq, LRU_W), bf16)]
    if emit_state:
        out_specs.append(pl.BlockSpec((None, 2, LRU_BW), lambda s, b: (s, 0, b)))
        out_shape.append(jax.ShapeDtypeStruct((path.nseq, 2, LRU_W), f32))
    kern = functools.partial(_lru_kernel, zero_init=h0 is None, emit_state=emit_state)
    outs = pl.pallas_call(
        kern, grid=(path.nseq, LRU_BLOCKS), in_specs=in_specs, out_specs=out_specs, out_shape=out_shape,
        scratch_shapes=[pltpu.VMEM((2, seq, LRU_BW), f32), pltpu.VMEM((2, seq, LRU_BW), f32)],
        compiler_params=_cparams(("arbitrary", "arbitrary")), name="rglru")(*args)
    y = outs[0].reshape(path.tokens, LRU_W)
    return (y, outs[1]) if emit_state else (y, None)


def _pad_cols(w, n):
    return jnp.pad(w, ((0, 0),) * (w.ndim - 1) + ((0, n - w.shape[-1]),))


def _pos_table():
    quarter = D_MODEL // 4
    freqs = jnp.exp(-math.log(10000.0) * jnp.arange(quarter, dtype=f32) / quarter)
    e = jnp.arange(GRID_W, dtype=f32)[:, None] * freqs
    return jnp.concatenate([jnp.sin(e), jnp.cos(e)], axis=-1)


def _mixer_and_ffn(path, x, layer, rows, mod3, lp, ffn, s0_gdn, h0_lru, emit_state, pos):
    if pos is not None:
        h, x = _norm_mod(path, x, lp['norm1_g'], mod3, layer, rows, 1, 0, pos=pos)
    else:
        (h,) = _norm_mod(path, x, lp['norm1_g'], mod3, layer, rows, 1, 0)
    proj = _matmul(h, lp['w_in'], 1024, 896, f32)
    qkv = _qkv_prep(path, proj, lp['gdn_conv_w'])
    outs_f = _gdn_scan(path, qkv, proj, lp['gdn_prm'], s0_gdn, layer, False, emit_state)
    outs_b = _gdn_scan(path, qkv, proj, lp['gdn_prm'], s0_gdn, layer, True, emit_state)
    o_mix = _gdn_combine(outs_f[0], outs_b[0], proj, lp['gdn_norm_g'])
    y_lru, h_fin = _lru(path, proj, lp, h0_lru, layer, emit_state)
    x = _out_proj(path, o_mix, y_lru, lp['w_out'], x, mod3, layer, rows)
    if ffn['kind'] == 'dense':
        (h2,) = _norm_mod(path, x, lp['norm2_g'], mod3, layer, rows, 4, 3)
        act = _gate_up(h2, ffn['w1'], ffn['w3'], 1024, 512)
        x = _down_proj(path, act, ffn['w2'], x, mod3, layer, rows, 512, ffn['w2'].shape[0] // 4)
    else:
        h2, gates = _norm_mod(path, x, lp['norm2_g'], mod3, layer, rows, 4, 3, router=(ffn['wr'], ffn['br']))
        gates_t = jnp.transpose(gates[:, :N_EXPERTS])[:, :, None]
        act = _moe_gate_up(h2, ffn['w1'], ffn['w3'], gates_t, 1024, 256)
        x = _down_proj(path, act, ffn['w2'], x, mod3, layer, rows, 512, 1408)
    s_fin = jnp.stack([outs_f[1], outs_b[1]], axis=1) if emit_state else None
    return x, s_fin, h_fin


def kernel(x_prompt, x_sample, state_gdn, state_lru, c, c_ctx, w_mod, b_mod, norm1_g, norm2_g, w_in, gdn_conv_w, gdn_A_log, gdn_dt_bias, gdn_norm_g, lru_conv_w, lru_conv_b, lru_wa, lru_ba, lru_wi, lru_bi, lru_lambda, w_out, ffd_w1, ffd_w3, ffd_w2, moe_wr, moe_br, moe_w1, moe_w3, moe_w2, final_g):
    depth = w_mod.shape[0]
    bp, sp_len, d = x_prompt.shape
    bs, ss_len, _ = x_sample.shape
    ctx = _Path(bp, sp_len, 0, False)
    lat = _Path(bs, ss_len, 1, True)

    rows = SUBLANES * ((1 + bs + SUBLANES - 1) // SUBLANES)
    cvec = jnp.zeros((rows, d), f32).at[0].set(c_ctx).at[1:1 + bs].set(c)
    mod3 = _modulation(cvec, w_mod, b_mod).reshape(depth * rows * 6, 1, d)

    pos = _pos_table()
    xp = x_prompt.reshape(ctx.tokens, d)
    xs = x_sample.reshape(lat.tokens, d)
    new_gdn, new_lru = [], []
    for l in range(depth):
        w_in_l = w_in[l].astype(bf16)
        n_ab = 4 * N_HEADS
        w_in_p = jnp.concatenate([w_in_l[:, :P_AB + n_ab], jnp.zeros((d, LANES - n_ab), bf16),
                                  w_in_l[:, P_AB + n_ab:]], axis=1)
        prm = jnp.stack([_pad_cols(gdn_A_log[l].reshape(1, -1), LANES)[0],
                         _pad_cols(gdn_dt_bias[l].reshape(1, -1), LANES)[0]])
        lp = {
            'norm1_g': norm1_g[l], 'norm2_g': norm2_g[l], 'w_in': w_in_p, 'gdn_conv_w': gdn_conv_w[l],
            'gdn_prm': prm, 'gdn_norm_g': gdn_norm_g[l], 'lru_conv_w': lru_conv_w[l], 'lru_conv_b': lru_conv_b[l],
            'lru_wa': lru_wa[l], 'lru_ba': lru_ba[l], 'lru_wi': lru_wi[l], 'lru_bi': lru_bi[l],
            'lru_lambda': lru_lambda[l], 'w_out': w_out[l].astype(bf16),
        }
        j = l // 2
        if l % 2 == 0:
            f = ffd_w1.shape[2]
            fp = 512 * ((f + 511) // 512)
            ffn = {'kind': 'dense', 'w1': _pad_cols(ffd_w1[j].astype(bf16), fp),
                   'w3': _pad_cols(ffd_w3[j].astype(bf16), fp),
                   'w2': jnp.pad(ffd_w2[j].astype(bf16), ((0, fp - f), (0, 0)))}
        else:
            ne, _, fe = moe_w1.shape[1:]
            ffn = {'kind': 'moe', 'wr': _pad_cols(moe_wr[j], LANES), 'br': _pad_cols(moe_br[j].reshape(1, -1), LANES),
                   'w1': moe_w1[j].astype(bf16), 'w3': moe_w3[j].astype(bf16),
                   'w2': moe_w2[j].astype(bf16).reshape(ne * fe, d)}
        xp, s_g, s_l = _mixer_and_ffn(ctx, xp, l, rows, mod3, lp, ffn, None, None, True, None)
        new_gdn.append(s_g)
        new_lru.append(s_l)
        xs, _, _ = _mixer_and_ffn(lat, xs, l, rows, mod3, lp, ffn, state_gdn, state_lru, False,
                                  pos if l == 0 else None)
    y_prompt = _final_norm(xp, final_g).reshape(bp, sp_len, d)
    y_sample = _final_norm(xs, final_g).reshape(bs, ss_len, d)
    return (y_prompt, y_sample, jnp.stack(new_gdn, axis=1), jnp.stack(new_lru, axis=1))
```

```python
import functools
import math

import jax
import jax.numpy as jnp
from jax import lax
from jax.experimental import pallas as pl
from jax.experimental.pallas import tpu as pltpu

f32 = jnp.float32
bf16 = jnp.bfloat16

D_MODEL = 2048
GRID_W = 64
N_HEADS = 8
HEAD_D = 128
GDN_W = N_HEADS * HEAD_D
LRU_W = 1024
LRU_BLOCKS = 8
LRU_BW = LRU_W // LRU_BLOCKS
CONV_W = 4
CONV_LEFT = 2
CHUNK = 64
RG_C = 8.0
N_EXPERTS = 8
EPS = 1e-6
LANES = 128
SUBLANES = 8

P_Q = 0
P_K = P_Q + GDN_W
P_V = P_K + GDN_W
P_Z = P_V + GDN_W
P_AB = P_Z + GDN_W
P_LX = P_AB + LANES
P_LG = P_LX + LRU_W
P_COLS = P_LG + LRU_W

SEQ_TILE = 256
VMEM_LIMIT = 48 * 1024 * 1024


def _cparams(sem):
    return pltpu.CompilerParams(dimension_semantics=sem, vmem_limit_bytes=VMEM_LIMIT)


def _dot(a, b):
    return jnp.dot(a, b, preferred_element_type=f32)


def _dot_nt(a, b, precision=None):
    return lax.dot_general(a, b, (((1,), (1,)), ((), ())), precision=precision, preferred_element_type=f32)


def _dot_tn(a, b):
    return lax.dot_general(a, b, (((0,), (0,)), ((), ())), preferred_element_type=f32)


def _mod_kernel(c_ref, w_ref, b_ref, o_ref):
    a = jax.nn.silu(c_ref[...]).astype(bf16)
    o_ref[...] = _dot(a, w_ref[...].astype(bf16)) + b_ref[...]


def _modulation(cvec, w_mod, b_mod):
    depth, d, n = w_mod.shape
    tn = 1024
    return pl.pallas_call(
        _mod_kernel,
        grid=(depth, n // tn),
        in_specs=[
            pl.BlockSpec((cvec.shape[0], d), lambda l, j: (0, 0)),
            pl.BlockSpec((None, d, tn), lambda l, j: (l, 0, j)),
            pl.BlockSpec((None, 1, tn), lambda l, j: (l, 0, j)),
        ],
        out_specs=pl.BlockSpec((None, cvec.shape[0], tn), lambda l, j: (l, 0, j)),
        out_shape=jax.ShapeDtypeStruct((depth, cvec.shape[0], n), f32),
        compiler_params=_cparams(("arbitrary", "arbitrary")),
        name="modulation",
    )(cvec, w_mod, b_mod.reshape(depth, 1, n))


class _Path:
    def __init__(self, nseq, seq_len, mod_row0, per_seq_mod):
        self.nseq = nseq
        self.seq_len = seq_len
        self.tokens = nseq * seq_len
        self.mod_row0 = mod_row0
        self.per_seq_mod = per_seq_mod

    def mod_row(self, tile_idx, tile):
        if self.per_seq_mod:
            return self.mod_row0 + tile_idx // (self.seq_len // tile)
        return self.mod_row0


def _mod_spec(path, layer, rows, part, tile, width, col_of=None, grid_rank=1):
    def imap(*idx):
        i = idx[0]
        j = idx[col_of] if col_of is not None else 0
        return ((layer * rows + path.mod_row(i, tile)) * 6 + part, 0, j)
    return pl.BlockSpec((None, 1, width), imap)


def _rms(x, g):
    return x * lax.rsqrt(jnp.mean(x * x, axis=-1, keepdims=True) + EPS) * g


def _norm_mod_kernel(*refs, with_pos, with_router, tile, tiles_per_seq):
    it = iter(refs)
    x_ref, g_ref, scale_ref, shift_ref = next(it), next(it), next(it), next(it)
    pos_ref = next(it) if with_pos else None
    wr_ref, br_ref = (next(it), next(it)) if with_router else (None, None)
    h_ref = next(it)
    xo_ref = next(it) if with_pos else None
    gates_ref = next(it) if with_router else None

    x = x_ref[...]
    if with_pos:
        rows_per_tile = tile // GRID_W
        row0 = (pl.program_id(0) % tiles_per_seq) * rows_per_tile
        half = D_MODEL // 2
        col_part = pos_ref[...]
        pieces = []
        for r in range(rows_per_tile):
            row_part = jnp.broadcast_to(pos_ref[pl.ds(row0 + r, 1), :], (GRID_W, half))
            pieces.append(jnp.concatenate([row_part, col_part], axis=1))
        x = x + jnp.concatenate(pieces, axis=0)
        xo_ref[...] = x
    h = _rms(x, g_ref[...]) * (1.0 + scale_ref[...]) + shift_ref[...]
    h_ref[...] = h.astype(bf16)
    if with_router:
        logits = jnp.dot(h, wr_ref[...], precision=lax.Precision.HIGHEST, preferred_element_type=f32) + br_ref[...]
        lane = lax.broadcasted_iota(jnp.int32, logits.shape, 1)
        valid = lane < N_EXPERTS
        logits = jnp.where(valid, logits, -jnp.inf)
        e = jnp.exp(logits - jnp.max(logits, axis=-1, keepdims=True))
        p = e / jnp.sum(e, axis=-1, keepdims=True)
        p = jnp.where(valid, p, -1.0)
        m1 = jnp.max(p, axis=-1, keepdims=True)
        i1 = jnp.min(jnp.where(p == m1, lane, LANES), axis=-1, keepdims=True)
        p2 = jnp.where(lane == i1, -1.0, p)
        m2 = jnp.max(p2, axis=-1, keepdims=True)
        i2 = jnp.min(jnp.where(p2 == m2, lane, LANES), axis=-1, keepdims=True)
        den = m1 + m2
        gates_ref[...] = jnp.where(lane == i1, m1 / den, 0.0) + jnp.where(lane == i2, m2 / den, 0.0)


def _norm_mod(path, x, g, mod3, layer, rows, part_scale, part_shift, pos=None, router=None):
    tile = SEQ_TILE
    n = path.tokens // tile
    d = D_MODEL
    in_specs = [
        pl.BlockSpec((tile, d), lambda i: (i, 0)),
        pl.BlockSpec((1, d), lambda i: (0, 0)),
        _mod_spec(path, layer, rows, part_scale, tile, d),
        _mod_spec(path, layer, rows, part_shift, tile, d),
    ]
    args = [x, g.reshape(1, d), mod3, mod3]
    out_specs = [pl.BlockSpec((tile, d), lambda i: (i, 0))]
    out_shape = [jax.ShapeDtypeStruct((path.tokens, d), bf16)]
    if pos is not None:
        in_specs.append(pl.BlockSpec(pos.shape, lambda i: (0, 0)))
        args.append(pos)
        out_specs.append(pl.BlockSpec((tile, d), lambda i: (i, 0)))
        out_shape.append(jax.ShapeDtypeStruct((path.tokens, d), f32))
    if router is not None:
        wr, br = router
        in_specs += [pl.BlockSpec((d, LANES), lambda i: (0, 0)), pl.BlockSpec((1, LANES), lambda i: (0, 0))]
        args += [wr, br]
        out_specs.append(pl.BlockSpec((tile, LANES), lambda i: (i, 0)))
        out_shape.append(jax.ShapeDtypeStruct((path.tokens, LANES), f32))
    kern = functools.partial(_norm_mod_kernel, with_pos=pos is not None, with_router=router is not None,
                             tile=tile, tiles_per_seq=path.seq_len // tile)
    return pl.pallas_call(kern, grid=(n,), in_specs=in_specs, out_specs=out_specs, out_shape=out_shape,
                          compiler_params=_cparams(("arbitrary",)), name="norm_mod")(*args)


def _final_norm_kernel(x_ref, g_ref, o_ref):
    o_ref[...] = _rms(x_ref[...], g_ref[...])


def _final_norm(x, g):
    t, d = x.shape
    tile = 512
    return pl.pallas_call(
        _final_norm_kernel, grid=(t // tile,),
        in_specs=[pl.BlockSpec((tile, d), lambda i: (i, 0)), pl.BlockSpec((1, d), lambda i: (0, 0))],
        out_specs=pl.BlockSpec((tile, d), lambda i: (i, 0)),
        out_shape=jax.ShapeDtypeStruct((t, d), f32),
        compiler_params=_cparams(("arbitrary",)), name="final_norm")(x, g.reshape(1, d))


def _mm_kernel(a_ref, w_ref, o_ref):
    o_ref[...] = _dot(a_ref[...], w_ref[...]).astype(o_ref.dtype)


def _matmul(a, w, tm, tn, out_dtype):
    m, k = a.shape
    n = w.shape[1]
    return pl.pallas_call(
        _mm_kernel, grid=(m // tm, n // tn),
        in_specs=[pl.BlockSpec((tm, k), lambda i, j: (i, 0)), pl.BlockSpec((k, tn), lambda i, j: (0, j))],
        out_specs=pl.BlockSpec((tm, tn), lambda i, j: (i, j)),
        out_shape=jax.ShapeDtypeStruct((m, n), out_dtype),
        compiler_params=_cparams(("arbitrary", "arbitrary")), name="matmul")(a, w)


def _out_proj_kernel(a1_ref, a2_ref, w1_ref, w2_ref, x_ref, gate_ref, o_ref):
    acc = _dot(a1_ref[...], w1_ref[...]) + _dot(a2_ref[...], w2_ref[...])
    o_ref[...] = x_ref[...] + gate_ref[...] * acc


def _out_proj(path, a1, a2, w, x, mod3, layer, rows):
    tm, tn = 1024, 512
    m, d = x.shape
    k1, k2 = a1.shape[1], a2.shape[1]
    return pl.pallas_call(
        _out_proj_kernel, grid=(m // tm, d // tn),
        in_specs=[
            pl.BlockSpec((tm, k1), lambda i, j: (i, 0)),
            pl.BlockSpec((tm, k2), lambda i, j: (i, 0)),
            pl.BlockSpec((k1, tn), lambda i, j: (0, j)),
            pl.BlockSpec((k2, tn), lambda i, j: (k1 // k2, j)),
            pl.BlockSpec((tm, tn), lambda i, j: (i, j)),
            _mod_spec(path, layer, rows, 2, tm, tn, col_of=1),
        ],
        out_specs=pl.BlockSpec((tm, tn), lambda i, j: (i, j)),
        out_shape=jax.ShapeDtypeStruct((m, d), f32),
        compiler_params=_cparams(("arbitrary", "arbitrary")), name="out_proj")(a1, a2, w, w, x, mod3)


def _gate_up_kernel(h_ref, w1_ref, w3_ref, o_ref):
    h = h_ref[...]
    a = _dot(h, w1_ref[...])
    b = _dot(h, w3_ref[...])
    o_ref[...] = (jax.nn.silu(a) * b).astype(o_ref.dtype)


def _gate_up(h, w1, w3, tm, tn):
    m, k = h.shape
    n = w1.shape[1]
    return pl.pallas_call(
        _gate_up_kernel, grid=(m // tm, n // tn),
        in_specs=[pl.BlockSpec((tm, k), lambda i, j: (i, 0)),
                  pl.BlockSpec((k, tn), lambda i, j: (0, j)),
                  pl.BlockSpec((k, tn), lambda i, j: (0, j))],
        out_specs=pl.BlockSpec((tm, tn), lambda i, j: (i, j)),
        out_shape=jax.ShapeDtypeStruct((m, n), bf16),
        compiler_params=_cparams(("arbitrary", "arbitrary")), name="gate_up")(h, w1, w3)


def _moe_gate_up_kernel(h_ref, w1_ref, w3_ref, g_ref, o_ref):
    h = h_ref[...]
    a = _dot(h, w1_ref[...])
    b = _dot(h, w3_ref[...])
    o_ref[...] = (g_ref[...] * (jax.nn.silu(a) * b)).astype(o_ref.dtype)


def _moe_gate_up(h, w1, w3, gates_t, tm, tn):
    m, k = h.shape
    ne, _, f = w1.shape
    nj = f // tn
    return pl.pallas_call(
        _moe_gate_up_kernel, grid=(m // tm, ne, nj),
        in_specs=[pl.BlockSpec((tm, k), lambda i, e, j: (i, 0)),
                  pl.BlockSpec((None, k, tn), lambda i, e, j: (e, 0, j)),
                  pl.BlockSpec((None, k, tn), lambda i, e, j: (e, 0, j)),
                  pl.BlockSpec((None, tm, 1), lambda i, e, j: (e, i, 0))],
        out_specs=pl.BlockSpec((tm, tn), lambda i, e, j: (i, e * nj + j)),
        out_shape=jax.ShapeDtypeStruct((m, ne * f), bf16),
        compiler_params=_cparams(("arbitrary", "arbitrary", "arbitrary")), name="moe_gate_up")(h, w1, w3, gates_t)


def _down_kernel(a_ref, w_ref, x_ref, gate_ref, o_ref, acc_ref, *, nk):
    kk = pl.program_id(1)

    @pl.when(kk == 0)
    def _():
        acc_ref[...] = jnp.zeros_like(acc_ref)

    acc_ref[...] += _dot(a_ref[...], w_ref[...])

    @pl.when(kk == nk - 1)
    def _():
        o_ref[...] = x_ref[...] + gate_ref[...] * acc_ref[...]


def _down_proj(path, act, w2, x, mod3, layer, rows, tm, tk):
    m, d = x.shape
    kdim = act.shape[1]
    nk = kdim // tk
    return pl.pallas_call(
        functools.partial(_down_kernel, nk=nk), grid=(m // tm, nk),
        in_specs=[pl.BlockSpec((tm, tk), lambda i, k: (i, k)),
                  pl.BlockSpec((tk, d), lambda i, k: (k, 0)),
                  pl.BlockSpec((tm, d), lambda i, k: (i, 0)),
                  _mod_spec(path, layer, rows, 5, tm, d)],
        out_specs=pl.BlockSpec((tm, d), lambda i, k: (i, 0)),
        out_shape=jax.ShapeDtypeStruct((m, d), f32),
        scratch_shapes=[pltpu.VMEM((tm, d), f32)],
        compiler_params=_cparams(("arbitrary", "arbitrary")), name="down_proj")(act, w2, x, mod3)


def _conv_window(win, w):
    n = win.shape[0] - 2 * SUBLANES
    acc = None
    for j in range(CONV_W):
        off = SUBLANES + j - CONV_LEFT
        term = win[off:off + n] * w[j:j + 1]
        acc = term if acc is None else acc + term
    return acc


def _qkv_prep_kernel(prev_ref, cur_ref, next_ref, w_ref, o_ref, *, tiles_per_seq):
    t = pl.program_id(0) % tiles_per_seq
    kind = pl.program_id(1)
    has_prev = (t > 0).astype(f32)
    has_next = (t < tiles_per_seq - 1).astype(f32)
    scale = jnp.where(kind == 0, HEAD_D ** -0.5, 1.0).astype(f32)
    tile = cur_ref.shape[0]
    for c in range(tile // CHUNK):
        r0 = c * CHUNK
        for hd in range(N_HEADS):
            cols = slice(hd * HEAD_D, (hd + 1) * HEAD_D)
            top = prev_ref[:, cols] * has_prev if c == 0 else cur_ref[r0 - SUBLANES:r0, cols]
            bot = (next_ref[:, cols] * has_next if r0 + CHUNK == tile
                   else cur_ref[r0 + CHUNK:r0 + CHUNK + SUBLANES, cols])
            win = jnp.concatenate([top, cur_ref[r0:r0 + CHUNK, cols], bot], axis=0)
            y = jax.nn.silu(_conv_window(win, w_ref[:, cols]))
            nrm = lax.rsqrt(jnp.sum(y * y, axis=-1, keepdims=True) + EPS) * scale
            y = y * jnp.where(kind < 2, nrm, 1.0)
            o_ref[r0:r0 + CHUNK, cols] = y


def _qkv_prep(path, proj, conv_w):
    tile = SEQ_TILE
    n = path.tokens // tile
    tps = path.seq_len // tile
    rb = tile // SUBLANES
    last = path.tokens // SUBLANES - 1
    return pl.pallas_call(
        functools.partial(_qkv_prep_kernel, tiles_per_seq=tps), grid=(n, 3),
        in_specs=[
            pl.BlockSpec((SUBLANES, GDN_W), lambda i, j: (jnp.maximum(i * rb - 1, 0), j)),
            pl.BlockSpec((tile, GDN_W), lambda i, j: (i, j)),
            pl.BlockSpec((SUBLANES, GDN_W), lambda i, j: (jnp.minimum((i + 1) * rb, last), j)),
            pl.BlockSpec((CONV_W, GDN_W), lambda i, j: (0, j)),
        ],
        out_specs=pl.BlockSpec((tile, GDN_W), lambda i, j: (i, j)),
        out_shape=jax.ShapeDtypeStruct((path.tokens, 3 * GDN_W), f32),
        compiler_params=_cparams(("arbitrary", "arbitrary")), name="qkv_prep")(proj, proj, proj, conv_w)


INV_BLOCK = 16


def _bdot(a, b):
    return _dot(a.astype(bf16), b.astype(bf16))


def _unit_tri_inverse(a_list, eye, diag_blk):
    p = [-jnp.where(diag_blk, a, 0.0) for a in a_list]
    x = [eye + pi for pi in p]
    span = 2
    while span < INV_BLOCK:
        p = [_bdot(pi, pi) for pi in p]
        x = [xi + _bdot(xi, pi) for xi, pi in zip(x, p)]
        span *= 2
    ia = [(eye + a).astype(bf16) for a in a_list]
    span = INV_BLOCK
    while span < a_list[0].shape[0]:
        r = [eye - _dot(iai, xi.astype(bf16)) for iai, xi in zip(ia, x)]
        x = [xi + _bdot(xi, ri) for xi, ri in zip(x, r)]
        span *= 2
    return x


def _gdn_kernel(*refs, reverse, tiles_per_seq, zero_init, emit_state):
    it = iter(refs)
    q_ref, k_ref, v_ref, ab_ref, prm_ref = next(it), next(it), next(it), next(it), next(it)
    s0_ref = None if zero_init else next(it)
    o_ref = next(it)
    sfin_ref = next(it) if emit_state else None
    s_scr = next(it)

    t = pl.program_id(1)
    nchunks = q_ref.shape[0] // CHUNK

    @pl.when(t == 0)
    def _():
        if zero_init:
            s_scr[...] = jnp.zeros_like(s_scr)
        else:
            s_scr[...] = s0_ref[...]

    ri = lax.broadcasted_iota(jnp.int32, (CHUNK, CHUNK), 0)
    ci = lax.broadcasted_iota(jnp.int32, (CHUNK, CHUNK), 1)
    incl = (ri <= ci) if reverse else (ri >= ci)
    strict = (ri < ci) if reverse else (ri > ci)
    eye = (ri == ci).astype(f32)
    diag_blk = (ri // INV_BLOCK) == (ci // INV_BLOCK)
    tri = incl.astype(f32)
    last_row = 0 if reverse else CHUNK - 1
    dir_off = N_HEADS if reverse else 0
    neg_a = -jnp.exp(prm_ref[0:1, :])
    dt_bias = prm_ref[1:2, :]

    def chunk_body(c, carry):
        cc = (nchunks - 1 - c) if reverse else c
        r0 = pl.multiple_of(cc * CHUNK, CHUNK)
        ab = ab_ref[pl.ds(r0, CHUNK), :]
        g = neg_a * jax.nn.softplus(ab + dt_bias)
        beta = jax.nn.sigmoid(ab)
        gc = jnp.dot(tri, g, precision=lax.Precision.HIGHEST, preferred_element_type=f32)
        g_last = gc[last_row:last_row + 1, :]
        e_gc = jnp.exp(gc)
        e_rest = jnp.exp(g_last - gc)
        e_last = jnp.exp(g_last)
        gc_t = gc.T
        heads = range(N_HEADS)
        rows = pl.ds(r0, CHUNK)
        cols = [slice(hd * HEAD_D, (hd + 1) * HEAD_D) for hd in heads]
        ga = [dir_off + hd for hd in heads]
        b_col = [beta[:, 2 * N_HEADS + g_:2 * N_HEADS + g_ + 1] for g_ in ga]
        eg_col = [e_gc[:, g_:g_ + 1] for g_ in ga]
        qh = [q_ref[rows, cs] for cs in cols]
        kh = [k_ref[rows, cs] for cs in cols]
        vh = [v_ref[rows, cs] for cs in cols]
        khb = [x.astype(bf16) for x in kh]
        gram = [_dot_nt(jnp.concatenate([qh[h], kh[h]], axis=0).astype(bf16), khb[h]) for h in heads]
        decay = [jnp.exp(jnp.where(incl, gc[:, g_:g_ + 1] - gc_t[g_:g_ + 1, :], -jnp.inf)) for g_ in ga]
        a_low = [jnp.where(strict, gram[h][CHUNK:] * b_col[h] * decay[h], 0.0) for h in heads]
        qk = [(gram[h][:CHUNK] * decay[h]).astype(bf16) for h in heads]
        t_inv = _unit_tri_inverse(a_low, eye, diag_blk)
        rhs = [jnp.concatenate([vh[h] * b_col[h], kh[h] * b_col[h] * eg_col[h]], axis=1) for h in heads]
        uw = [_bdot(t_inv[h], rhs[h]) for h in heads]
        s_old = [s_scr[h] for h in heads]
        s_b = [x.astype(bf16) for x in s_old]
        ws = [_dot(jnp.concatenate([uw[h][:, HEAD_D:], qh[h] * eg_col[h]], axis=0).astype(bf16), s_b[h])
              for h in heads]
        v_new = [(uw[h][:, :HEAD_D] - ws[h][:CHUNK]).astype(bf16) for h in heads]
        for h in heads:
            o_ref[rows, cols[h]] = ws[h][CHUNK:] + _dot(qk[h], v_new[h])
        for h in heads:
            kd = (kh[h] * e_rest[:, ga[h]:ga[h] + 1]).astype(bf16)
            s_scr[h] = s_old[h] * e_last[:, ga[h]:ga[h] + 1] + _dot_tn(kd, v_new[h])
        return carry

    lax.fori_loop(0, nchunks, chunk_body, 0)

    if emit_state:
        @pl.when(t == tiles_per_seq - 1)
        def _():
            sfin_ref[...] = s_scr[...]


def _gdn_scan(path, qkv, proj, prm, s0, layer, reverse, emit_state):
    tile = SEQ_TILE
    tps = path.seq_len // tile
    d = 1 if reverse else 0

    def tok(s, t):
        return s * tps + ((tps - 1 - t) if reverse else t)

    in_specs = [
        pl.BlockSpec((tile, GDN_W), lambda s, t: (tok(s, t), 0)),
        pl.BlockSpec((tile, GDN_W), lambda s, t: (tok(s, t), 1)),
        pl.BlockSpec((tile, GDN_W), lambda s, t: (tok(s, t), 2)),
        pl.BlockSpec((tile, LANES), lambda s, t: (tok(s, t), P_AB // LANES)),
        pl.BlockSpec((2, LANES), lambda s, t: (0, 0)),
    ]
    args = [qkv, qkv, qkv, proj, prm]
    if s0 is not None:
        in_specs.append(pl.BlockSpec((None, None, None, N_HEADS, HEAD_D, HEAD_D),
                                     lambda s, t: (s, layer, d, 0, 0, 0)))
        args.append(s0)
    out_specs = [pl.BlockSpec((tile, GDN_W), lambda s, t: (tok(s, t), 0))]
    out_shape = [jax.ShapeDtypeStruct((path.tokens, GDN_W), f32)]
    if emit_state:
        out_specs.append(pl.BlockSpec((None, N_HEADS, HEAD_D, HEAD_D), lambda s, t: (s, 0, 0, 0)))
        out_shape.append(jax.ShapeDtypeStruct((path.nseq, N_HEADS, HEAD_D, HEAD_D), f32))
    kern = functools.partial(_gdn_kernel, reverse=reverse, tiles_per_seq=tps, zero_init=s0 is None,
                             emit_state=emit_state)
    return pl.pallas_call(
        kern, grid=(path.nseq, tps), in_specs=in_specs, out_specs=out_specs, out_shape=out_shape,
        scratch_shapes=[pltpu.VMEM((N_HEADS, HEAD_D, HEAD_D), f32)],
        compiler_params=_cparams(("arbitrary", "arbitrary")), name="gdn_scan")(*args)


def _gdn_combine_kernel(of_ref, ob_ref, z_ref, g_ref, o_ref):
    g = g_ref[...]
    for hd in range(N_HEADS):
        cols = slice(hd * HEAD_D, (hd + 1) * HEAD_D)
        y = _rms(of_ref[:, cols] + ob_ref[:, cols], g)
        o_ref[:, cols] = (y * jax.nn.silu(z_ref[:, cols])).astype(o_ref.dtype)


def _gdn_combine(o_f, o_b, proj, g):
    t = o_f.shape[0]
    tile = 512
    return pl.pallas_call(
        _gdn_combine_kernel, grid=(t // tile,),
        in_specs=[pl.BlockSpec((tile, GDN_W), lambda i: (i, 0)),
                  pl.BlockSpec((tile, GDN_W), lambda i: (i, 0)),
                  pl.BlockSpec((tile, GDN_W), lambda i: (i, P_Z // GDN_W)),
                  pl.BlockSpec((1, HEAD_D), lambda i: (0, 0))],
        out_specs=pl.BlockSpec((tile, GDN_W), lambda i: (i, 0)),
        out_shape=jax.ShapeDtypeStruct((t, GDN_W), bf16),
        compiler_params=_cparams(("arbitrary",)), name="gdn_combine")(o_f, o_b, proj, g.reshape(1, HEAD_D))


def _lru_kernel(*refs, zero_init, emit_state):
    it = iter(refs)
    lx_ref, lg_ref, cw_ref, cb_ref = next(it), next(it), next(it), next(it)
    wa_ref, wi_ref, ba_ref, bi_ref, lam_ref = next(it), next(it), next(it), next(it), next(it)
    h0_ref = None if zero_init else next(it)
    y_ref = next(it)
    hfin_ref = next(it) if emit_state else None
    a_scr, b_scr = next(it), next(it)

    seq = lx_ref.shape[0]
    blk = 256
    nblk = seq // blk
    w_cat = jnp.concatenate([wa_ref[0], wi_ref[0], wa_ref[1], wi_ref[1]], axis=1).astype(bf16)
    bias_cat = jnp.concatenate([ba_ref[0:1], bi_ref[0:1], ba_ref[1:2], bi_ref[1:2]], axis=1)
    sp = jax.nn.softplus(-lam_ref[...])
    cw = cw_ref[...]
    cb = cb_ref[...]

    def gates_body(i, carry):
        r0 = pl.multiple_of(i * blk, blk)
        r_top = pl.multiple_of(jnp.maximum(r0 - SUBLANES, 0), SUBLANES)
        r_bot = pl.multiple_of(jnp.minimum(r0 + blk, seq - SUBLANES), SUBLANES)
        top = lx_ref[pl.ds(r_top, SUBLANES), :] * (i > 0).astype(f32)
        bot = lx_ref[pl.ds(r_bot, SUBLANES), :] * (i < nblk - 1).astype(f32)
        win = jnp.concatenate([top, lx_ref[pl.ds(r0, blk), :], bot], axis=0)
        xl = _conv_window(win, cw) + cb
        pre = _dot(xl.astype(bf16), w_cat) + bias_cat
        for d in range(2):
            r = jax.nn.sigmoid(pre[:, (2 * d) * LRU_BW:(2 * d + 1) * LRU_BW])
            ig = jax.nn.sigmoid(pre[:, (2 * d + 1) * LRU_BW:(2 * d + 2) * LRU_BW])
            a = jnp.exp(-RG_C * r * sp[d:d + 1])
            a_scr[d, pl.ds(r0, blk), :] = a
            b_scr[d, pl.ds(r0, blk), :] = jnp.sqrt(1.0 - a * a) * (ig * xl)
        return carry

    lax.fori_loop(0, nblk, gates_body, 0)

    row = lax.broadcasted_iota(jnp.int32, (CHUNK, LRU_BW), 0)
    nchunks = seq // CHUNK

    def chunk_scan(a, b, h_prev, reverse):
        s = 1
        while s < CHUNK:
            if reverse:
                keep = row < CHUNK - s
                a_sh = jnp.where(keep, pltpu.roll(a, CHUNK - s, 0), 1.0)
                b_sh = jnp.where(keep, pltpu.roll(b, CHUNK - s, 0), 0.0)
            else:
                keep = row >= s
                a_sh = jnp.where(keep, pltpu.roll(a, s, 0), 1.0)
                b_sh = jnp.where(keep, pltpu.roll(b, s, 0), 0.0)
            b = a * b_sh + b
            a = a * a_sh
            s *= 2
        return b + a * h_prev

    def scan_body(c, carry):
        hf, hb = carry
        rf = pl.multiple_of(c * CHUNK, CHUNK)
        rb = pl.multiple_of((nchunks - 1 - c) * CHUNK, CHUNK)
        h_f = chunk_scan(a_scr[0, pl.ds(rf, CHUNK), :], b_scr[0, pl.ds(rf, CHUNK), :], hf, False)
        h_b = chunk_scan(a_scr[1, pl.ds(rb, CHUNK), :], b_scr[1, pl.ds(rb, CHUNK), :], hb, True)
        b_scr[0, pl.ds(rf, CHUNK), :] = h_f
        b_scr[1, pl.ds(rb, CHUNK), :] = h_b
        return h_f[CHUNK - 1:CHUNK, :], h_b[0:1, :]

    if zero_init:
        init = (jnp.zeros((1, LRU_BW), f32), jnp.zeros((1, LRU_BW), f32))
    else:
        init = (h0_ref[0:1, :], h0_ref[1:2, :])
    hf, hb = lax.fori_loop(0, nchunks, scan_body, init)
    if emit_state:
        hfin_ref[0:1, :] = hf
        hfin_ref[1:2, :] = hb

    def out_body(i, carry):
        r0 = pl.multiple_of(i * blk, blk)
        h = b_scr[0, pl.ds(r0, blk), :] + b_scr[1, pl.ds(r0, blk), :]
        y_ref[pl.ds(r0, blk), :] = (h * jax.nn.gelu(lg_ref[pl.ds(r0, blk), :])).astype(y_ref.dtype)
        return carry

    lax.fori_loop(0, nblk, out_body, 0)


def _lru(path, proj, lp, h0, layer, emit_state):
    seq = path.seq_len
    proj3 = proj.reshape(path.nseq, seq, P_COLS)
    bx, bg = P_LX // LRU_BW, P_LG // LRU_BW
    in_specs = [
        pl.BlockSpec((None, seq, LRU_BW), lambda s, b: (s, 0, bx + b)),
        pl.BlockSpec((None, seq, LRU_BW), lambda s, b: (s, 0, bg + b)),
        pl.BlockSpec((CONV_W, LRU_BW), lambda s, b: (0, b)),
        pl.BlockSpec((1, LRU_BW), lambda s, b: (0, b)),
        pl.BlockSpec((2, None, LRU_BW, LRU_BW), lambda s, b: (0, b, 0, 0)),
        pl.BlockSpec((2, None, LRU_BW, LRU_BW), lambda s, b: (0, b, 0, 0)),
        pl.BlockSpec((2, LRU_BW), lambda s, b: (0, b)),
        pl.BlockSpec((2, LRU_BW), lambda s, b: (0, b)),
        pl.BlockSpec((2, LRU_BW), lambda s, b: (0, b)),
    ]
    args = [proj3, proj3, lp['lru_conv_w'], lp['lru_conv_b'].reshape(1, LRU_W), lp['lru_wa'], lp['lru_wi'],
            lp['lru_ba'], lp['lru_bi'], lp['lru_lambda']]
    if h0 is not None:
        in_specs.append(pl.BlockSpec((None, None, 2, LRU_BW), lambda s, b: (s, layer, 0, b)))
        args.append(h0)
    out_specs = [pl.BlockSpec((None, seq, LRU_BW), lambda s, b: (s, 0, b))]
    out_shape = [jax.ShapeDtypeStruct((path.nseq, seq, LRU_W), bf16)]
    if emit_state:
        out_specs.append(pl.BlockSpec((None, 2, LRU_BW), lambda s, b: (s, 0, b)))
        out_shape.append(jax.ShapeDtypeStruct((path.nseq, 2, LRU_W), f32))
    kern = functools.partial(_lru_kernel, zero_init=h0 is None, emit_state=emit_state)
    outs = pl.pallas_call(
        kern, grid=(path.nseq, LRU_BLOCKS), in_specs=in_specs, out_specs=out_specs, out_shape=out_shape,
        scratch_shapes=[pltpu.VMEM((2, seq, LRU_BW), f32), pltpu.VMEM((2, seq, LRU_BW), f32)],
        compiler_params=_cparams(("arbitrary", "arbitrary")), name="rglru")(*args)
    y = outs[0].reshape(path.tokens, LRU_W)
    return (y, outs[1]) if emit_state else (y, None)


def _pad_cols(w, n):
    return jnp.pad(w, ((0, 0),) * (w.ndim - 1) + ((0, n - w.shape[-1]),))


def _pos_table():
    quarter = D_MODEL // 4
    freqs = jnp.exp(-math.log(10000.0) * jnp.arange(quarter, dtype=f32) / quarter)
    e = jnp.arange(GRID_W, dtype=f32)[:, None] * freqs
    return jnp.concatenate([jnp.sin(e), jnp.cos(e)], axis=-1)


def _mixer_and_ffn(path, x, layer, rows, mod3, lp, ffn, s0_gdn, h0_lru, emit_state, pos):
    if pos is not None:
        h, x = _norm_mod(path, x, lp['norm1_g'], mod3, layer, rows, 1, 0, pos=pos)
    else:
        (h,) = _norm_mod(path, x, lp['norm1_g'], mod3, layer, rows, 1, 0)
    proj = _matmul(h, lp['w_in'], 1024, 896, f32)
    qkv = _qkv_prep(path, proj, lp['gdn_conv_w'])
    outs_f = _gdn_scan(path, qkv, proj, lp['gdn_prm'], s0_gdn, layer, False, emit_state)
    outs_b = _gdn_scan(path, qkv, proj, lp['gdn_prm'], s0_gdn, layer, True, emit_state)
    o_mix = _gdn_combine(outs_f[0], outs_b[0], proj, lp['gdn_norm_g'])
    y_lru, h_fin = _lru(path, proj, lp, h0_lru, layer, emit_state)
    x = _out_proj(path, o_mix, y_lru, lp['w_out'], x, mod3, layer, rows)
    if ffn['kind'] == 'dense':
        (h2,) = _norm_mod(path, x, lp['norm2_g'], mod3, layer, rows, 4, 3)
        act = _gate_up(h2, ffn['w1'], ffn['w3'], 1024, 512)
        x = _down_proj(path, act, ffn['w2'], x, mod3, layer, rows, 512, ffn['w2'].shape[0] // 4)
    else:
        h2, gates = _norm_mod(path, x, lp['norm2_g'], mod3, layer, rows, 4, 3, router=(ffn['wr'], ffn['br']))
        gates_t = jnp.transpose(gates[:, :N_EXPERTS])[:, :, None]
        act = _moe_gate_up(h2, ffn['w1'], ffn['w3'], gates_t, 1024, 256)
        x = _down_proj(path, act, ffn['w2'], x, mod3, layer, rows, 512, 1408)
    s_fin = jnp.stack([outs_f[1], outs_b[1]], axis=1) if emit_state else None
    return x, s_fin, h_fin


def kernel(x_prompt, x_sample, state_gdn, state_lru, c, c_ctx, w_mod, b_mod, norm1_g, norm2_g, w_in, gdn_conv_w, gdn_A_log, gdn_dt_bias, gdn_norm_g, lru_conv_w, lru_conv_b, lru_wa, lru_ba, lru_wi, lru_bi, lru_lambda, w_out, ffd_w1, ffd_w3, ffd_w2, moe_wr, moe_br, moe_w1, moe_w3, moe_w2, final_g):
    depth = w_mod.shape[0]
    bp, sp_len, d = x_prompt.shape
    bs, ss_len, _ = x_sample.shape
    ctx = _Path(bp, sp_len, 0, False)
    lat = _Path(bs, ss_len, 1, True)

    rows = SUBLANES * ((1 + bs + SUBLANES - 1) // SUBLANES)
    cvec = jnp.zeros((rows, d), f32).at[0].set(c_ctx).at[1:1 + bs].set(c)
    mod3 = _modulation(cvec, w_mod, b_mod).reshape(depth * rows * 6, 1, d)

    pos = _pos_table()
    xp = x_prompt.reshape(ctx.tokens, d)
    xs = x_sample.reshape(lat.tokens, d)
    new_gdn, new_lru = [], []
    for l in range(depth):
        w_in_l = w_in[l].astype(bf16)
        n_ab = 4 * N_HEADS
        w_in_p = jnp.concatenate([w_in_l[:, :P_AB + n_ab], jnp.zeros((d, LANES - n_ab), bf16),
                                  w_in_l[:, P_AB + n_ab:]], axis=1)
        prm = jnp.stack([_pad_cols(gdn_A_log[l].reshape(1, -1), LANES)[0],
                         _pad_cols(gdn_dt_bias[l].reshape(1, -1), LANES)[0]])
        lp = {
            'norm1_g': norm1_g[l], 'norm2_g': norm2_g[l], 'w_in': w_in_p, 'gdn_conv_w': gdn_conv_w[l],
            'gdn_prm': prm, 'gdn_norm_g': gdn_norm_g[l], 'lru_conv_w': lru_conv_w[l], 'lru_conv_b': lru_conv_b[l],
            'lru_wa': lru_wa[l], 'lru_ba': lru_ba[l], 'lru_wi': lru_wi[l], 'lru_bi': lru_bi[l],
            'lru_lambda': lru_lambda[l], 'w_out': w_out[l].astype(bf16),
        }
        j = l // 2
        if l % 2 == 0:
            f = ffd_w1.shape[2]
            fp = 512 * ((f + 511) // 512)
            ffn = {'kind': 'dense', 'w1': _pad_cols(ffd_w1[j].astype(bf16), fp),
                   'w3': _pad_cols(ffd_w3[j].astype(bf16), fp),
                   'w2': jnp.pad(ffd_w2[j].astype(bf16), ((0, fp - f), (0, 0)))}
        else:
            ne, _, fe = moe_w1.shape[1:]
            ffn = {'kind': 'moe', 'wr': _pad_cols(moe_wr[j], LANES), 'br': _pad_cols(moe_br[j].reshape(1, -1), LANES),
                   'w1': moe_w1[j].astype(bf16), 'w3': moe_w3[j].astype(bf16),
                   'w2': moe_w2[j].astype(bf16).reshape(ne * fe, d)}
        xp, s_g, s_l = _mixer_and_ffn(ctx, xp, l, rows, mod3, lp, ffn, None, None, True, None)
        new_gdn.append(s_g)
        new_lru.append(s_l)
        xs, _, _ = _mixer_and_ffn(lat, xs, l, rows, mod3, lp, ffn, state_gdn, state_lru, False,
                                  pos if l == 0 else None)
    y_prompt = _final_norm(xp, final_g).reshape(bp, sp_len, d)
    y_sample = _final_norm(xs, final_g).reshape(bs, ss_len, d)
    return (y_prompt, y_sample, jnp.stack(new_gdn, axis=1), jnp.stack(new_lru, axis=1))
```

```python
import functools
import math

import jax
import jax.numpy as jnp
from jax import lax
from jax.experimental import pallas as pl
from jax.experimental.pallas import tpu as pltpu

f32 = jnp.float32
bf16 = jnp.bfloat16

D_MODEL = 2048
GRID_W = 64
N_HEADS = 8
HEAD_D = 128
GDN_W = N_HEADS * HEAD_D
LRU_W = 1024
LRU_BLOCKS = 8
LRU_BW = LRU_W // LRU_BLOCKS
CONV_W = 4
CONV_LEFT = 2
CHUNK = 64
RG_C = 8.0
N_EXPERTS = 8
EPS = 1e-6
LANES = 128
SUBLANES = 8

P_Q = 0
P_K = P_Q + GDN_W
P_V = P_K + GDN_W
P_Z = P_V + GDN_W
P_AB = P_Z + GDN_W
P_LX = P_AB + LANES
P_LG = P_LX + LRU_W
P_COLS = P_LG + LRU_W

SEQ_TILE = 256
VMEM_LIMIT = 48 * 1024 * 1024


def _cparams(sem):
    return pltpu.CompilerParams(dimension_semantics=sem, vmem_limit_bytes=VMEM_LIMIT)


def _dot(a, b):
    return jnp.dot(a, b, preferred_element_type=f32)


def _dot_nt(a, b, precision=None):
    return lax.dot_general(a, b, (((1,), (1,)), ((), ())), precision=precision, preferred_element_type=f32)


def _dot_tn(a, b):
    return lax.dot_general(a, b, (((0,), (0,)), ((), ())), preferred_element_type=f32)


def _mod_kernel(c_ref, w_ref, b_ref, o_ref):
    a = jax.nn.silu(c_ref[...]).astype(bf16)
    o_ref[...] = _dot(a, w_ref[...].astype(bf16)) + b_ref[...]


def _modulation(cvec, w_mod, b_mod):
    depth, d, n = w_mod.shape
    tn = 1024
    return pl.pallas_call(
        _mod_kernel,
        grid=(depth, n // tn),
        in_specs=[
            pl.BlockSpec((cvec.shape[0], d), lambda l, j: (0, 0)),
            pl.BlockSpec((None, d, tn), lambda l, j: (l, 0, j)),
            pl.BlockSpec((None, 1, tn), lambda l, j: (l, 0, j)),
        ],
        out_specs=pl.BlockSpec((None, cvec.shape[0], tn), lambda l, j: (l, 0, j)),
        out_shape=jax.ShapeDtypeStruct((depth, cvec.shape[0], n), f32),
        compiler_params=_cparams(("arbitrary", "arbitrary")),
        name="modulation",
    )(cvec, w_mod, b_mod.reshape(depth, 1, n))


class _Path:
    def __init__(self, nseq, seq_len, mod_row0, per_seq_mod):
        self.nseq = nseq
        self.seq_len = seq_len
        self.tokens = nseq * seq_len
        self.mod_row0 = mod_row0
        self.per_seq_mod = per_seq_mod

    def mod_row(self, tile_idx, tile):
        if self.per_seq_mod:
            return self.mod_row0 + tile_idx // (self.seq_len // tile)
        return self.mod_row0


def _mod_spec(path, layer, rows, part, tile, width, col_of=None, grid_rank=1):
    def imap(*idx):
        i = idx[0]
        j = idx[col_of] if col_of is not None else 0
        return ((layer * rows + path.mod_row(i, tile)) * 6 + part, 0, j)
    return pl.BlockSpec((None, 1, width), imap)


def _rms(x, g):
    return x * lax.rsqrt(jnp.mean(x * x, axis=-1, keepdims=True) + EPS) * g


def _norm_mod_kernel(*refs, with_pos, with_router, tile, tiles_per_seq):
    it = iter(refs)
    x_ref, g_ref, scale_ref, shift_ref = next(it), next(it), next(it), next(it)
    pos_ref = next(it) if with_pos else None
    wr_ref, br_ref = (next(it), next(it)) if with_router else (None, None)
    h_ref = next(it)
    xo_ref = next(it) if with_pos else None
    gates_ref = next(it) if with_router else None

    x = x_ref[...]
    if with_pos:
        rows_per_tile = tile // GRID_W
        row0 = (pl.program_id(0) % tiles_per_seq) * rows_per_tile
        half = D_MODEL // 2
        col_part = pos_ref[...]
        pieces = []
        for r in range(rows_per_tile):
            row_part = jnp.broadcast_to(pos_ref[pl.ds(row0 + r, 1), :], (GRID_W, half))
            pieces.append(jnp.concatenate([row_part, col_part], axis=1))
        x = x + jnp.concatenate(pieces, axis=0)
        xo_ref[...] = x
    h = _rms(x, g_ref[...]) * (1.0 + scale_ref[...]) + shift_ref[...]
    h_ref[...] = h.astype(h_ref.dtype)
    if with_router:
        logits = jnp.dot(h, wr_ref[...], precision=lax.Precision.HIGHEST, preferred_element_type=f32) + br_ref[...]
        lane = lax.broadcasted_iota(jnp.int32, logits.shape, 1)
        valid = lane < N_EXPERTS
        logits = jnp.where(valid, logits, -jnp.inf)
        e = jnp.exp(logits - jnp.max(logits, axis=-1, keepdims=True))
        p = e / jnp.sum(e, axis=-1, keepdims=True)
        p = jnp.where(valid, p, -1.0)
        m1 = jnp.max(p, axis=-1, keepdims=True)
        i1 = jnp.min(jnp.where(p == m1, lane, LANES), axis=-1, keepdims=True)
        p2 = jnp.where(lane == i1, -1.0, p)
        m2 = jnp.max(p2, axis=-1, keepdims=True)
        i2 = jnp.min(jnp.where(p2 == m2, lane, LANES), axis=-1, keepdims=True)
        den = m1 + m2
        gates_ref[...] = (jnp.where(lane == R_IDX, i1.astype(f32), 0.0) + jnp.where(lane == R_IDX + 1, i2.astype(f32), 0.0)
                          + jnp.where(lane == R_GATE, m1 / den, 0.0) + jnp.where(lane == R_GATE + 1, m2 / den, 0.0))


R_IDX, R_GATE = 0, 2


def _norm_mod(path, x, g, mod3, layer, rows, part_scale, part_shift, pos=None, router=None, h_dtype=bf16):
    tile = SEQ_TILE
    n = path.tokens // tile
    d = D_MODEL
    in_specs = [
        pl.BlockSpec((tile, d), lambda i: (i, 0)),
        pl.BlockSpec((1, d), lambda i: (0, 0)),
        _mod_spec(path, layer, rows, part_scale, tile, d),
        _mod_spec(path, layer, rows, part_shift, tile, d),
    ]
    args = [x, g.reshape(1, d), mod3, mod3]
    out_specs = [pl.BlockSpec((tile, d), lambda i: (i, 0))]
    out_shape = [jax.ShapeDtypeStruct((path.tokens, d), h_dtype)]
    if pos is not None:
        in_specs.append(pl.BlockSpec(pos.shape, lambda i: (0, 0)))
        args.append(pos)
        out_specs.append(pl.BlockSpec((tile, d), lambda i: (i, 0)))
        out_shape.append(jax.ShapeDtypeStruct((path.tokens, d), f32))
    if router is not None:
        wr, br = router
        in_specs += [pl.BlockSpec((d, LANES), lambda i: (0, 0)), pl.BlockSpec((1, LANES), lambda i: (0, 0))]
        args += [wr, br]
        out_specs.append(pl.BlockSpec((tile, LANES), lambda i: (i, 0)))
        out_shape.append(jax.ShapeDtypeStruct((path.tokens, LANES), f32))
    kern = functools.partial(_norm_mod_kernel, with_pos=pos is not None, with_router=router is not None,
                             tile=tile, tiles_per_seq=path.seq_len // tile)
    return pl.pallas_call(kern, grid=(n,), in_specs=in_specs, out_specs=out_specs, out_shape=out_shape,
                          compiler_params=_cparams(("arbitrary",)), name="norm_mod")(*args)


def _final_norm_kernel(x_ref, g_ref, o_ref):
    o_ref[...] = _rms(x_ref[...], g_ref[...])


def _final_norm(x, g):
    t, d = x.shape
    tile = 512
    return pl.pallas_call(
        _final_norm_kernel, grid=(t // tile,),
        in_specs=[pl.BlockSpec((tile, d), lambda i: (i, 0)), pl.BlockSpec((1, d), lambda i: (0, 0))],
        out_specs=pl.BlockSpec((tile, d), lambda i: (i, 0)),
        out_shape=jax.ShapeDtypeStruct((t, d), f32),
        compiler_params=_cparams(("arbitrary",)), name="final_norm")(x, g.reshape(1, d))


def _mm_kernel(a_ref, w_ref, o_ref):
    o_ref[...] = _dot(a_ref[...], w_ref[...]).astype(o_ref.dtype)


def _matmul(a, w, tm, tn, out_dtype):
    m, k = a.shape
    n = w.shape[1]
    return pl.pallas_call(
        _mm_kernel, grid=(m // tm, n // tn),
        in_specs=[pl.BlockSpec((tm, k), lambda i, j: (i, 0)), pl.BlockSpec((k, tn), lambda i, j: (0, j))],
        out_specs=pl.BlockSpec((tm, tn), lambda i, j: (i, j)),
        out_shape=jax.ShapeDtypeStruct((m, n), out_dtype),
        compiler_params=_cparams(("arbitrary", "arbitrary")), name="matmul")(a, w)


def _out_proj_kernel(a1_ref, a2_ref, w1_ref, w2_ref, x_ref, gate_ref, o_ref):
    acc = _dot(a1_ref[...], w1_ref[...]) + _dot(a2_ref[...], w2_ref[...])
    o_ref[...] = x_ref[...] + gate_ref[...] * acc


def _out_proj(path, a1, a2, w, x, mod3, layer, rows):
    tm, tn = 1024, 512
    m, d = x.shape
    k1, k2 = a1.shape[1], a2.shape[1]
    return pl.pallas_call(
        _out_proj_kernel, grid=(m // tm, d // tn),
        in_specs=[
            pl.BlockSpec((tm, k1), lambda i, j: (i, 0)),
            pl.BlockSpec((tm, k2), lambda i, j: (i, 0)),
            pl.BlockSpec((k1, tn), lambda i, j: (0, j)),
            pl.BlockSpec((k2, tn), lambda i, j: (k1 // k2, j)),
            pl.BlockSpec((tm, tn), lambda i, j: (i, j)),
            _mod_spec(path, layer, rows, 2, tm, tn, col_of=1),
        ],
        out_specs=pl.BlockSpec((tm, tn), lambda i, j: (i, j)),
        out_shape=jax.ShapeDtypeStruct((m, d), f32),
        compiler_params=_cparams(("arbitrary", "arbitrary")), name="out_proj")(a1, a2, w, w, x, mod3)


def _gate_up_kernel(h_ref, w1_ref, w3_ref, o_ref):
    h = h_ref[...]
    a = _dot(h, w1_ref[...])
    b = _dot(h, w3_ref[...])
    o_ref[...] = (jax.nn.silu(a) * b).astype(o_ref.dtype)


def _gate_up(h, w1, w3, tm, tn):
    m, k = h.shape
    n = w1.shape[1]
    return pl.pallas_call(
        _gate_up_kernel, grid=(m // tm, n // tn),
        in_specs=[pl.BlockSpec((tm, k), lambda i, j: (i, 0)),
                  pl.BlockSpec((k, tn), lambda i, j: (0, j)),
                  pl.BlockSpec((k, tn), lambda i, j: (0, j))],
        out_specs=pl.BlockSpec((tm, tn), lambda i, j: (i, j)),
        out_shape=jax.ShapeDtypeStruct((m, n), bf16),
        compiler_params=_cparams(("arbitrary", "arbitrary")), name="gate_up")(h, w1, w3)


MOE_TILE = 256


def _moe_route(route, n_tok):
    tm = MOE_TILE
    n_rows = 2 * n_tok + N_EXPERTS * tm
    e_idx = route[:, R_IDX:R_IDX + 2].astype(jnp.int32).reshape(-1)
    gate = route[:, R_GATE:R_GATE + 2].reshape(-1)
    one_hot = (e_idx[:, None] == jnp.arange(N_EXPERTS, dtype=jnp.int32)[None, :]).astype(jnp.int32)
    csum = jnp.cumsum(one_hot, axis=0)
    rank = jnp.take_along_axis(csum, e_idx[:, None], axis=1)[:, 0] - 1
    padded = ((csum[-1] + tm - 1) // tm) * tm
    ends = jnp.cumsum(padded)
    pos = (ends - padded)[e_idx] + rank
    tok_of_row = jnp.zeros((n_rows,), jnp.int32).at[pos].set(jnp.arange(2 * n_tok, dtype=jnp.int32) // 2)
    gate_of_row = jnp.zeros((n_rows,), f32).at[pos].set(gate)
    tile_start = jnp.arange(n_rows // tm, dtype=jnp.int32) * tm
    expert_of_tile = jnp.minimum(jnp.sum(tile_start[:, None] >= ends[None, :], axis=1), N_EXPERTS - 1)
    n_used = (ends[-1:] // tm).astype(jnp.int32)
    return tok_of_row, gate_of_row.reshape(n_rows, 1), expert_of_tile.astype(jnp.int32), n_used, pos.astype(jnp.int32)


def _row_copy(src_hbm, src_row, dst_vmem, dst_row, sem):
    return pltpu.make_async_copy(src_hbm.at[pl.ds(src_row, 1), :], dst_vmem.at[pl.ds(dst_row, 1), :], sem)


def _row_gather_kernel(tok_ref, nused_ref, h_hbm, o_ref, sem):
    i = pl.program_id(0)
    tm = o_ref.shape[0]

    @pl.when(i < nused_ref[0])
    def _():
        def issue(r, c):
            _row_copy(h_hbm, tok_ref[i * tm + r], o_ref, r, sem).start()
            return c
        lax.fori_loop(0, tm, issue, 0)

        def drain(r, c):
            _row_copy(h_hbm, 0, o_ref, 0, sem).wait()
            return c
        lax.fori_loop(0, tm, drain, 0)

    @pl.when(i >= nused_ref[0])
    def _():
        o_ref[...] = jnp.zeros_like(o_ref)


def _row_gather(h, tok_of_row, n_used):
    tm = MOE_TILE
    n_rows = tok_of_row.shape[0]
    d = h.shape[1]
    return pl.pallas_call(
        _row_gather_kernel,
        grid_spec=pltpu.PrefetchScalarGridSpec(
            num_scalar_prefetch=2, grid=(n_rows // tm,),
            in_specs=[pl.BlockSpec(memory_space=pl.ANY)],
            out_specs=pl.BlockSpec((tm, d), lambda i, tok, nu: (i, 0)),
            scratch_shapes=[pltpu.SemaphoreType.DMA]),
        out_shape=jax.ShapeDtypeStruct((n_rows, d), h.dtype),
        compiler_params=_cparams(("arbitrary",)), name="moe_row_gather")(tok_of_row, n_used, h)


def _moe_gate_up_kernel(eot_ref, nused_ref, x_ref, w1_ref, w3_ref, g_ref, o_ref):
    i = pl.program_id(1)

    @pl.when(i < nused_ref[0])
    def _():
        x = x_ref[...].astype(bf16)
        a = _dot(x, w1_ref[...])
        b = _dot(x, w3_ref[...])
        o_ref[...] = (g_ref[...] * (jax.nn.silu(a) * b)).astype(o_ref.dtype)

    @pl.when(i >= nused_ref[0])
    def _():
        o_ref[...] = jnp.zeros_like(o_ref)


def _moe_gate_up(rows, w1, w3, gate_of_row, expert_of_tile, n_used, tn):
    tm = MOE_TILE
    n_rows, k = rows.shape
    f = w1.shape[2]
    return pl.pallas_call(
        _moe_gate_up_kernel,
        grid_spec=pltpu.PrefetchScalarGridSpec(
            num_scalar_prefetch=2, grid=(f // tn, n_rows // tm),
            in_specs=[pl.BlockSpec((tm, k), lambda j, i, eot, nu: (i, 0)),
                      pl.BlockSpec((None, k, tn), lambda j, i, eot, nu: (eot[i], 0, j)),
                      pl.BlockSpec((None, k, tn), lambda j, i, eot, nu: (eot[i], 0, j)),
                      pl.BlockSpec((tm, 1), lambda j, i, eot, nu: (i, 0))],
            out_specs=pl.BlockSpec((tm, tn), lambda j, i, eot, nu: (i, j))),
        out_shape=jax.ShapeDtypeStruct((n_rows, f), bf16),
        compiler_params=_cparams(("arbitrary", "arbitrary")), name="moe_gate_up")(
            expert_of_tile, n_used, rows, w1, w3, gate_of_row)


def _moe_down_kernel(eot_ref, nused_ref, a_ref, w_ref, o_ref):
    i = pl.program_id(0)

    @pl.when(i < nused_ref[0])
    def _():
        o_ref[...] = _dot(a_ref[...], w_ref[...])

    @pl.when(i >= nused_ref[0])
    def _():
        o_ref[...] = jnp.zeros_like(o_ref)


def _moe_down(act, w2, expert_of_tile, n_used):
    tm = MOE_TILE
    n_rows, f = act.shape
    d = w2.shape[2]
    return pl.pallas_call(
        _moe_down_kernel,
        grid_spec=pltpu.PrefetchScalarGridSpec(
            num_scalar_prefetch=2, grid=(n_rows // tm,),
            in_specs=[pl.BlockSpec((tm, f), lambda i, eot, nu: (i, 0)),
                      pl.BlockSpec((None, f, d), lambda i, eot, nu: (eot[i], 0, 0))],
            out_specs=pl.BlockSpec((tm, d), lambda i, eot, nu: (i, 0))),
        out_shape=jax.ShapeDtypeStruct((n_rows, d), f32),
        compiler_params=_cparams(("arbitrary",)), name="moe_down")(expert_of_tile, n_used, act, w2)


def _moe_combine_kernel(pos_ref, y_hbm, x_ref, gate_ref, o_ref, buf, sem):
    i = pl.program_id(0)
    tm = o_ref.shape[0]

    def issue(r, c):
        a = 2 * (i * tm + r)
        _row_copy(y_hbm, pos_ref[a], buf.at[0], r, sem).start()
        _row_copy(y_hbm, pos_ref[a + 1], buf.at[1], r, sem).start()
        return c
    lax.fori_loop(0, tm, issue, 0)

    def drain(r, c):
        _row_copy(y_hbm, 0, buf.at[0], 0, sem).wait()
        return c
    lax.fori_loop(0, 2 * tm, drain, 0)
    o_ref[...] = x_ref[...] + gate_ref[...] * (buf[0] + buf[1])


def _moe_combine(path, y_rows, pos, x, mod3, layer, rows):
    tm = SEQ_TILE
    m, d = x.shape
    return pl.pallas_call(
        _moe_combine_kernel,
        grid_spec=pltpu.PrefetchScalarGridSpec(
            num_scalar_prefetch=1, grid=(m // tm,),
            in_specs=[pl.BlockSpec(memory_space=pl.ANY),
                      pl.BlockSpec((tm, d), lambda i, pos: (i, 0)),
                      pl.BlockSpec((None, 1, d),
                                   lambda i, pos: ((layer * rows + path.mod_row(i, tm)) * 6 + 5, 0, 0))],
            out_specs=pl.BlockSpec((tm, d), lambda i, pos: (i, 0)),
            scratch_shapes=[pltpu.VMEM((2, tm, d), f32), pltpu.SemaphoreType.DMA]),
        out_shape=jax.ShapeDtypeStruct((m, d), f32),
        compiler_params=_cparams(("arbitrary",)), name="moe_combine")(pos, y_rows, x, mod3)


def _down_kernel(a_ref, w_ref, x_ref, gate_ref, o_ref, acc_ref, *, nk):
    kk = pl.program_id(1)

    @pl.when(kk == 0)
    def _():
        acc_ref[...] = jnp.zeros_like(acc_ref)

    acc_ref[...] += _dot(a_ref[...], w_ref[...])

    @pl.when(kk == nk - 1)
    def _():
        o_ref[...] = x_ref[...] + gate_ref[...] * acc_ref[...]


def _down_proj(path, act, w2, x, mod3, layer, rows, tm, tk):
    m, d = x.shape
    kdim = act.shape[1]
    nk = kdim // tk
    return pl.pallas_call(
        functools.partial(_down_kernel, nk=nk), grid=(m // tm, nk),
        in_specs=[pl.BlockSpec((tm, tk), lambda i, k: (i, k)),
                  pl.BlockSpec((tk, d), lambda i, k: (k, 0)),
                  pl.BlockSpec((tm, d), lambda i, k: (i, 0)),
                  _mod_spec(path, layer, rows, 5, tm, d)],
        out_specs=pl.BlockSpec((tm, d), lambda i, k: (i, 0)),
        out_shape=jax.ShapeDtypeStruct((m, d), f32),
        scratch_shapes=[pltpu.VMEM((tm, d), f32)],
        compiler_params=_cparams(("arbitrary", "arbitrary")), name="down_proj")(act, w2, x, mod3)


def _conv_window(win, w):
    n = win.shape[0] - 2 * SUBLANES
    acc = None
    for j in range(CONV_W):
        off = SUBLANES + j - CONV_LEFT
        term = win[off:off + n] * w[j:j + 1]
        acc = term if acc is None else acc + term
    return acc


def _qkv_prep_kernel(prev_ref, cur_ref, next_ref, w_ref, o_ref, *, tiles_per_seq):
    t = pl.program_id(0) % tiles_per_seq
    kind = pl.program_id(1)
    has_prev = (t > 0).astype(f32)
    has_next = (t < tiles_per_seq - 1).astype(f32)
    scale = jnp.where(kind == 0, HEAD_D ** -0.5, 1.0).astype(f32)
    tile = cur_ref.shape[0]
    for c in range(tile // CHUNK):
        r0 = c * CHUNK
        for hd in range(N_HEADS):
            cols = slice(hd * HEAD_D, (hd + 1) * HEAD_D)
            top = prev_ref[:, cols] * has_prev if c == 0 else cur_ref[r0 - SUBLANES:r0, cols]
            bot = (next_ref[:, cols] * has_next if r0 + CHUNK == tile
                   else cur_ref[r0 + CHUNK:r0 + CHUNK + SUBLANES, cols])
            win = jnp.concatenate([top, cur_ref[r0:r0 + CHUNK, cols], bot], axis=0)
            y = jax.nn.silu(_conv_window(win, w_ref[:, cols]))
            nrm = lax.rsqrt(jnp.sum(y * y, axis=-1, keepdims=True) + EPS) * scale
            y = y * jnp.where(kind < 2, nrm, 1.0)
            o_ref[r0:r0 + CHUNK, cols] = y


def _qkv_prep(path, proj, conv_w):
    tile = SEQ_TILE
    n = path.tokens // tile
    tps = path.seq_len // tile
    rb = tile // SUBLANES
    last = path.tokens // SUBLANES - 1
    return pl.pallas_call(
        functools.partial(_qkv_prep_kernel, tiles_per_seq=tps), grid=(n, 3),
        in_specs=[
            pl.BlockSpec((SUBLANES, GDN_W), lambda i, j: (jnp.maximum(i * rb - 1, 0), j)),
            pl.BlockSpec((tile, GDN_W), lambda i, j: (i, j)),
            pl.BlockSpec((SUBLANES, GDN_W), lambda i, j: (jnp.minimum((i + 1) * rb, last), j)),
            pl.BlockSpec((CONV_W, GDN_W), lambda i, j: (0, j)),
        ],
        out_specs=pl.BlockSpec((tile, GDN_W), lambda i, j: (i, j)),
        out_shape=jax.ShapeDtypeStruct((path.tokens, 3 * GDN_W), f32),
        compiler_params=_cparams(("arbitrary", "arbitrary")), name="qkv_prep")(proj, proj, proj, conv_w)


INV_BLOCK = 16


def _bdot(a, b):
    return _dot(a.astype(bf16), b.astype(bf16))


def _unit_tri_inverse(a_list, eye, diag_blk):
    p = [-jnp.where(diag_blk, a, 0.0) for a in a_list]
    x = [eye + pi for pi in p]
    span = 2
    while span < INV_BLOCK:
        p = [_bdot(pi, pi) for pi in p]
        x = [xi + _bdot(xi, pi) for xi, pi in zip(x, p)]
        span *= 2
    ia = [(eye + a).astype(bf16) for a in a_list]
    span = INV_BLOCK
    while span < a_list[0].shape[0]:
        r = [eye - _dot(iai, xi.astype(bf16)) for iai, xi in zip(ia, x)]
        x = [xi + _bdot(xi, ri) for xi, ri in zip(x, r)]
        span *= 2
    return x


def _gdn_kernel(*refs, reverse, tiles_per_seq, zero_init, emit_state):
    it = iter(refs)
    q_ref, k_ref, v_ref, ab_ref, prm_ref = next(it), next(it), next(it), next(it), next(it)
    s0_ref = None if zero_init else next(it)
    o_ref = next(it)
    sfin_ref = next(it) if emit_state else None
    s_scr = next(it)

    t = pl.program_id(1)
    nchunks = q_ref.shape[0] // CHUNK

    @pl.when(t == 0)
    def _():
        if zero_init:
            s_scr[...] = jnp.zeros_like(s_scr)
        else:
            s_scr[...] = s0_ref[...]

    ri = lax.broadcasted_iota(jnp.int32, (CHUNK, CHUNK), 0)
    ci = lax.broadcasted_iota(jnp.int32, (CHUNK, CHUNK), 1)
    incl = (ri <= ci) if reverse else (ri >= ci)
    strict = (ri < ci) if reverse else (ri > ci)
    eye = (ri == ci).astype(f32)
    diag_blk = (ri // INV_BLOCK) == (ci // INV_BLOCK)
    tri = incl.astype(f32)
    last_row = 0 if reverse else CHUNK - 1
    dir_off = N_HEADS if reverse else 0
    neg_a = -jnp.exp(prm_ref[0:1, :])
    dt_bias = prm_ref[1:2, :]

    def chunk_body(c, carry):
        cc = (nchunks - 1 - c) if reverse else c
        r0 = pl.multiple_of(cc * CHUNK, CHUNK)
        ab = ab_ref[pl.ds(r0, CHUNK), :]
        g = neg_a * jax.nn.softplus(ab + dt_bias)
        beta = jax.nn.sigmoid(ab)
        gc = jnp.dot(tri, g, precision=lax.Precision.HIGHEST, preferred_element_type=f32)
        g_last = gc[last_row:last_row + 1, :]
        e_gc = jnp.exp(gc)
        e_rest = jnp.exp(g_last - gc)
        e_last = jnp.exp(g_last)
        gc_t = gc.T
        heads = range(N_HEADS)
        rows = pl.ds(r0, CHUNK)
        cols = [slice(hd * HEAD_D, (hd + 1) * HEAD_D) for hd in heads]
        ga = [dir_off + hd for hd in heads]
        b_col = [beta[:, 2 * N_HEADS + g_:2 * N_HEADS + g_ + 1] for g_ in ga]
        eg_col = [e_gc[:, g_:g_ + 1] for g_ in ga]
        qh = [q_ref[rows, cs] for cs in cols]
        kh = [k_ref[rows, cs] for cs in cols]
        vh = [v_ref[rows, cs] for cs in cols]
        khb = [x.astype(bf16) for x in kh]
        gram = [_dot_nt(jnp.concatenate([qh[h], kh[h]], axis=0).astype(bf16), khb[h]) for h in heads]
        decay = [jnp.exp(jnp.where(incl, gc[:, g_:g_ + 1] - gc_t[g_:g_ + 1, :], -jnp.inf)) for g_ in ga]
        a_low = [jnp.where(strict, gram[h][CHUNK:] * b_col[h] * decay[h], 0.0) for h in heads]
        qk = [(gram[h][:CHUNK] * decay[h]).astype(bf16) for h in heads]
        t_inv = _unit_tri_inverse(a_low, eye, diag_blk)
        rhs = [jnp.concatenate([vh[h] * b_col[h], kh[h] * b_col[h] * eg_col[h]], axis=1) for h in heads]
        uw = [_bdot(t_inv[h], rhs[h]) for h in heads]
        s_old = [s_scr[h] for h in heads]
        s_b = [x.astype(bf16) for x in s_old]
        ws = [_dot(jnp.concatenate([uw[h][:, HEAD_D:], qh[h] * eg_col[h]], axis=0).astype(bf16), s_b[h])
              for h in heads]
        v_new = [(uw[h][:, :HEAD_D] - ws[h][:CHUNK]).astype(bf16) for h in heads]
        for h in heads:
            o_ref[rows, cols[h]] = ws[h][CHUNK:] + _dot(qk[h], v_new[h])
        for h in heads:
            kd = (kh[h] * e_rest[:, ga[h]:ga[h] + 1]).astype(bf16)
            s_scr[h] = s_old[h] * e_last[:, ga[h]:ga[h] + 1] + _dot_tn(kd, v_new[h])
        return carry

    lax.fori_loop(0, nchunks, chunk_body, 0)

    if emit_state:
        @pl.when(t == tiles_per_seq - 1)
        def _():
            sfin_ref[...] = s_scr[...]


def _gdn_scan(path, qkv, proj, prm, s0, layer, reverse, emit_state):
    tile = SEQ_TILE
    tps = path.seq_len // tile
    d = 1 if reverse else 0

    def tok(s, t):
        return s * tps + ((tps - 1 - t) if reverse else t)

    in_specs = [
        pl.BlockSpec((tile, GDN_W), lambda s, t: (tok(s, t), 0)),
        pl.BlockSpec((tile, GDN_W), lambda s, t: (tok(s, t), 1)),
        pl.BlockSpec((tile, GDN_W), lambda s, t: (tok(s, t), 2)),
        pl.BlockSpec((tile, LANES), lambda s, t: (tok(s, t), P_AB // LANES)),
        pl.BlockSpec((2, LANES), lambda s, t: (0, 0)),
    ]
    args = [qkv, qkv, qkv, proj, prm]
    if s0 is not None:
        in_specs.append(pl.BlockSpec((None, None, None, N_HEADS, HEAD_D, HEAD_D),
                                     lambda s, t: (s, layer, d, 0, 0, 0)))
        args.append(s0)
    out_specs = [pl.BlockSpec((tile, GDN_W), lambda s, t: (tok(s, t), 0))]
    out_shape = [jax.ShapeDtypeStruct((path.tokens, GDN_W), f32)]
    if emit_state:
        out_specs.append(pl.BlockSpec((None, N_HEADS, HEAD_D, HEAD_D), lambda s, t: (s, 0, 0, 0)))
        out_shape.append(jax.ShapeDtypeStruct((path.nseq, N_HEADS, HEAD_D, HEAD_D), f32))
    kern = functools.partial(_gdn_kernel, reverse=reverse, tiles_per_seq=tps, zero_init=s0 is None,
                             emit_state=emit_state)
    return pl.pallas_call(
        kern, grid=(path.nseq, tps), in_specs=in_specs, out_specs=out_specs, out_shape=out_shape,
        scratch_shapes=[pltpu.VMEM((N_HEADS, HEAD_D, HEAD_D), f32)],
        compiler_params=_cparams(("arbitrary", "arbitrary")), name="gdn_scan")(*args)


def _gdn_combine_kernel(of_ref, ob_ref, z_ref, g_ref, o_ref):
    g = g_ref[...]
    for hd in range(N_HEADS):
        cols = slice(hd * HEAD_D, (hd + 1) * HEAD_D)
        y = _rms(of_ref[:, cols] + ob_ref[:, cols], g)
        o_ref[:, cols] = (y * jax.nn.silu(z_ref[:, cols])).astype(o_ref.dtype)


def _gdn_combine(o_f, o_b, proj, g):
    t = o_f.shape[0]
    tile = 512
    return pl.pallas_call(
        _gdn_combine_kernel, grid=(t // tile,),
        in_specs=[pl.BlockSpec((tile, GDN_W), lambda i: (i, 0)),
                  pl.BlockSpec((tile, GDN_W), lambda i: (i, 0)),
                  pl.BlockSpec((tile, GDN_W), lambda i: (i, P_Z // GDN_W)),
                  pl.BlockSpec((1, HEAD_D), lambda i: (0, 0))],
        out_specs=pl.BlockSpec((tile, GDN_W), lambda i: (i, 0)),
        out_shape=jax.ShapeDtypeStruct((t, GDN_W), bf16),
        compiler_params=_cparams(("arbitrary",)), name="gdn_combine")(o_f, o_b, proj, g.reshape(1, HEAD_D))


def _lru_kernel(*refs, zero_init, emit_state):
    it = iter(refs)
    lx_ref, lg_ref, cw_ref, cb_ref = next(it), next(it), next(it), next(it)
    wa_ref, wi_ref, ba_ref, bi_ref, lam_ref = next(it), next(it), next(it), next(it), next(it)
    h0_ref = None if zero_init else next(it)
    y_ref = next(it)
    hfin_ref = next(it) if emit_state else None
    a_scr, b_scr = next(it), next(it)

    seq = lx_ref.shape[0]
    blk = 256
    nblk = seq // blk
    w_cat = jnp.concatenate([wa_ref[0], wi_ref[0], wa_ref[1], wi_ref[1]], axis=1).astype(bf16)
    bias_cat = jnp.concatenate([ba_ref[0:1], bi_ref[0:1], ba_ref[1:2], bi_ref[1:2]], axis=1)
    sp = jax.nn.softplus(-lam_ref[...])
    cw = cw_ref[...]
    cb = cb_ref[...]

    def gates_body(i, carry):
        r0 = pl.multiple_of(i * blk, blk)
        r_top = pl.multiple_of(jnp.maximum(r0 - SUBLANES, 0), SUBLANES)
        r_bot = pl.multiple_of(jnp.minimum(r0 + blk, seq - SUBLANES), SUBLANES)
        top = lx_ref[pl.ds(r_top, SUBLANES), :] * jnp.where(i > 0, 1.0, 0.0)
        bot = lx_ref[pl.ds(r_bot, SUBLANES), :] * jnp.where(i < nblk - 1, 1.0, 0.0)
        win = jnp.concatenate([top, lx_ref[pl.ds(r0, blk), :], bot], axis=0)
        xl = _conv_window(win, cw) + cb
        pre = _dot(xl.astype(bf16), w_cat) + bias_cat
        for d in range(2):
            r = jax.nn.sigmoid(pre[:, (2 * d) * LRU_BW:(2 * d + 1) * LRU_BW])
            ig = jax.nn.sigmoid(pre[:, (2 * d + 1) * LRU_BW:(2 * d + 2) * LRU_BW])
            a = jnp.exp(-RG_C * r * sp[d:d + 1])
            a_scr[d, pl.ds(r0, blk), :] = a
            b_scr[d, pl.ds(r0, blk), :] = jnp.sqrt(1.0 - a * a) * (ig * xl)
        return carry

    lax.fori_loop(0, nblk, gates_body, 0)

    row = lax.broadcasted_iota(jnp.int32, (CHUNK, LRU_BW), 0)
    nchunks = seq // CHUNK

    def chunk_scan(a, b, h_prev, reverse):
        s = 1
        while s < CHUNK:
            if reverse:
                keep = row < CHUNK - s
                a_sh = jnp.where(keep, pltpu.roll(a, CHUNK - s, 0), 1.0)
                b_sh = jnp.where(keep, pltpu.roll(b, CHUNK - s, 0), 0.0)
            else:
                keep = row >= s
                a_sh = jnp.where(keep, pltpu.roll(a, s, 0), 1.0)
                b_sh = jnp.where(keep, pltpu.roll(b, s, 0), 0.0)
            b = a * b_sh + b
            a = a * a_sh
            s *= 2
        return b + a * h_prev

    def scan_body(c, carry):
        hf, hb = carry
        rf = pl.multiple_of(c * CHUNK, CHUNK)
        rb = pl.multiple_of((nchunks - 1 - c) * CHUNK, CHUNK)
        h_f = chunk_scan(a_scr[0, pl.ds(rf, CHUNK), :], b_scr[0, pl.ds(rf, CHUNK), :], hf, False)
        h_b = chunk_scan(a_scr[1, pl.ds(rb, CHUNK), :], b_scr[1, pl.ds(rb, CHUNK), :], hb, True)
        b_scr[0, pl.ds(rf, CHUNK), :] = h_f
        b_scr[1, pl.ds(rb, CHUNK), :] = h_b
        return h_f[CHUNK - 1:CHUNK, :], h_b[0:1, :]

    if zero_init:
        init = (jnp.zeros((1, LRU_BW), f32), jnp.zeros((1, LRU_BW), f32))
    else:
        init = (h0_ref[0:1, :], h0_ref[1:2, :])
    hf, hb = lax.fori_loop(0, nchunks, scan_body, init)
    if emit_state:
        hfin_ref[0:1, :] = hf
        hfin_ref[1:2, :] = hb

    def out_body(i, carry):
        r0 = pl.multiple_of(i * blk, blk)
        h = b_scr[0, pl.ds(r0, blk), :] + b_scr[1, pl.ds(r0, blk), :]
        y_ref[pl.ds(r0, blk), :] = (h * jax.nn.gelu(lg_ref[pl.ds(r0, blk), :])).astype(y_ref.dtype)
        return carry

    lax.fori_loop(0, nblk, out_body, 0)


def _lru(path, proj, lp, h0, layer, emit_state):
    seq = path.seq_len
    proj3 = proj.reshape(path.nseq, seq, P_COLS)
    bx, bg = P_LX // LRU_BW, P_LG // LRU_BW
    in_specs = [
        pl.BlockSpec((None, seq, LRU_BW), lambda s, b: (s, 0, bx + b)),
        pl.BlockSpec((None, seq, LRU_BW), lambda s, b: (s, 0, bg + b)),
        pl.BlockSpec((CONV_W, LRU_BW), lambda s, b: (0, b)),
        pl.BlockSpec((1, LRU_BW), lambda s, b: (0, b)),
        pl.BlockSpec((2, None, LRU_BW, LRU_BW), lambda s, b: (0, b, 0, 0)),
        pl.BlockSpec((2, None, LRU_BW, LRU_BW), lambda s, b: (0, b, 0, 0)),
        pl.BlockSpec((2, LRU_BW), lambda s, b: (0, b)),
        pl.BlockSpec((2, LRU_BW), lambda s, b: (0, b)),
        pl.BlockSpec((2, LRU_BW), lambda s, b: (0, b)),
    ]
    args = [proj3, proj3, lp['lru_conv_w'], lp['lru_conv_b'].reshape(1, LRU_W), lp['lru_wa'], lp['lru_wi'],
            lp['lru_ba'], lp['lru_bi'], lp['lru_lambda']]
    if h0 is not None:
        in_specs.append(pl.BlockSpec((None, None, 2, LRU_BW), lambda s, b: (s, layer, 0, b)))
        args.append(h0)
    out_specs = [pl.BlockSpec((None, seq, LRU_BW), lambda s, b: (s, 0, b))]
    out_shape = [jax.ShapeDtypeStruct((path.nseq, seq, LRU_W), bf16)]
    if emit_state:
        out_specs.append(pl.BlockSpec((None, 2, LRU_BW), lambda s, b: (s, 0, b)))
        out_shape.append(jax.ShapeDtypeStruct((path.nseq, 2, LRU_W), f32))
    kern = functools.partial(_lru_kernel, zero_init=h0 is None, emit_state=emit_state)
    outs = pl.pallas_call(
        kern, grid=(path.nseq, LRU_BLOCKS), in_specs=in_specs, out_specs=out_specs, out_shape=out_shape,
        scratch_shapes=[pltpu.VMEM((2, seq, LRU_BW), f32), pltpu.VMEM((2, seq, LRU_BW), f32)],
        compiler_params=_cparams(("arbitrary", "arbitrary")), name="rglru")(*args)
    y = outs[0].reshape(path.tokens, LRU_W)
    return (y, outs[1]) if emit_state else (y, None)


def _pad_cols(w, n):
    return jnp.pad(w, ((0, 0),) * (w.ndim - 1) + ((0, n - w.shape[-1]),))


def _pos_table():
    quarter = D_MODEL // 4
    freqs = jnp.exp(-math.log(10000.0) * jnp.arange(quarter, dtype=f32) / quarter)
    e = jnp.arange(GRID_W, dtype=f32)[:, None] * freqs
    return jnp.concatenate([jnp.sin(e), jnp.cos(e)], axis=-1)


def _mixer_and_ffn(path, x, layer, rows, mod3, lp, ffn, s0_gdn, h0_lru, emit_state, pos):
    if pos is not None:
        h, x = _norm_mod(path, x, lp['norm1_g'], mod3, layer, rows, 1, 0, pos=pos)
    else:
        (h,) = _norm_mod(path, x, lp['norm1_g'], mod3, layer, rows, 1, 0)
    proj = _matmul(h, lp['w_in'], 1024, 896, f32)
    qkv = _qkv_prep(path, proj, lp['gdn_conv_w'])
    outs_f = _gdn_scan(path, qkv, proj, lp['gdn_prm'], s0_gdn, layer, False, emit_state)
    outs_b = _gdn_scan(path, qkv, proj, lp['gdn_prm'], s0_gdn, layer, True, emit_state)
    o_mix = _gdn_combine(outs_f[0], outs_b[0], proj, lp['gdn_norm_g'])
    y_lru, h_fin = _lru(path, proj, lp, h0_lru, layer, emit_state)
    x = _out_proj(path, o_mix, y_lru, lp['w_out'], x, mod3, layer, rows)
    if ffn['kind'] == 'dense':
        (h2,) = _norm_mod(path, x, lp['norm2_g'], mod3, layer, rows, 4, 3)
        act = _gate_up(h2, ffn['w1'], ffn['w3'], 1024, 512)
        x = _down_proj(path, act, ffn['w2'], x, mod3, layer, rows, 512, ffn['w2'].shape[0] // 4)
    else:
        h2, route = _norm_mod(path, x, lp['norm2_g'], mod3, layer, rows, 4, 3, router=(ffn['wr'], ffn['br']),
                              h_dtype=f32)
        tok_of_row, gate_of_row, expert_of_tile, n_used, pos = _moe_route(route, path.tokens)
        h_rows = _row_gather(h2, tok_of_row, n_used)
        act = _moe_gate_up(h_rows, ffn['w1'], ffn['w3'], gate_of_row, expert_of_tile, n_used,
                           ffn['w1'].shape[2] // 2)
        y_rows = _moe_down(act, ffn['w2'], expert_of_tile, n_used)
        x = _moe_combine(path, y_rows, pos, x, mod3, layer, rows)
    s_fin = jnp.stack([outs_f[1], outs_b[1]], axis=1) if emit_state else None
    return x, s_fin, h_fin


def kernel(x_prompt, x_sample, state_gdn, state_lru, c, c_ctx, w_mod, b_mod, norm1_g, norm2_g, w_in, gdn_conv_w, gdn_A_log, gdn_dt_bias, gdn_norm_g, lru_conv_w, lru_conv_b, lru_wa, lru_ba, lru_wi, lru_bi, lru_lambda, w_out, ffd_w1, ffd_w3, ffd_w2, moe_wr, moe_br, moe_w1, moe_w3, moe_w2, final_g):
    depth = w_mod.shape[0]
    bp, sp_len, d = x_prompt.shape
    bs, ss_len, _ = x_sample.shape
    ctx = _Path(bp, sp_len, 0, False)
    lat = _Path(bs, ss_len, 1, True)

    rows = SUBLANES * ((1 + bs + SUBLANES - 1) // SUBLANES)
    cvec = jnp.zeros((rows, d), f32).at[0].set(c_ctx).at[1:1 + bs].set(c)
    mod3 = _modulation(cvec, w_mod, b_mod).reshape(depth * rows * 6, 1, d)

    pos = _pos_table()
    xp = x_prompt.reshape(ctx.tokens, d)
    xs = x_sample.reshape(lat.tokens, d)
    new_gdn, new_lru = [], []
    for l in range(depth):
        w_in_l = w_in[l].astype(bf16)
        n_ab = 4 * N_HEADS
        w_in_p = jnp.concatenate([w_in_l[:, :P_AB + n_ab], jnp.zeros((d, LANES - n_ab), bf16),
                                  w_in_l[:, P_AB + n_ab:]], axis=1)
        prm = jnp.stack([_pad_cols(gdn_A_log[l].reshape(1, -1), LANES)[0],
                         _pad_cols(gdn_dt_bias[l].reshape(1, -1), LANES)[0]])
        lp = {
            'norm1_g': norm1_g[l], 'norm2_g': norm2_g[l], 'w_in': w_in_p, 'gdn_conv_w': gdn_conv_w[l],
            'gdn_prm': prm, 'gdn_norm_g': gdn_norm_g[l], 'lru_conv_w': lru_conv_w[l], 'lru_conv_b': lru_conv_b[l],
            'lru_wa': lru_wa[l], 'lru_ba': lru_ba[l], 'lru_wi': lru_wi[l], 'lru_bi': lru_bi[l],
            'lru_lambda': lru_lambda[l], 'w_out': w_out[l].astype(bf16),
        }
        j = l // 2
        if l % 2 == 0:
            f = ffd_w1.shape[2]
            fp = 512 * ((f + 511) // 512)
            ffn = {'kind': 'dense', 'w1': _pad_cols(ffd_w1[j].astype(bf16), fp),
                   'w3': _pad_cols(ffd_w3[j].astype(bf16), fp),
                   'w2': jnp.pad(ffd_w2[j].astype(bf16), ((0, fp - f), (0, 0)))}
        else:
            ffn = {'kind': 'moe', 'wr': _pad_cols(moe_wr[j], LANES), 'br': _pad_cols(moe_br[j].reshape(1, -1), LANES),
                   'w1': moe_w1[j].astype(bf16), 'w3': moe_w3[j].astype(bf16), 'w2': moe_w2[j].astype(bf16)}
        xp, s_g, s_l = _mixer_and_ffn(ctx, xp, l, rows, mod3, lp, ffn, None, None, True, None)
        new_gdn.append(s_g)
        new_lru.append(s_l)
        xs, _, _ = _mixer_and_ffn(lat, xs, l, rows, mod3, lp, ffn, state_gdn, state_lru, False,
                                  pos if l == 0 else None)
    y_prompt = _final_norm(xp, final_g).reshape(bp, sp_len, d)
    y_sample = _final_norm(xs, final_g).reshape(bs, ss_len, d)
    return (y_prompt, y_sample, jnp.stack(new_gdn, axis=1), jnp.stack(new_lru, axis=1))
```

```python
import functools
import math

import jax
import jax.numpy as jnp
from jax import lax
from jax.experimental import pallas as pl
from jax.experimental.pallas import tpu as pltpu

f32 = jnp.float32
bf16 = jnp.bfloat16

D_MODEL = 2048
GRID_W = 64
N_HEADS = 8
HEAD_D = 128
GDN_W = N_HEADS * HEAD_D
LRU_W = 1024
LRU_BLOCKS = 8
LRU_BW = LRU_W // LRU_BLOCKS
CONV_W = 4
CONV_LEFT = 2
CHUNK = 64
RG_C = 8.0
N_EXPERTS = 8
EPS = 1e-6
LANES = 128
SUBLANES = 8

P_Q = 0
P_K = P_Q + GDN_W
P_V = P_K + GDN_W
P_Z = P_V + GDN_W
P_AB = P_Z + GDN_W
P_LX = P_AB + LANES
P_LG = P_LX + LRU_W
P_COLS = P_LG + LRU_W

SEQ_TILE = 256
VMEM_LIMIT = 48 * 1024 * 1024


def _cparams(sem):
    return pltpu.CompilerParams(dimension_semantics=sem, vmem_limit_bytes=VMEM_LIMIT)


def _dot(a, b):
    return jnp.dot(a, b, preferred_element_type=f32)


def _dot_nt(a, b, precision=None):
    return lax.dot_general(a, b, (((1,), (1,)), ((), ())), precision=precision, preferred_element_type=f32)


def _dot_tn(a, b):
    return lax.dot_general(a, b, (((0,), (0,)), ((), ())), preferred_element_type=f32)


def _mod_kernel(c_ref, w_ref, b_ref, o_ref):
    a = jax.nn.silu(c_ref[...]).astype(bf16)
    o_ref[...] = _dot(a, w_ref[...].astype(bf16)) + b_ref[...]


def _modulation(cvec, w_mod, b_mod):
    depth, d, n = w_mod.shape
    tn = 1024
    return pl.pallas_call(
        _mod_kernel,
        grid=(depth, n // tn),
        in_specs=[
            pl.BlockSpec((cvec.shape[0], d), lambda l, j: (0, 0)),
            pl.BlockSpec((None, d, tn), lambda l, j: (l, 0, j)),
            pl.BlockSpec((None, 1, tn), lambda l, j: (l, 0, j)),
        ],
        out_specs=pl.BlockSpec((None, cvec.shape[0], tn), lambda l, j: (l, 0, j)),
        out_shape=jax.ShapeDtypeStruct((depth, cvec.shape[0], n), f32),
        compiler_params=_cparams(("arbitrary", "arbitrary")),
        name="modulation",
    )(cvec, w_mod, b_mod.reshape(depth, 1, n))


class _Path:
    def __init__(self, nseq, seq_len, mod_row0, per_seq_mod):
        self.nseq = nseq
        self.seq_len = seq_len
        self.tokens = nseq * seq_len
        self.mod_row0 = mod_row0
        self.per_seq_mod = per_seq_mod

    def mod_row(self, tile_idx, tile):
        if self.per_seq_mod:
            return self.mod_row0 + tile_idx // (self.seq_len // tile)
        return self.mod_row0


def _mod_spec(path, layer, rows, part, tile, width, col_of=None, grid_rank=1):
    def imap(*idx):
        i = idx[0]
        j = idx[col_of] if col_of is not None else 0
        return ((layer * rows + path.mod_row(i, tile)) * 6 + part, 0, j)
    return pl.BlockSpec((None, 1, width), imap)


def _rms(x, g):
    return x * lax.rsqrt(jnp.mean(x * x, axis=-1, keepdims=True) + EPS) * g


def _norm_mod_kernel(*refs, with_pos, with_router, tile, tiles_per_seq):
    it = iter(refs)
    x_ref, g_ref, scale_ref, shift_ref = next(it), next(it), next(it), next(it)
    pos_ref = next(it) if with_pos else None
    wr_ref, br_ref = (next(it), next(it)) if with_router else (None, None)
    h_ref = next(it)
    xo_ref = next(it) if with_pos else None
    gates_ref = next(it) if with_router else None

    x = x_ref[...]
    if with_pos:
        rows_per_tile = tile // GRID_W
        row0 = (pl.program_id(0) % tiles_per_seq) * rows_per_tile
        half = D_MODEL // 2
        col_part = pos_ref[...]
        pieces = []
        for r in range(rows_per_tile):
            row_part = jnp.broadcast_to(pos_ref[pl.ds(row0 + r, 1), :], (GRID_W, half))
            pieces.append(jnp.concatenate([row_part, col_part], axis=1))
        x = x + jnp.concatenate(pieces, axis=0)
        xo_ref[...] = x
    h = _rms(x, g_ref[...]) * (1.0 + scale_ref[...]) + shift_ref[...]
    h_ref[...] = h.astype(h_ref.dtype)
    if with_router:
        logits = jnp.dot(h, wr_ref[...], precision=lax.Precision.HIGHEST, preferred_element_type=f32) + br_ref[...]
        lane = lax.broadcasted_iota(jnp.int32, logits.shape, 1)
        valid = lane < N_EXPERTS
        logits = jnp.where(valid, logits, -jnp.inf)
        e = jnp.exp(logits - jnp.max(logits, axis=-1, keepdims=True))
        p = e / jnp.sum(e, axis=-1, keepdims=True)
        p = jnp.where(valid, p, -1.0)
        m1 = jnp.max(p, axis=-1, keepdims=True)
        i1 = jnp.min(jnp.where(p == m1, lane, LANES), axis=-1, keepdims=True)
        p2 = jnp.where(lane == i1, -1.0, p)
        m2 = jnp.max(p2, axis=-1, keepdims=True)
        i2 = jnp.min(jnp.where(p2 == m2, lane, LANES), axis=-1, keepdims=True)
        den = m1 + m2
        gates_ref[...] = (jnp.where(lane == R_IDX, i1.astype(f32), 0.0) + jnp.where(lane == R_IDX + 1, i2.astype(f32), 0.0)
                          + jnp.where(lane == R_GATE, m1 / den, 0.0) + jnp.where(lane == R_GATE + 1, m2 / den, 0.0))


R_IDX, R_GATE = 0, 2


def _norm_mod(path, x, g, mod3, layer, rows, part_scale, part_shift, pos=None, router=None, h_dtype=bf16):
    tile = SEQ_TILE
    n = path.tokens // tile
    d = D_MODEL
    in_specs = [
        pl.BlockSpec((tile, d), lambda i: (i, 0)),
        pl.BlockSpec((1, d), lambda i: (0, 0)),
        _mod_spec(path, layer, rows, part_scale, tile, d),
        _mod_spec(path, layer, rows, part_shift, tile, d),
    ]
    args = [x, g.reshape(1, d), mod3, mod3]
    out_specs = [pl.BlockSpec((tile, d), lambda i: (i, 0))]
    out_shape = [jax.ShapeDtypeStruct((path.tokens, d), h_dtype)]
    if pos is not None:
        in_specs.append(pl.BlockSpec(pos.shape, lambda i: (0, 0)))
        args.append(pos)
        out_specs.append(pl.BlockSpec((tile, d), lambda i: (i, 0)))
        out_shape.append(jax.ShapeDtypeStruct((path.tokens, d), f32))
    if router is not None:
        wr, br = router
        in_specs += [pl.BlockSpec((d, LANES), lambda i: (0, 0)), pl.BlockSpec((1, LANES), lambda i: (0, 0))]
        args += [wr, br]
        out_specs.append(pl.BlockSpec((tile, LANES), lambda i: (i, 0)))
        out_shape.append(jax.ShapeDtypeStruct((path.tokens, LANES), f32))
    kern = functools.partial(_norm_mod_kernel, with_pos=pos is not None, with_router=router is not None,
                             tile=tile, tiles_per_seq=path.seq_len // tile)
    return pl.pallas_call(kern, grid=(n,), in_specs=in_specs, out_specs=out_specs, out_shape=out_shape,
                          compiler_params=_cparams(("arbitrary",)), name="norm_mod")(*args)


def _final_norm_kernel(x_ref, g_ref, o_ref):
    o_ref[...] = _rms(x_ref[...], g_ref[...])


def _final_norm(x, g):
    t, d = x.shape
    tile = 512
    return pl.pallas_call(
        _final_norm_kernel, grid=(t // tile,),
        in_specs=[pl.BlockSpec((tile, d), lambda i: (i, 0)), pl.BlockSpec((1, d), lambda i: (0, 0))],
        out_specs=pl.BlockSpec((tile, d), lambda i: (i, 0)),
        out_shape=jax.ShapeDtypeStruct((t, d), f32),
        compiler_params=_cparams(("arbitrary",)), name="final_norm")(x, g.reshape(1, d))


def _mm_kernel(a_ref, w_ref, o_ref):
    o_ref[...] = _dot(a_ref[...], w_ref[...]).astype(o_ref.dtype)


def _matmul(a, w, tm, tn, out_dtype):
    m, k = a.shape
    n = w.shape[1]
    return pl.pallas_call(
        _mm_kernel, grid=(m // tm, n // tn),
        in_specs=[pl.BlockSpec((tm, k), lambda i, j: (i, 0)), pl.BlockSpec((k, tn), lambda i, j: (0, j))],
        out_specs=pl.BlockSpec((tm, tn), lambda i, j: (i, j)),
        out_shape=jax.ShapeDtypeStruct((m, n), out_dtype),
        compiler_params=_cparams(("arbitrary", "arbitrary")), name="matmul")(a, w)


def _out_proj_kernel(a1_ref, a2_ref, w1_ref, w2_ref, x_ref, gate_ref, o_ref):
    acc = _dot(a1_ref[...], w1_ref[...]) + _dot(a2_ref[...], w2_ref[...])
    o_ref[...] = x_ref[...] + gate_ref[...] * acc


def _out_proj(path, a1, a2, w, x, mod3, layer, rows):
    tm, tn = 1024, 512
    m, d = x.shape
    k1, k2 = a1.shape[1], a2.shape[1]
    return pl.pallas_call(
        _out_proj_kernel, grid=(m // tm, d // tn),
        in_specs=[
            pl.BlockSpec((tm, k1), lambda i, j: (i, 0)),
            pl.BlockSpec((tm, k2), lambda i, j: (i, 0)),
            pl.BlockSpec((k1, tn), lambda i, j: (0, j)),
            pl.BlockSpec((k2, tn), lambda i, j: (k1 // k2, j)),
            pl.BlockSpec((tm, tn), lambda i, j: (i, j)),
            _mod_spec(path, layer, rows, 2, tm, tn, col_of=1),
        ],
        out_specs=pl.BlockSpec((tm, tn), lambda i, j: (i, j)),
        out_shape=jax.ShapeDtypeStruct((m, d), f32),
        compiler_params=_cparams(("arbitrary", "arbitrary")), name="out_proj")(a1, a2, w, w, x, mod3)


def _gate_up_kernel(h_ref, w1_ref, w3_ref, o_ref):
    h = h_ref[...]
    a = _dot(h, w1_ref[...])
    b = _dot(h, w3_ref[...])
    o_ref[...] = (jax.nn.silu(a) * b).astype(o_ref.dtype)


def _gate_up(h, w1, w3, tm, tn):
    m, k = h.shape
    n = w1.shape[1]
    return pl.pallas_call(
        _gate_up_kernel, grid=(m // tm, n // tn),
        in_specs=[pl.BlockSpec((tm, k), lambda i, j: (i, 0)),
                  pl.BlockSpec((k, tn), lambda i, j: (0, j)),
                  pl.BlockSpec((k, tn), lambda i, j: (0, j))],
        out_specs=pl.BlockSpec((tm, tn), lambda i, j: (i, j)),
        out_shape=jax.ShapeDtypeStruct((m, n), bf16),
        compiler_params=_cparams(("arbitrary", "arbitrary")), name="gate_up")(h, w1, w3)


MOE_TILE = 256
DMA_UNROLL = 8


def _moe_route(route, n_tok):
    tm = MOE_TILE
    n_rows = 2 * n_tok + N_EXPERTS * tm
    e_idx = jnp.clip(route[:, R_IDX:R_IDX + 2].astype(jnp.int32), 0, N_EXPERTS - 1).reshape(-1)
    gate = route[:, R_GATE:R_GATE + 2].reshape(-1)
    one_hot = (e_idx[:, None] == jnp.arange(N_EXPERTS, dtype=jnp.int32)[None, :]).astype(jnp.int32)
    csum = jnp.cumsum(one_hot, axis=0)
    rank = jnp.take_along_axis(csum, e_idx[:, None], axis=1)[:, 0] - 1
    padded = ((csum[-1] + tm - 1) // tm) * tm
    ends = jnp.cumsum(padded)
    pos = (ends - padded)[e_idx] + rank
    tok_of_row = jnp.zeros((n_rows,), jnp.int32).at[pos].set(jnp.arange(2 * n_tok, dtype=jnp.int32) // 2)
    gate_of_row = jnp.zeros((n_rows,), f32).at[pos].set(gate)
    tile_start = jnp.arange(n_rows // tm, dtype=jnp.int32) * tm
    expert_of_tile = jnp.minimum(jnp.sum(tile_start[:, None] >= ends[None, :], axis=1), N_EXPERTS - 1)
    n_used = (ends[-1:] // tm).astype(jnp.int32)
    return tok_of_row, gate_of_row.reshape(n_rows, 1), expert_of_tile.astype(jnp.int32), n_used, pos.astype(jnp.int32)


def _row_copy(src_hbm, src_row, dst_vmem, dst_row, sem):
    return pltpu.make_async_copy(src_hbm.at[pl.ds(src_row, 1), :], dst_vmem.at[pl.ds(dst_row, 1), :], sem)


def _row_gather_kernel(tok_ref, nused_ref, h_hbm, o_ref, sem):
    i = pl.program_id(0)
    tm = o_ref.shape[0]

    @pl.when(i < nused_ref[0])
    def _():
        def issue(r, c):
            _row_copy(h_hbm, tok_ref[i * tm + r], o_ref, r, sem).start()
            return c
        lax.fori_loop(0, tm, issue, 0, unroll=DMA_UNROLL)
        pltpu.make_async_copy(h_hbm.at[pl.ds(0, tm), :], o_ref, sem).wait()

    @pl.when(i >= nused_ref[0])
    def _():
        o_ref[...] = jnp.zeros_like(o_ref)


def _row_gather(h, tok_of_row, n_used):
    tm = MOE_TILE
    n_rows = tok_of_row.shape[0]
    d = h.shape[1]
    return pl.pallas_call(
        _row_gather_kernel,
        grid_spec=pltpu.PrefetchScalarGridSpec(
            num_scalar_prefetch=2, grid=(n_rows // tm,),
            in_specs=[pl.BlockSpec(memory_space=pl.ANY)],
            out_specs=pl.BlockSpec((tm, d), lambda i, tok, nu: (i, 0)),
            scratch_shapes=[pltpu.SemaphoreType.DMA]),
        out_shape=jax.ShapeDtypeStruct((n_rows, d), h.dtype),
        compiler_params=_cparams(("arbitrary",)), name="moe_row_gather")(tok_of_row, n_used, h)


def _moe_gate_up_kernel(eot_ref, nused_ref, x_ref, w1_ref, w3_ref, g_ref, o_ref):
    i = pl.program_id(1)

    @pl.when(i < nused_ref[0])
    def _():
        x = x_ref[...].astype(bf16)
        a = _dot(x, w1_ref[...])
        b = _dot(x, w3_ref[...])
        o_ref[...] = (g_ref[...] * (jax.nn.silu(a) * b)).astype(o_ref.dtype)

    @pl.when(i >= nused_ref[0])
    def _():
        o_ref[...] = jnp.zeros_like(o_ref)


def _moe_gate_up(rows, w1, w3, gate_of_row, expert_of_tile, n_used, tn):
    tm = MOE_TILE
    n_rows, k = rows.shape
    f = w1.shape[2]
    return pl.pallas_call(
        _moe_gate_up_kernel,
        grid_spec=pltpu.PrefetchScalarGridSpec(
            num_scalar_prefetch=2, grid=(f // tn, n_rows // tm),
            in_specs=[pl.BlockSpec((tm, k), lambda j, i, eot, nu: (i, 0)),
                      pl.BlockSpec((None, k, tn), lambda j, i, eot, nu: (eot[i], 0, j)),
                      pl.BlockSpec((None, k, tn), lambda j, i, eot, nu: (eot[i], 0, j)),
                      pl.BlockSpec((tm, 1), lambda j, i, eot, nu: (i, 0))],
            out_specs=pl.BlockSpec((tm, tn), lambda j, i, eot, nu: (i, j))),
        out_shape=jax.ShapeDtypeStruct((n_rows, f), bf16),
        compiler_params=_cparams(("arbitrary", "arbitrary")), name="moe_gate_up")(
            expert_of_tile, n_used, rows, w1, w3, gate_of_row)


def _moe_down_kernel(eot_ref, nused_ref, a_ref, w_ref, o_ref):
    i = pl.program_id(0)

    @pl.when(i < nused_ref[0])
    def _():
        o_ref[...] = _dot(a_ref[...], w_ref[...])

    @pl.when(i >= nused_ref[0])
    def _():
        o_ref[...] = jnp.zeros_like(o_ref)


def _moe_down(act, w2, expert_of_tile, n_used):
    tm = MOE_TILE
    n_rows, f = act.shape
    d = w2.shape[2]
    return pl.pallas_call(
        _moe_down_kernel,
        grid_spec=pltpu.PrefetchScalarGridSpec(
            num_scalar_prefetch=2, grid=(n_rows // tm,),
            in_specs=[pl.BlockSpec((tm, f), lambda i, eot, nu: (i, 0)),
                      pl.BlockSpec((None, f, d), lambda i, eot, nu: (eot[i], 0, 0))],
            out_specs=pl.BlockSpec((tm, d), lambda i, eot, nu: (i, 0))),
        out_shape=jax.ShapeDtypeStruct((n_rows, d), f32),
        compiler_params=_cparams(("arbitrary",)), name="moe_down")(expert_of_tile, n_used, act, w2)


def _moe_combine_kernel(pos_ref, y_hbm, x_ref, gate_ref, *rest):
    fg_ref = rest[0] if len(rest) == 4 else None
    o_ref, buf, sem = rest[-3:]
    i = pl.program_id(0)
    tm = o_ref.shape[0]

    def issue(r, c):
        a = 2 * (i * tm + r)
        _row_copy(y_hbm, pos_ref[a], buf.at[0], r, sem).start()
        _row_copy(y_hbm, pos_ref[a + 1], buf.at[1], r, sem).start()
        return c
    lax.fori_loop(0, tm, issue, 0, unroll=DMA_UNROLL)
    for b in range(2):
        pltpu.make_async_copy(y_hbm.at[pl.ds(0, tm), :], buf.at[b], sem).wait()
    y = x_ref[...] + gate_ref[...] * (buf[0] + buf[1])
    o_ref[...] = y if fg_ref is None else _rms(y, fg_ref[...])


def _moe_combine(path, y_rows, pos, x, mod3, layer, rows, final_g=None):
    tm = SEQ_TILE
    m, d = x.shape
    in_specs = [pl.BlockSpec(memory_space=pl.ANY),
                pl.BlockSpec((tm, d), lambda i, pos: (i, 0)),
                pl.BlockSpec((None, 1, d), lambda i, pos: ((layer * rows + path.mod_row(i, tm)) * 6 + 5, 0, 0))]
    args = [pos, y_rows, x, mod3]
    if final_g is not None:
        in_specs.append(pl.BlockSpec((1, d), lambda i, pos: (0, 0)))
        args.append(final_g.reshape(1, d))
    return pl.pallas_call(
        _moe_combine_kernel,
        grid_spec=pltpu.PrefetchScalarGridSpec(
            num_scalar_prefetch=1, grid=(m // tm,), in_specs=in_specs,
            out_specs=pl.BlockSpec((tm, d), lambda i, pos: (i, 0)),
            scratch_shapes=[pltpu.VMEM((2, tm, d), f32), pltpu.SemaphoreType.DMA]),
        out_shape=jax.ShapeDtypeStruct((m, d), f32),
        compiler_params=_cparams(("arbitrary",)), name="moe_combine")(*args)


def _down_kernel(a_ref, w_ref, x_ref, gate_ref, o_ref, acc_ref, *, nk):
    kk = pl.program_id(1)

    @pl.when(kk == 0)
    def _():
        acc_ref[...] = jnp.zeros_like(acc_ref)

    acc_ref[...] += _dot(a_ref[...], w_ref[...])

    @pl.when(kk == nk - 1)
    def _():
        o_ref[...] = x_ref[...] + gate_ref[...] * acc_ref[...]


def _down_proj(path, act, w2, x, mod3, layer, rows, tm, tk):
    m, d = x.shape
    kdim = act.shape[1]
    nk = kdim // tk
    return pl.pallas_call(
        functools.partial(_down_kernel, nk=nk), grid=(m // tm, nk),
        in_specs=[pl.BlockSpec((tm, tk), lambda i, k: (i, k)),
                  pl.BlockSpec((tk, d), lambda i, k: (k, 0)),
                  pl.BlockSpec((tm, d), lambda i, k: (i, 0)),
                  _mod_spec(path, layer, rows, 5, tm, d)],
        out_specs=pl.BlockSpec((tm, d), lambda i, k: (i, 0)),
        out_shape=jax.ShapeDtypeStruct((m, d), f32),
        scratch_shapes=[pltpu.VMEM((tm, d), f32)],
        compiler_params=_cparams(("arbitrary", "arbitrary")), name="down_proj")(act, w2, x, mod3)


def _conv_window(win, w):
    n = win.shape[0] - 2 * SUBLANES
    acc = None
    for j in range(CONV_W):
        off = SUBLANES + j - CONV_LEFT
        term = win[off:off + n] * w[j:j + 1]
        acc = term if acc is None else acc + term
    return acc


def _qkv_prep_kernel(prev_ref, cur_ref, next_ref, w_ref, o_ref, *, tiles_per_seq):
    t = pl.program_id(0) % tiles_per_seq
    kind = pl.program_id(1)
    has_prev = (t > 0).astype(f32)
    has_next = (t < tiles_per_seq - 1).astype(f32)
    scale = jnp.where(kind == 0, HEAD_D ** -0.5, 1.0).astype(f32)
    tile = cur_ref.shape[0]
    for c in range(tile // CHUNK):
        r0 = c * CHUNK
        for hd in range(N_HEADS):
            cols = slice(hd * HEAD_D, (hd + 1) * HEAD_D)
            top = prev_ref[:, cols] * has_prev if c == 0 else cur_ref[r0 - SUBLANES:r0, cols]
            bot = (next_ref[:, cols] * has_next if r0 + CHUNK == tile
                   else cur_ref[r0 + CHUNK:r0 + CHUNK + SUBLANES, cols])
            win = jnp.concatenate([top, cur_ref[r0:r0 + CHUNK, cols], bot], axis=0)
            y = jax.nn.silu(_conv_window(win, w_ref[:, cols]))
            nrm = lax.rsqrt(jnp.sum(y * y, axis=-1, keepdims=True) + EPS) * scale
            y = y * jnp.where(kind < 2, nrm, 1.0)
            o_ref[r0:r0 + CHUNK, cols] = y


def _qkv_prep(path, proj, conv_w):
    tile = SEQ_TILE
    n = path.tokens // tile
    tps = path.seq_len // tile
    rb = tile // SUBLANES
    last = path.tokens // SUBLANES - 1
    return pl.pallas_call(
        functools.partial(_qkv_prep_kernel, tiles_per_seq=tps), grid=(n, 3),
        in_specs=[
            pl.BlockSpec((SUBLANES, GDN_W), lambda i, j: (jnp.maximum(i * rb - 1, 0), j)),
            pl.BlockSpec((tile, GDN_W), lambda i, j: (i, j)),
            pl.BlockSpec((SUBLANES, GDN_W), lambda i, j: (jnp.minimum((i + 1) * rb, last), j)),
            pl.BlockSpec((CONV_W, GDN_W), lambda i, j: (0, j)),
        ],
        out_specs=pl.BlockSpec((tile, GDN_W), lambda i, j: (i, j)),
        out_shape=jax.ShapeDtypeStruct((path.tokens, 3 * GDN_W), f32),
        compiler_params=_cparams(("arbitrary", "arbitrary")), name="qkv_prep")(proj, proj, proj, conv_w)


INV_BLOCK = 16


def _bdot(a, b):
    return _dot(a.astype(bf16), b.astype(bf16))


def _unit_tri_inverse(a_list, eye, diag_blk):
    p = [-jnp.where(diag_blk, a, 0.0) for a in a_list]
    x = [eye + pi for pi in p]
    span = 2
    while span < INV_BLOCK:
        p = [_bdot(pi, pi) for pi in p]
        x = [xi + _bdot(xi, pi) for xi, pi in zip(x, p)]
        span *= 2
    ia = [(eye + a).astype(bf16) for a in a_list]
    span = INV_BLOCK
    while span < a_list[0].shape[0]:
        r = [eye - _dot(iai, xi.astype(bf16)) for iai, xi in zip(ia, x)]
        x = [xi + _bdot(xi, ri) for xi, ri in zip(x, r)]
        span *= 2
    return x


def _gdn_kernel(*refs, reverse, tiles_per_seq, zero_init, emit_state):
    it = iter(refs)
    q_ref, k_ref, v_ref, ab_ref, prm_ref = next(it), next(it), next(it), next(it), next(it)
    s0_ref = None if zero_init else next(it)
    o_ref = next(it)
    sfin_ref = next(it) if emit_state else None
    s_scr = next(it)

    t = pl.program_id(1)
    nchunks = q_ref.shape[0] // CHUNK

    @pl.when(t == 0)
    def _():
        if zero_init:
            s_scr[...] = jnp.zeros_like(s_scr)
        else:
            s_scr[...] = s0_ref[...]

    ri = lax.broadcasted_iota(jnp.int32, (CHUNK, CHUNK), 0)
    ci = lax.broadcasted_iota(jnp.int32, (CHUNK, CHUNK), 1)
    incl = (ri <= ci) if reverse else (ri >= ci)
    strict = (ri < ci) if reverse else (ri > ci)
    eye = (ri == ci).astype(f32)
    diag_blk = (ri // INV_BLOCK) == (ci // INV_BLOCK)
    tri = incl.astype(f32)
    last_row = 0 if reverse else CHUNK - 1
    dir_off = N_HEADS if reverse else 0
    neg_a = -jnp.exp(prm_ref[0:1, :])
    dt_bias = prm_ref[1:2, :]

    heads = range(N_HEADS)
    cols = [slice(hd * HEAD_D, (hd + 1) * HEAD_D) for hd in heads]
    ga = [dir_off + hd for hd in heads]

    def prepare(chunks):
        gate = []
        for cc in chunks:
            rows = slice(cc * CHUNK, (cc + 1) * CHUNK)
            ab = ab_ref[rows, :]
            g = neg_a * jax.nn.softplus(ab + dt_bias)
            beta = jax.nn.sigmoid(ab)
            gc = jnp.dot(tri, g, precision=lax.Precision.HIGHEST, preferred_element_type=f32)
            g_last = gc[last_row:last_row + 1, :]
            gate.append({'beta': beta, 'gc': gc, 'gc_t': gc.T, 'e_gc': jnp.exp(gc),
                         'e_rest': jnp.exp(g_last - gc), 'e_last': jnp.exp(g_last)})
        probs = [(i, h) for i in range(len(chunks)) for h in heads]

        def rows_of(i):
            return slice(chunks[i] * CHUNK, (chunks[i] + 1) * CHUNK)

        def col(i, name, lane):
            return gate[i][name][:, lane:lane + 1]

        b_col = [col(i, 'beta', 2 * N_HEADS + ga[h]) for i, h in probs]
        eg_col = [col(i, 'e_gc', ga[h]) for i, h in probs]
        qh = [q_ref[rows_of(i), cols[h]] for i, h in probs]
        kh = [k_ref[rows_of(i), cols[h]] for i, h in probs]
        vh = [v_ref[rows_of(i), cols[h]] for i, h in probs]
        n = range(len(probs))
        gram = [_dot_nt(jnp.concatenate([qh[p], kh[p]], axis=0).astype(bf16), kh[p].astype(bf16)) for p in n]
        decay = [jnp.exp(jnp.where(incl, col(i, 'gc', ga[h]) - gate[i]['gc_t'][ga[h]:ga[h] + 1, :], -jnp.inf))
                 for i, h in probs]
        a_low = [jnp.where(strict, gram[p][CHUNK:] * b_col[p] * decay[p], 0.0) for p in n]
        qk = [(gram[p][:CHUNK] * decay[p]).astype(bf16) for p in n]
        t_inv = _unit_tri_inverse(a_low, eye, diag_blk)
        rhs = [jnp.concatenate([vh[p] * b_col[p], kh[p] * b_col[p] * eg_col[p]], axis=1) for p in n]
        uw = [_bdot(t_inv[p], rhs[p]) for p in n]
        w_qg = [jnp.concatenate([uw[p][:, HEAD_D:], qh[p] * eg_col[p]], axis=0).astype(bf16) for p in n]
        kd = [(kh[p] * col(i, 'e_rest', ga[h])).astype(bf16) for p, (i, h) in enumerate(probs)]
        e_last = [col(i, 'e_last', ga[h]) for i, h in probs]

        def per_chunk(lst):
            return [lst[i * N_HEADS:(i + 1) * N_HEADS] for i in range(len(chunks))]
        return {'u': per_chunk([x[:, :HEAD_D] for x in uw]), 'w_qg': per_chunk(w_qg), 'qk': per_chunk(qk),
                'kd': per_chunk(kd), 'e_last': per_chunk(e_last)}

    order = list(range(nchunks))[::-1] if reverse else list(range(nchunks))
    pre = prepare(order)
    state = [s_scr[h] for h in heads]
    for i, cc in enumerate(order):
        rows = slice(cc * CHUNK, (cc + 1) * CHUNK)
        s_b = [x.astype(bf16) for x in state]
        ws = [_dot(pre['w_qg'][i][h], s_b[h]) for h in heads]
        v_new = [(pre['u'][i][h] - ws[h][:CHUNK]).astype(bf16) for h in heads]
        for h in heads:
            o_ref[rows, cols[h]] = ws[h][CHUNK:] + _dot(pre['qk'][i][h], v_new[h])
        state = [state[h] * pre['e_last'][i][h] + _dot_tn(pre['kd'][i][h], v_new[h]) for h in heads]
    for h in heads:
        s_scr[h] = state[h]

    if emit_state:
        @pl.when(t == tiles_per_seq - 1)
        def _():
            sfin_ref[...] = s_scr[...]


def _gdn_scan(path, qkv, proj, prm, s0, layer, reverse, emit_state):
    tile = SEQ_TILE
    tps = path.seq_len // tile
    d = 1 if reverse else 0

    def tok(s, t):
        return s * tps + ((tps - 1 - t) if reverse else t)

    in_specs = [
        pl.BlockSpec((tile, GDN_W), lambda s, t: (tok(s, t), 0)),
        pl.BlockSpec((tile, GDN_W), lambda s, t: (tok(s, t), 1)),
        pl.BlockSpec((tile, GDN_W), lambda s, t: (tok(s, t), 2)),
        pl.BlockSpec((tile, LANES), lambda s, t: (tok(s, t), P_AB // LANES)),
        pl.BlockSpec((2, LANES), lambda s, t: (0, 0)),
    ]
    args = [qkv, qkv, qkv, proj, prm]
    if s0 is not None:
        in_specs.append(pl.BlockSpec((None, None, None, N_HEADS, HEAD_D, HEAD_D),
                                     lambda s, t: (s, layer, d, 0, 0, 0)))
        args.append(s0)
    out_specs = [pl.BlockSpec((tile, GDN_W), lambda s, t: (tok(s, t), 0))]
    out_shape = [jax.ShapeDtypeStruct((path.tokens, GDN_W), f32)]
    if emit_state:
        out_specs.append(pl.BlockSpec((None, N_HEADS, HEAD_D, HEAD_D), lambda s, t: (s, 0, 0, 0)))
        out_shape.append(jax.ShapeDtypeStruct((path.nseq, N_HEADS, HEAD_D, HEAD_D), f32))
    kern = functools.partial(_gdn_kernel, reverse=reverse, tiles_per_seq=tps, zero_init=s0 is None,
                             emit_state=emit_state)
    return pl.pallas_call(
        kern, grid=(path.nseq, tps), in_specs=in_specs, out_specs=out_specs, out_shape=out_shape,
        scratch_shapes=[pltpu.VMEM((N_HEADS, HEAD_D, HEAD_D), f32)],
        compiler_params=_cparams(("arbitrary", "arbitrary")), name="gdn_scan")(*args)


def _gdn_combine_kernel(of_ref, ob_ref, z_ref, g_ref, o_ref):
    g = g_ref[...]
    for hd in range(N_HEADS):
        cols = slice(hd * HEAD_D, (hd + 1) * HEAD_D)
        y = _rms(of_ref[:, cols] + ob_ref[:, cols], g)
        o_ref[:, cols] = (y * jax.nn.silu(z_ref[:, cols])).astype(o_ref.dtype)


def _gdn_combine(o_f, o_b, proj, g):
    t = o_f.shape[0]
    tile = 512
    return pl.pallas_call(
        _gdn_combine_kernel, grid=(t // tile,),
        in_specs=[pl.BlockSpec((tile, GDN_W), lambda i: (i, 0)),
                  pl.BlockSpec((tile, GDN_W), lambda i: (i, 0)),
                  pl.BlockSpec((tile, GDN_W), lambda i: (i, P_Z // GDN_W)),
                  pl.BlockSpec((1, HEAD_D), lambda i: (0, 0))],
        out_specs=pl.BlockSpec((tile, GDN_W), lambda i: (i, 0)),
        out_shape=jax.ShapeDtypeStruct((t, GDN_W), bf16),
        compiler_params=_cparams(("arbitrary",)), name="gdn_combine")(o_f, o_b, proj, g.reshape(1, HEAD_D))


def _lru_kernel(*refs, zero_init, emit_state):
    it = iter(refs)
    lx_ref, lg_ref, cw_ref, cb_ref = next(it), next(it), next(it), next(it)
    wa_ref, wi_ref, ba_ref, bi_ref, lam_ref = next(it), next(it), next(it), next(it), next(it)
    h0_ref = None if zero_init else next(it)
    y_ref = next(it)
    hfin_ref = next(it) if emit_state else None
    a_scr, b_scr = next(it), next(it)

    seq = lx_ref.shape[0]
    blk = 256
    nblk = seq // blk
    w_cat = jnp.concatenate([wa_ref[0], wi_ref[0], wa_ref[1], wi_ref[1]], axis=1).astype(bf16)
    bias_cat = jnp.concatenate([ba_ref[0:1], bi_ref[0:1], ba_ref[1:2], bi_ref[1:2]], axis=1)
    sp = jax.nn.softplus(-lam_ref[...])
    cw = cw_ref[...]
    cb = cb_ref[...]

    def gates_body(i, carry):
        r0 = pl.multiple_of(i * blk, blk)
        r_top = pl.multiple_of(jnp.maximum(r0 - SUBLANES, 0), SUBLANES)
        r_bot = pl.multiple_of(jnp.minimum(r0 + blk, seq - SUBLANES), SUBLANES)
        top = lx_ref[pl.ds(r_top, SUBLANES), :] * jnp.where(i > 0, 1.0, 0.0)
        bot = lx_ref[pl.ds(r_bot, SUBLANES), :] * jnp.where(i < nblk - 1, 1.0, 0.0)
        win = jnp.concatenate([top, lx_ref[pl.ds(r0, blk), :], bot], axis=0)
        xl = _conv_window(win, cw) + cb
        pre = _dot(xl.astype(bf16), w_cat) + bias_cat
        for d in range(2):
            r = jax.nn.sigmoid(pre[:, (2 * d) * LRU_BW:(2 * d + 1) * LRU_BW])
            ig = jax.nn.sigmoid(pre[:, (2 * d + 1) * LRU_BW:(2 * d + 2) * LRU_BW])
            a = jnp.exp(-RG_C * r * sp[d:d + 1])
            a_scr[d, pl.ds(r0, blk), :] = a
            b_scr[d, pl.ds(r0, blk), :] = jnp.sqrt(1.0 - a * a) * (ig * xl)
        return carry

    lax.fori_loop(0, nblk, gates_body, 0)

    row = lax.broadcasted_iota(jnp.int32, (CHUNK, LRU_BW), 0)
    nchunks = seq // CHUNK

    def chunk_scan(a, b, h_prev, reverse):
        s = 1
        while s < CHUNK:
            if reverse:
                keep = row < CHUNK - s
                a_sh = jnp.where(keep, pltpu.roll(a, CHUNK - s, 0), 1.0)
                b_sh = jnp.where(keep, pltpu.roll(b, CHUNK - s, 0), 0.0)
            else:
                keep = row >= s
                a_sh = jnp.where(keep, pltpu.roll(a, s, 0), 1.0)
                b_sh = jnp.where(keep, pltpu.roll(b, s, 0), 0.0)
            b = a * b_sh + b
            a = a * a_sh
            s *= 2
        return b + a * h_prev

    def scan_body(c, carry):
        hf, hb = carry
        rf = pl.multiple_of(c * CHUNK, CHUNK)
        rb = pl.multiple_of((nchunks - 1 - c) * CHUNK, CHUNK)
        h_f = chunk_scan(a_scr[0, pl.ds(rf, CHUNK), :], b_scr[0, pl.ds(rf, CHUNK), :], hf, False)
        h_b = chunk_scan(a_scr[1, pl.ds(rb, CHUNK), :], b_scr[1, pl.ds(rb, CHUNK), :], hb, True)
        b_scr[0, pl.ds(rf, CHUNK), :] = h_f
        b_scr[1, pl.ds(rb, CHUNK), :] = h_b
        return h_f[CHUNK - 1:CHUNK, :], h_b[0:1, :]

    if zero_init:
        init = (jnp.zeros((1, LRU_BW), f32), jnp.zeros((1, LRU_BW), f32))
    else:
        init = (h0_ref[0:1, :], h0_ref[1:2, :])
    hf, hb = lax.fori_loop(0, nchunks, scan_body, init)
    if emit_state:
        hfin_ref[0:1, :] = hf
        hfin_ref[1:2, :] = hb

    def out_body(i, carry):
        r0 = pl.multiple_of(i * blk, blk)
        h = b_scr[0, pl.ds(r0, blk), :] + b_scr[1, pl.ds(r0, blk), :]
        y_ref[pl.ds(r0, blk), :] = (h * jax.nn.gelu(lg_ref[pl.ds(r0, blk), :])).astype(y_ref.dtype)
        return carry

    lax.fori_loop(0, nblk, out_body, 0)


def _lru(path, proj, lp, h0, layer, emit_state):
    seq = path.seq_len
    proj3 = proj.reshape(path.nseq, seq, P_COLS)
    bx, bg = P_LX // LRU_BW, P_LG // LRU_BW
    in_specs = [
        pl.BlockSpec((None, seq, LRU_BW), lambda s, b: (s, 0, bx + b)),
        pl.BlockSpec((None, seq, LRU_BW), lambda s, b: (s, 0, bg + b)),
        pl.BlockSpec((CONV_W, LRU_BW), lambda s, b: (0, b)),
        pl.BlockSpec((1, LRU_BW), lambda s, b: (0, b)),
        pl.BlockSpec((2, None, LRU_BW, LRU_BW), lambda s, b: (0, b, 0, 0)),
        pl.BlockSpec((2, None, LRU_BW, LRU_BW), lambda s, b: (0, b, 0, 0)),
        pl.BlockSpec((2, LRU_BW), lambda s, b: (0, b)),
        pl.BlockSpec((2, LRU_BW), lambda s, b: (0, b)),
        pl.BlockSpec((2, LRU_BW), lambda s, b: (0, b)),
    ]
    args = [proj3, proj3, lp['lru_conv_w'], lp['lru_conv_b'].reshape(1, LRU_W), lp['lru_wa'], lp['lru_wi'],
            lp['lru_ba'], lp['lru_bi'], lp['lru_lambda']]
    if h0 is not None:
        in_specs.append(pl.BlockSpec((None, None, 2, LRU_BW), lambda s, b: (s, layer, 0, b)))
        args.append(h0)
    out_specs = [pl.BlockSpec((None, seq, LRU_BW), lambda s, b: (s, 0, b))]
    out_shape = [jax.ShapeDtypeStruct((path.nseq, seq, LRU_W), bf16)]
    if emit_state:
        out_specs.append(pl.BlockSpec((None, 2, LRU_BW), lambda s, b: (s, 0, b)))
        out_shape.append(jax.ShapeDtypeStruct((path.nseq, 2, LRU_W), f32))
    kern = functools.partial(_lru_kernel, zero_init=h0 is None, emit_state=emit_state)
    outs = pl.pallas_call(
        kern, grid=(path.nseq, LRU_BLOCKS), in_specs=in_specs, out_specs=out_specs, out_shape=out_shape,
        scratch_shapes=[pltpu.VMEM((2, seq, LRU_BW), f32), pltpu.VMEM((2, seq, LRU_BW), f32)],
        compiler_params=_cparams(("arbitrary", "arbitrary")), name="rglru")(*args)
    y = outs[0].reshape(path.tokens, LRU_W)
    return (y, outs[1]) if emit_state else (y, None)


def _pad_cols(w, n):
    return jnp.pad(w, ((0, 0),) * (w.ndim - 1) + ((0, n - w.shape[-1]),))


def _pos_table():
    quarter = D_MODEL // 4
    freqs = jnp.exp(-math.log(10000.0) * jnp.arange(quarter, dtype=f32) / quarter)
    e = jnp.arange(GRID_W, dtype=f32)[:, None] * freqs
    return jnp.concatenate([jnp.sin(e), jnp.cos(e)], axis=-1)


def _mixer_and_ffn(path, x, layer, rows, mod3, lp, ffn, s0_gdn, h0_lru, emit_state, pos):
    if pos is not None:
        h, x = _norm_mod(path, x, lp['norm1_g'], mod3, layer, rows, 1, 0, pos=pos)
    else:
        (h,) = _norm_mod(path, x, lp['norm1_g'], mod3, layer, rows, 1, 0)
    proj = _matmul(h, lp['w_in'], 1024, 896, f32)
    qkv = _qkv_prep(path, proj, lp['gdn_conv_w'])
    outs_f = _gdn_scan(path, qkv, proj, lp['gdn_prm'], s0_gdn, layer, False, emit_state)
    outs_b = _gdn_scan(path, qkv, proj, lp['gdn_prm'], s0_gdn, layer, True, emit_state)
    o_mix = _gdn_combine(outs_f[0], outs_b[0], proj, lp['gdn_norm_g'])
    y_lru, h_fin = _lru(path, proj, lp, h0_lru, layer, emit_state)
    x = _out_proj(path, o_mix, y_lru, lp['w_out'], x, mod3, layer, rows)
    if ffn['kind'] == 'dense':
        (h2,) = _norm_mod(path, x, lp['norm2_g'], mod3, layer, rows, 4, 3)
        act = _gate_up(h2, ffn['w1'], ffn['w3'], 1024, 512)
        x = _down_proj(path, act, ffn['w2'], x, mod3, layer, rows, 512, ffn['w2'].shape[0] // 4)
    else:
        h2, route = _norm_mod(path, x, lp['norm2_g'], mod3, layer, rows, 4, 3, router=(ffn['wr'], ffn['br']),
                              h_dtype=f32)
        tok_of_row, gate_of_row, expert_of_tile, n_used, pos = _moe_route(route, path.tokens)
        h_rows = _row_gather(h2, tok_of_row, n_used)
        act = _moe_gate_up(h_rows, ffn['w1'], ffn['w3'], gate_of_row, expert_of_tile, n_used,
                           ffn['w1'].shape[2] // 2)
        y_rows = _moe_down(act, ffn['w2'], expert_of_tile, n_used)
        x = _moe_combine(path, y_rows, pos, x, mod3, layer, rows, final_g=ffn.get('final_g'))
    s_fin = jnp.stack([outs_f[1], outs_b[1]], axis=1) if emit_state else None
    return x, s_fin, h_fin


def kernel(x_prompt, x_sample, state_gdn, state_lru, c, c_ctx, w_mod, b_mod, norm1_g, norm2_g, w_in, gdn_conv_w, gdn_A_log, gdn_dt_bias, gdn_norm_g, lru_conv_w, lru_conv_b, lru_wa, lru_ba, lru_wi, lru_bi, lru_lambda, w_out, ffd_w1, ffd_w3, ffd_w2, moe_wr, moe_br, moe_w1, moe_w3, moe_w2, final_g):
    depth = w_mod.shape[0]
    bp, sp_len, d = x_prompt.shape
    bs, ss_len, _ = x_sample.shape
    ctx = _Path(bp, sp_len, 0, False)
    lat = _Path(bs, ss_len, 1, True)

    rows = SUBLANES * ((1 + bs + SUBLANES - 1) // SUBLANES)
    cvec = jnp.zeros((rows, d), f32).at[0].set(c_ctx).at[1:1 + bs].set(c)
    mod3 = _modulation(cvec, w_mod, b_mod).reshape(depth * rows * 6, 1, d)

    pos = _pos_table()
    xp = x_prompt.reshape(ctx.tokens, d)
    xs = x_sample.reshape(lat.tokens, d)
    new_gdn, new_lru = [], []
    for l in range(depth):
        w_in_l = w_in[l].astype(bf16)
        n_ab = 4 * N_HEADS
        w_in_p = jnp.concatenate([w_in_l[:, :P_AB + n_ab], jnp.zeros((d, LANES - n_ab), bf16),
                                  w_in_l[:, P_AB + n_ab:]], axis=1)
        prm = jnp.stack([_pad_cols(gdn_A_log[l].reshape(1, -1), LANES)[0],
                         _pad_cols(gdn_dt_bias[l].reshape(1, -1), LANES)[0]])
        lp = {
            'norm1_g': norm1_g[l], 'norm2_g': norm2_g[l], 'w_in': w_in_p, 'gdn_conv_w': gdn_conv_w[l],
            'gdn_prm': prm, 'gdn_norm_g': gdn_norm_g[l], 'lru_conv_w': lru_conv_w[l], 'lru_conv_b': lru_conv_b[l],
            'lru_wa': lru_wa[l], 'lru_ba': lru_ba[l], 'lru_wi': lru_wi[l], 'lru_bi': lru_bi[l],
            'lru_lambda': lru_lambda[l], 'w_out': w_out[l].astype(bf16),
        }
        j = l // 2
        if l % 2 == 0:
            f = ffd_w1.shape[2]
            fp = 512 * ((f + 511) // 512)
            ffn = {'kind': 'dense', 'w1': _pad_cols(ffd_w1[j].astype(bf16), fp),
                   'w3': _pad_cols(ffd_w3[j].astype(bf16), fp),
                   'w2': jnp.pad(ffd_w2[j].astype(bf16), ((0, fp - f), (0, 0)))}
        else:
            ffn = {'kind': 'moe', 'wr': _pad_cols(moe_wr[j], LANES), 'br': _pad_cols(moe_br[j].reshape(1, -1), LANES),
                   'w1': moe_w1[j].astype(bf16), 'w3': moe_w3[j].astype(bf16), 'w2': moe_w2[j].astype(bf16)}
            if l == depth - 1:
                ffn['final_g'] = final_g
        xp, s_g, s_l = _mixer_and_ffn(ctx, xp, l, rows, mod3, lp, ffn, None, None, True, None)
        new_gdn.append(s_g)
        new_lru.append(s_l)
        xs, _, _ = _mixer_and_ffn(lat, xs, l, rows, mod3, lp, ffn, state_gdn, state_lru, False,
                                  pos if l == 0 else None)
    if depth % 2 == 1:
        xp, xs = _final_norm(xp, final_g), _final_norm(xs, final_g)
    y_prompt = xp.reshape(bp, sp_len, d)
    y_sample = xs.reshape(bs, ss_len, d)
    return (y_prompt, y_sample, jnp.stack(new_gdn, axis=1), jnp.stack(new_lru, axis=1))
```

```python
import functools
import math

import jax
import jax.numpy as jnp
from jax import lax
from jax.experimental import pallas as pl
from jax.experimental.pallas import tpu as pltpu

f32 = jnp.float32
bf16 = jnp.bfloat16

D_MODEL = 2048
GRID_W = 64
N_HEADS = 8
HEAD_D = 128
GDN_W = N_HEADS * HEAD_D
LRU_W = 1024
LRU_BLOCKS = 8
LRU_BW = LRU_W // LRU_BLOCKS
CONV_W = 4
CONV_LEFT = 2
CHUNK = 64
RG_C = 8.0
N_EXPERTS = 8
EPS = 1e-6
LANES = 128
SUBLANES = 8

P_Q = 0
P_K = P_Q + GDN_W
P_V = P_K + GDN_W
P_Z = P_V + GDN_W
P_AB = P_Z + GDN_W
P_LX = P_AB + LANES
P_LG = P_LX + LRU_W
P_COLS = P_LG + LRU_W

SEQ_TILE = 256
VMEM_LIMIT = 48 * 1024 * 1024


def _cparams(sem):
    return pltpu.CompilerParams(dimension_semantics=sem, vmem_limit_bytes=VMEM_LIMIT)


def _dot(a, b):
    return jnp.dot(a, b, preferred_element_type=f32)


def _sigmoid(x):
    return 0.5 * jnp.tanh(0.5 * x) + 0.5


def _silu(x):
    return x * _sigmoid(x)


def _dot_nt(a, b, precision=None):
    return lax.dot_general(a, b, (((1,), (1,)), ((), ())), precision=precision, preferred_element_type=f32)


def _dot_tn(a, b):
    return lax.dot_general(a, b, (((0,), (0,)), ((), ())), preferred_element_type=f32)


def _mod_kernel(c_ref, w_ref, b_ref, o_ref):
    a = _silu(c_ref[...]).astype(bf16)
    o_ref[...] = _dot(a, w_ref[...].astype(bf16)) + b_ref[...]


def _modulation(cvec, w_mod, b_mod):
    depth, d, n = w_mod.shape
    tn = 1024
    return pl.pallas_call(
        _mod_kernel,
        grid=(depth, n // tn),
        in_specs=[
            pl.BlockSpec((cvec.shape[0], d), lambda l, j: (0, 0)),
            pl.BlockSpec((None, d, tn), lambda l, j: (l, 0, j)),
            pl.BlockSpec((None, 1, tn), lambda l, j: (l, 0, j)),
        ],
        out_specs=pl.BlockSpec((None, cvec.shape[0], tn), lambda l, j: (l, 0, j)),
        out_shape=jax.ShapeDtypeStruct((depth, cvec.shape[0], n), f32),
        compiler_params=_cparams(("arbitrary", "arbitrary")),
        name="modulation",
    )(cvec, w_mod, b_mod.reshape(depth, 1, n))


class _Path:
    def __init__(self, nseq, seq_len, mod_row0, per_seq_mod):
        self.nseq = nseq
        self.seq_len = seq_len
        self.tokens = nseq * seq_len
        self.mod_row0 = mod_row0
        self.per_seq_mod = per_seq_mod

    def mod_row(self, tile_idx, tile):
        if self.per_seq_mod:
            return self.mod_row0 + tile_idx // (self.seq_len // tile)
        return self.mod_row0


def _mod_spec(path, layer, rows, part, tile, width, col_of=None, grid_rank=1):
    def imap(*idx):
        i = idx[0]
        j = idx[col_of] if col_of is not None else 0
        return ((layer * rows + path.mod_row(i, tile)) * 6 + part, 0, j)
    return pl.BlockSpec((None, 1, width), imap)


def _rms(x, g):
    return x * lax.rsqrt(jnp.mean(x * x, axis=-1, keepdims=True) + EPS) * g


R_IDX, R_GATE = 0, 2


def _emit_modulated_norm(x, g_ref, scale_ref, shift_ref, h_ref, wr_ref=None, br_ref=None, route_ref=None):
    h = _rms(x, g_ref[...]) * (1.0 + scale_ref[...]) + shift_ref[...]
    h_ref[...] = h.astype(h_ref.dtype)
    if route_ref is None:
        return
    logits = jnp.dot(h, wr_ref[...], precision=lax.Precision.HIGHEST, preferred_element_type=f32) + br_ref[...]
    lane = lax.broadcasted_iota(jnp.int32, logits.shape, 1)
    valid = lane < N_EXPERTS
    logits = jnp.where(valid, logits, -jnp.inf)
    e = jnp.exp(logits - jnp.max(logits, axis=-1, keepdims=True))
    p = e / jnp.sum(e, axis=-1, keepdims=True)
    p = jnp.where(valid, p, -1.0)
    m1 = jnp.max(p, axis=-1, keepdims=True)
    i1 = jnp.min(jnp.where(p == m1, lane, LANES), axis=-1, keepdims=True)
    p2 = jnp.where(lane == i1, -1.0, p)
    m2 = jnp.max(p2, axis=-1, keepdims=True)
    i2 = jnp.min(jnp.where(p2 == m2, lane, LANES), axis=-1, keepdims=True)
    den = m1 + m2
    route_ref[...] = (jnp.where(lane == R_IDX, i1.astype(f32), 0.0) + jnp.where(lane == R_IDX + 1, i2.astype(f32), 0.0)
                      + jnp.where(lane == R_GATE, m1 / den, 0.0) + jnp.where(lane == R_GATE + 1, m2 / den, 0.0))


def _norm_io(path, g, mod3, layer, rows, part_scale, part_shift, tile, router, h_dtype):
    d = D_MODEL
    in_specs = [pl.BlockSpec((1, d), lambda *idx: (0, 0)),
                _mod_spec(path, layer, rows, part_scale, tile, d),
                _mod_spec(path, layer, rows, part_shift, tile, d)]
    args = [g.reshape(1, d), mod3, mod3]
    out_specs = [pl.BlockSpec((tile, d), lambda *idx: (idx[0], 0))]
    out_shape = [jax.ShapeDtypeStruct((path.tokens, d), h_dtype)]
    if router is not None:
        in_specs += [pl.BlockSpec((d, LANES), lambda *idx: (0, 0)), pl.BlockSpec((1, LANES), lambda *idx: (0, 0))]
        args += list(router)
        out_specs.append(pl.BlockSpec((tile, LANES), lambda *idx: (idx[0], 0)))
        out_shape.append(jax.ShapeDtypeStruct((path.tokens, LANES), f32))
    return in_specs, args, out_specs, out_shape


def _norm_mod_kernel(*refs, with_pos, with_router, tile, tiles_per_seq):
    it = iter(refs)
    x_ref, g_ref, scale_ref, shift_ref = next(it), next(it), next(it), next(it)
    pos_ref = next(it) if with_pos else None
    wr_ref, br_ref = (next(it), next(it)) if with_router else (None, None)
    h_ref = next(it)
    xo_ref = next(it) if with_pos else None
    gates_ref = next(it) if with_router else None

    x = x_ref[...]
    if with_pos:
        rows_per_tile = tile // GRID_W
        row0 = (pl.program_id(0) % tiles_per_seq) * rows_per_tile
        half = D_MODEL // 2
        col_part = pos_ref[...]
        pieces = []
        for r in range(rows_per_tile):
            row_part = jnp.broadcast_to(pos_ref[pl.ds(row0 + r, 1), :], (GRID_W, half))
            pieces.append(jnp.concatenate([row_part, col_part], axis=1))
        x = x + jnp.concatenate(pieces, axis=0)
        xo_ref[...] = x
    _emit_modulated_norm(x, g_ref, scale_ref, shift_ref, h_ref, wr_ref, br_ref, gates_ref)


def _norm_mod(path, x, g, mod3, layer, rows, part_scale, part_shift, pos=None, router=None, h_dtype=bf16):
    tile = SEQ_TILE
    n = path.tokens // tile
    d = D_MODEL
    in_specs = [
        pl.BlockSpec((tile, d), lambda i: (i, 0)),
        pl.BlockSpec((1, d), lambda i: (0, 0)),
        _mod_spec(path, layer, rows, part_scale, tile, d),
        _mod_spec(path, layer, rows, part_shift, tile, d),
    ]
    args = [x, g.reshape(1, d), mod3, mod3]
    out_specs = [pl.BlockSpec((tile, d), lambda i: (i, 0))]
    out_shape = [jax.ShapeDtypeStruct((path.tokens, d), h_dtype)]
    if pos is not None:
        in_specs.append(pl.BlockSpec(pos.shape, lambda i: (0, 0)))
        args.append(pos)
        out_specs.append(pl.BlockSpec((tile, d), lambda i: (i, 0)))
        out_shape.append(jax.ShapeDtypeStruct((path.tokens, d), f32))
    if router is not None:
        wr, br = router
        in_specs += [pl.BlockSpec((d, LANES), lambda i: (0, 0)), pl.BlockSpec((1, LANES), lambda i: (0, 0))]
        args += [wr, br]
        out_specs.append(pl.BlockSpec((tile, LANES), lambda i: (i, 0)))
        out_shape.append(jax.ShapeDtypeStruct((path.tokens, LANES), f32))
    kern = functools.partial(_norm_mod_kernel, with_pos=pos is not None, with_router=router is not None,
                             tile=tile, tiles_per_seq=path.seq_len // tile)
    return pl.pallas_call(kern, grid=(n,), in_specs=in_specs, out_specs=out_specs, out_shape=out_shape,
                          compiler_params=_cparams(("arbitrary",)), name="norm_mod")(*args)


def _final_norm_kernel(x_ref, g_ref, o_ref):
    o_ref[...] = _rms(x_ref[...], g_ref[...])


def _final_norm(x, g):
    t, d = x.shape
    tile = 512
    return pl.pallas_call(
        _final_norm_kernel, grid=(t // tile,),
        in_specs=[pl.BlockSpec((tile, d), lambda i: (i, 0)), pl.BlockSpec((1, d), lambda i: (0, 0))],
        out_specs=pl.BlockSpec((tile, d), lambda i: (i, 0)),
        out_shape=jax.ShapeDtypeStruct((t, d), f32),
        compiler_params=_cparams(("arbitrary",)), name="final_norm")(x, g.reshape(1, d))


def _mm_kernel(a_ref, w_ref, o_ref):
    o_ref[...] = _dot(a_ref[...], w_ref[...]).astype(o_ref.dtype)


def _matmul(a, w, tm, tn, out_dtype):
    m, k = a.shape
    n = w.shape[1]
    return pl.pallas_call(
        _mm_kernel, grid=(m // tm, n // tn),
        in_specs=[pl.BlockSpec((tm, k), lambda i, j: (i, 0)), pl.BlockSpec((k, tn), lambda i, j: (0, j))],
        out_specs=pl.BlockSpec((tm, tn), lambda i, j: (i, j)),
        out_shape=jax.ShapeDtypeStruct((m, n), out_dtype),
        compiler_params=_cparams(("arbitrary", "arbitrary")), name="matmul")(a, w)


def _out_proj_kernel(*refs, with_router):
    a1_ref, a2_ref, w1_ref, w2_ref, x_ref, gate_ref, g_ref, scale_ref, shift_ref = refs[:9]
    wr_ref, br_ref = refs[9:11] if with_router else (None, None)
    outs = refs[11:] if with_router else refs[9:]
    xo_ref, h_ref = outs[:2]
    route_ref = outs[2] if with_router else None
    acc = _dot(a1_ref[...], w1_ref[...]) + _dot(a2_ref[...], w2_ref[...])
    x = x_ref[...] + gate_ref[...] * acc
    xo_ref[...] = x
    _emit_modulated_norm(x, g_ref, scale_ref, shift_ref, h_ref, wr_ref, br_ref, route_ref)


def _out_proj(path, a1, a2, w, x, mod3, layer, rows, norm_g, router=None, h_dtype=bf16):
    tm = 512
    m, d = x.shape
    k1, k2 = a1.shape[1], a2.shape[1]
    n_in, n_args, n_out, n_shape = _norm_io(path, norm_g, mod3, layer, rows, 4, 3, tm, router, h_dtype)
    once = pl.Buffered(1)
    return pl.pallas_call(
        functools.partial(_out_proj_kernel, with_router=router is not None), grid=(m // tm,),
        in_specs=[
            pl.BlockSpec((tm, k1), lambda i: (i, 0)),
            pl.BlockSpec((tm, k2), lambda i: (i, 0)),
            pl.BlockSpec((k1, d), lambda i: (0, 0), pipeline_mode=once),
            pl.BlockSpec((k2, d), lambda i: (k1 // k2, 0), pipeline_mode=once),
            pl.BlockSpec((tm, d), lambda i: (i, 0)),
            _mod_spec(path, layer, rows, 2, tm, d),
        ] + n_in,
        out_specs=[pl.BlockSpec((tm, d), lambda i: (i, 0))] + n_out,
        out_shape=[jax.ShapeDtypeStruct((m, d), f32)] + n_shape,
        compiler_params=_cparams(("arbitrary",)), name="out_proj")(a1, a2, w, w, x, mod3, *n_args)


def _gate_up_kernel(h_ref, w1_ref, w3_ref, o_ref):
    h = h_ref[...]
    a = _dot(h, w1_ref[...])
    b = _dot(h, w3_ref[...])
    o_ref[...] = (_silu(a) * b).astype(o_ref.dtype)


def _gate_up(h, w1, w3, tm, tn):
    m, k = h.shape
    n = w1.shape[1]
    return pl.pallas_call(
        _gate_up_kernel, grid=(m // tm, n // tn),
        in_specs=[pl.BlockSpec((tm, k), lambda i, j: (i, 0)),
                  pl.BlockSpec((k, tn), lambda i, j: (0, j)),
                  pl.BlockSpec((k, tn), lambda i, j: (0, j))],
        out_specs=pl.BlockSpec((tm, tn), lambda i, j: (i, j)),
        out_shape=jax.ShapeDtypeStruct((m, n), bf16),
        compiler_params=_cparams(("arbitrary", "arbitrary")), name="gate_up")(h, w1, w3)


MOE_TILE = 256
DMA_UNROLL = 8


def _moe_route(route, n_tok):
    tm = MOE_TILE
    n_rows = 2 * n_tok + N_EXPERTS * tm
    e_idx = jnp.clip(route[:, R_IDX:R_IDX + 2].astype(jnp.int32), 0, N_EXPERTS - 1).reshape(-1)
    gate = route[:, R_GATE:R_GATE + 2].reshape(-1)
    one_hot = (e_idx[:, None] == jnp.arange(N_EXPERTS, dtype=jnp.int32)[None, :]).astype(jnp.int32)
    csum = jnp.cumsum(one_hot, axis=0)
    rank = jnp.take_along_axis(csum, e_idx[:, None], axis=1)[:, 0] - 1
    padded = ((csum[-1] + tm - 1) // tm) * tm
    ends = jnp.cumsum(padded)
    pos = (ends - padded)[e_idx] + rank
    tok_of_row = jnp.zeros((n_rows,), jnp.int32).at[pos].set(jnp.arange(2 * n_tok, dtype=jnp.int32) // 2)
    gate_of_row = jnp.zeros((n_rows,), f32).at[pos].set(gate)
    tile_start = jnp.arange(n_rows // tm, dtype=jnp.int32) * tm
    expert_of_tile = jnp.minimum(jnp.sum(tile_start[:, None] >= ends[None, :], axis=1), N_EXPERTS - 1)
    n_used = (ends[-1:] // tm).astype(jnp.int32)
    return tok_of_row, gate_of_row.reshape(n_rows, 1), expert_of_tile.astype(jnp.int32), n_used, pos.astype(jnp.int32)


def _row_copy(src_hbm, src_row, dst_vmem, dst_row, sem):
    return pltpu.make_async_copy(src_hbm.at[pl.ds(src_row, 1), :], dst_vmem.at[pl.ds(dst_row, 1), :], sem)


def _row_gather_kernel(tok_ref, nused_ref, h_hbm, o_hbm, zeros, sem, zsem):
    i = pl.program_id(0)
    last = pl.num_programs(0) - 1
    tm = zeros.shape[0]
    n_used = nused_ref[0]

    def tile_wait(slot):
        pltpu.make_async_copy(h_hbm.at[pl.ds(0, tm), :], o_hbm.at[pl.ds(0, tm), :], sem.at[slot]).wait()

    @pl.when(i == 0)
    def _():
        zeros[...] = jnp.zeros_like(zeros)

    @pl.when(i < n_used)
    def _():
        def issue(r, c):
            _row_copy(h_hbm, tok_ref[i * tm + r], o_hbm, i * tm + r, sem.at[i % 2]).start()
            return c
        lax.fori_loop(0, tm, issue, 0, unroll=DMA_UNROLL)

    @pl.when(i >= n_used)
    def _():
        fill = pltpu.make_async_copy(zeros, o_hbm.at[pl.ds(i * tm, tm), :], zsem)
        fill.start()
        fill.wait()

    @pl.when(jnp.logical_and(i > 0, i - 1 < n_used))
    def _():
        tile_wait((i + 1) % 2)

    @pl.when(jnp.logical_and(i == last, i < n_used))
    def _():
        tile_wait(i % 2)


def _row_gather(h, tok_of_row, n_used):
    tm = MOE_TILE
    n_rows = tok_of_row.shape[0]
    d = h.shape[1]
    return pl.pallas_call(
        _row_gather_kernel,
        grid_spec=pltpu.PrefetchScalarGridSpec(
            num_scalar_prefetch=2, grid=(n_rows // tm,),
            in_specs=[pl.BlockSpec(memory_space=pl.ANY)],
            out_specs=pl.BlockSpec(memory_space=pl.ANY),
            scratch_shapes=[pltpu.VMEM((tm, d), h.dtype), pltpu.SemaphoreType.DMA((2,)), pltpu.SemaphoreType.DMA]),
        out_shape=jax.ShapeDtypeStruct((n_rows, d), h.dtype),
        compiler_params=_cparams(("arbitrary",)), name="moe_row_gather")(tok_of_row, n_used, h)


def _moe_gate_up_kernel(eot_ref, nused_ref, x_ref, w1_ref, w3_ref, g_ref, o_ref):
    i = pl.program_id(1)

    @pl.when(i < nused_ref[0])
    def _():
        x = x_ref[...].astype(bf16)
        a = _dot(x, w1_ref[...])
        b = _dot(x, w3_ref[...])
        o_ref[...] = (g_ref[...] * (_silu(a) * b)).astype(o_ref.dtype)

    @pl.when(i >= nused_ref[0])
    def _():
        o_ref[...] = jnp.zeros_like(o_ref)


def _moe_gate_up(rows, w1, w3, gate_of_row, expert_of_tile, n_used, tn):
    tm = MOE_TILE
    n_rows, k = rows.shape
    f = w1.shape[2]
    return pl.pallas_call(
        _moe_gate_up_kernel,
        grid_spec=pltpu.PrefetchScalarGridSpec(
            num_scalar_prefetch=2, grid=(f // tn, n_rows // tm),
            in_specs=[pl.BlockSpec((tm, k), lambda j, i, eot, nu: (i, 0)),
                      pl.BlockSpec((None, k, tn), lambda j, i, eot, nu: (eot[i], 0, j)),
                      pl.BlockSpec((None, k, tn), lambda j, i, eot, nu: (eot[i], 0, j)),
                      pl.BlockSpec((tm, 1), lambda j, i, eot, nu: (i, 0))],
            out_specs=pl.BlockSpec((tm, tn), lambda j, i, eot, nu: (i, j))),
        out_shape=jax.ShapeDtypeStruct((n_rows, f), bf16),
        compiler_params=_cparams(("arbitrary", "arbitrary")), name="moe_gate_up")(
            expert_of_tile, n_used, rows, w1, w3, gate_of_row)


def _moe_down_kernel(eot_ref, nused_ref, a_ref, w_ref, o_ref):
    i = pl.program_id(0)

    @pl.when(i < nused_ref[0])
    def _():
        o_ref[...] = _dot(a_ref[...], w_ref[...])

    @pl.when(i >= nused_ref[0])
    def _():
        o_ref[...] = jnp.zeros_like(o_ref)


def _moe_down(act, w2, expert_of_tile, n_used):
    tm = MOE_TILE
    n_rows, f = act.shape
    d = w2.shape[2]
    return pl.pallas_call(
        _moe_down_kernel,
        grid_spec=pltpu.PrefetchScalarGridSpec(
            num_scalar_prefetch=2, grid=(n_rows // tm,),
            in_specs=[pl.BlockSpec((tm, f), lambda i, eot, nu: (i, 0)),
                      pl.BlockSpec((None, f, d), lambda i, eot, nu: (eot[i], 0, 0))],
            out_specs=pl.BlockSpec((tm, d), lambda i, eot, nu: (i, 0))),
        out_shape=jax.ShapeDtypeStruct((n_rows, d), f32),
        compiler_params=_cparams(("arbitrary",)), name="moe_down")(expert_of_tile, n_used, act, w2)


def _moe_combine_kernel(pos_ref, y_hbm, x_ref, gate_ref, *rest):
    fg_ref = rest[0] if len(rest) == 4 else None
    o_ref, buf, sem = rest[-3:]
    i = pl.program_id(0)
    tm = o_ref.shape[0]

    def issue_tile(t, slot):
        def issue(r, c):
            a = 2 * (t * tm + r)
            _row_copy(y_hbm, pos_ref[a], buf.at[slot, 0], r, sem.at[slot]).start()
            _row_copy(y_hbm, pos_ref[a + 1], buf.at[slot, 1], r, sem.at[slot]).start()
            return c
        lax.fori_loop(0, tm, issue, 0, unroll=DMA_UNROLL)

    @pl.when(i == 0)
    def _():
        issue_tile(0, 0)

    @pl.when(i + 1 < pl.num_programs(0))
    def _():
        issue_tile(i + 1, (i + 1) % 2)

    slot = i % 2
    for b in range(2):
        pltpu.make_async_copy(y_hbm.at[pl.ds(0, tm), :], buf.at[slot, b], sem.at[slot]).wait()
    y = x_ref[...] + gate_ref[...] * (buf[slot, 0] + buf[slot, 1])
    o_ref[...] = y if fg_ref is None else _rms(y, fg_ref[...])


def _moe_combine(path, y_rows, pos, x, mod3, layer, rows, final_g=None):
    tm = SEQ_TILE
    m, d = x.shape
    in_specs = [pl.BlockSpec(memory_space=pl.ANY),
                pl.BlockSpec((tm, d), lambda i, pos: (i, 0)),
                pl.BlockSpec((None, 1, d), lambda i, pos: ((layer * rows + path.mod_row(i, tm)) * 6 + 5, 0, 0))]
    args = [pos, y_rows, x, mod3]
    if final_g is not None:
        in_specs.append(pl.BlockSpec((1, d), lambda i, pos: (0, 0)))
        args.append(final_g.reshape(1, d))
    return pl.pallas_call(
        _moe_combine_kernel,
        grid_spec=pltpu.PrefetchScalarGridSpec(
            num_scalar_prefetch=1, grid=(m // tm,), in_specs=in_specs,
            out_specs=pl.BlockSpec((tm, d), lambda i, pos: (i, 0)),
            scratch_shapes=[pltpu.VMEM((2, 2, tm, d), f32), pltpu.SemaphoreType.DMA((2,))]),
        out_shape=jax.ShapeDtypeStruct((m, d), f32),
        compiler_params=_cparams(("arbitrary",)), name="moe_combine")(*args)


def _down_kernel(*refs, nk, with_norm):
    a_ref, w_ref, x_ref, gate_ref = refs[:4]
    g_ref, scale_ref, shift_ref = refs[4:7] if with_norm else (None, None, None)
    outs = refs[7:] if with_norm else refs[4:]
    o_ref = outs[0]
    h_ref = outs[1] if with_norm else None
    acc_ref = outs[-1]
    kk = pl.program_id(1)

    @pl.when(kk == 0)
    def _():
        acc_ref[...] = jnp.zeros_like(acc_ref)

    acc_ref[...] += _dot(a_ref[...], w_ref[...])

    @pl.when(kk == nk - 1)
    def _():
        x = x_ref[...] + gate_ref[...] * acc_ref[...]
        o_ref[...] = x
        if with_norm:
            _emit_modulated_norm(x, g_ref, scale_ref, shift_ref, h_ref)


def _down_proj(path, act, w2, x, mod3, layer, rows, tm, tk, next_norm_g=None):
    m, d = x.shape
    kdim = act.shape[1]
    nk = kdim // tk
    n_in, n_args, n_out, n_shape = ([], [], [], [])
    if next_norm_g is not None:
        n_in, n_args, n_out, n_shape = _norm_io(path, next_norm_g, mod3, layer + 1, rows, 1, 0, tm, None, bf16)
    return pl.pallas_call(
        functools.partial(_down_kernel, nk=nk, with_norm=next_norm_g is not None), grid=(m // tm, nk),
        in_specs=[pl.BlockSpec((tm, tk), lambda i, k: (i, k)),
                  pl.BlockSpec((tk, d), lambda i, k: (k, 0)),
                  pl.BlockSpec((tm, d), lambda i, k: (i, 0)),
                  _mod_spec(path, layer, rows, 5, tm, d)] + n_in,
        out_specs=[pl.BlockSpec((tm, d), lambda i, k: (i, 0))] + n_out,
        out_shape=[jax.ShapeDtypeStruct((m, d), f32)] + n_shape,
        scratch_shapes=[pltpu.VMEM((tm, d), f32)],
        compiler_params=_cparams(("arbitrary", "arbitrary")), name="down_proj")(act, w2, x, mod3, *n_args)


def _conv_window(win, w):
    n = win.shape[0] - 2 * SUBLANES
    acc = None
    for j in range(CONV_W):
        off = SUBLANES + j - CONV_LEFT
        term = win[off:off + n] * w[j:j + 1]
        acc = term if acc is None else acc + term
    return acc


def _qkv_prep_kernel(prev_ref, cur_ref, next_ref, w_ref, o_ref, *, tiles_per_seq):
    t = pl.program_id(0) % tiles_per_seq
    kind = pl.program_id(1)
    has_prev = (t > 0).astype(f32)
    has_next = (t < tiles_per_seq - 1).astype(f32)
    scale = jnp.where(kind == 0, HEAD_D ** -0.5, 1.0).astype(f32)
    tile = cur_ref.shape[0]
    for c in range(tile // CHUNK):
        r0 = c * CHUNK
        for hd in range(N_HEADS):
            cols = slice(hd * HEAD_D, (hd + 1) * HEAD_D)
            top = prev_ref[:, cols] * has_prev if c == 0 else cur_ref[r0 - SUBLANES:r0, cols]
            bot = (next_ref[:, cols] * has_next if r0 + CHUNK == tile
                   else cur_ref[r0 + CHUNK:r0 + CHUNK + SUBLANES, cols])
            win = jnp.concatenate([top, cur_ref[r0:r0 + CHUNK, cols], bot], axis=0)
            y = _silu(_conv_window(win, w_ref[:, cols]))
            nrm = lax.rsqrt(jnp.sum(y * y, axis=-1, keepdims=True) + EPS) * scale
            y = y * jnp.where(kind < 2, nrm, 1.0)
            o_ref[r0:r0 + CHUNK, cols] = y


def _qkv_prep(path, proj, conv_w):
    tile = SEQ_TILE
    n = path.tokens // tile
    tps = path.seq_len // tile
    rb = tile // SUBLANES
    last = path.tokens // SUBLANES - 1
    return pl.pallas_call(
        functools.partial(_qkv_prep_kernel, tiles_per_seq=tps), grid=(n, 3),
        in_specs=[
            pl.BlockSpec((SUBLANES, GDN_W), lambda i, j: (jnp.maximum(i * rb - 1, 0), j)),
            pl.BlockSpec((tile, GDN_W), lambda i, j: (i, j)),
            pl.BlockSpec((SUBLANES, GDN_W), lambda i, j: (jnp.minimum((i + 1) * rb, last), j)),
            pl.BlockSpec((CONV_W, GDN_W), lambda i, j: (0, j)),
        ],
        out_specs=pl.BlockSpec((tile, GDN_W), lambda i, j: (i, j)),
        out_shape=jax.ShapeDtypeStruct((path.tokens, 3 * GDN_W), f32),
        compiler_params=_cparams(("arbitrary", "arbitrary")), name="qkv_prep")(proj, proj, proj, conv_w)


INV_BLOCK = 16


def _bdot(a, b):
    return _dot(a.astype(bf16), b.astype(bf16))


def _unit_tri_inverse(a_list, eye, diag_blk):
    p = [-jnp.where(diag_blk, a, 0.0) for a in a_list]
    x = [eye + pi for pi in p]
    span = 2
    while span < INV_BLOCK:
        p = [_bdot(pi, pi) for pi in p]
        x = [xi + _bdot(xi, pi) for xi, pi in zip(x, p)]
        span *= 2
    ia = [(eye + a).astype(bf16) for a in a_list]
    span = INV_BLOCK
    while span < a_list[0].shape[0]:
        r = [eye - _dot(iai, xi.astype(bf16)) for iai, xi in zip(ia, x)]
        x = [xi + _bdot(xi, ri) for xi, ri in zip(x, r)]
        span *= 2
    return x


def _gdn_kernel(*refs, reverse, tiles_per_seq, zero_init, emit_state):
    it = iter(refs)
    q_ref, k_ref, v_ref, ab_ref, prm_ref = next(it), next(it), next(it), next(it), next(it)
    s0_ref = None if zero_init else next(it)
    o_ref = next(it)
    sfin_ref = next(it) if emit_state else None
    s_scr = next(it)

    t = pl.program_id(1)
    nchunks = q_ref.shape[0] // CHUNK

    @pl.when(t == 0)
    def _():
        if zero_init:
            s_scr[...] = jnp.zeros_like(s_scr)
        else:
            s_scr[...] = s0_ref[...]

    ri = lax.broadcasted_iota(jnp.int32, (CHUNK, CHUNK), 0)
    ci = lax.broadcasted_iota(jnp.int32, (CHUNK, CHUNK), 1)
    incl = (ri <= ci) if reverse else (ri >= ci)
    strict = (ri < ci) if reverse else (ri > ci)
    eye = (ri == ci).astype(f32)
    diag_blk = (ri // INV_BLOCK) == (ci // INV_BLOCK)
    tri = incl.astype(f32)
    last_row = 0 if reverse else CHUNK - 1
    dir_off = N_HEADS if reverse else 0
    neg_a = -jnp.exp(prm_ref[0:1, :])
    dt_bias = prm_ref[1:2, :]

    heads = range(N_HEADS)
    cols = [slice(hd * HEAD_D, (hd + 1) * HEAD_D) for hd in heads]
    ga = [dir_off + hd for hd in heads]

    def prepare(chunks):
        gate = []
        for cc in chunks:
            rows = slice(cc * CHUNK, (cc + 1) * CHUNK)
            ab = ab_ref[rows, :]
            g = neg_a * jax.nn.softplus(ab + dt_bias)
            beta = _sigmoid(ab)
            gc = jnp.dot(tri, g, precision=lax.Precision.HIGHEST, preferred_element_type=f32)
            g_last = gc[last_row:last_row + 1, :]
            gate.append({'beta': beta, 'gc': gc, 'gc_t': gc.T, 'e_gc': jnp.exp(gc),
                         'e_rest': jnp.exp(g_last - gc), 'e_last': jnp.exp(g_last)})
        probs = [(i, h) for i in range(len(chunks)) for h in heads]

        def rows_of(i):
            return slice(chunks[i] * CHUNK, (chunks[i] + 1) * CHUNK)

        def col(i, name, lane):
            return gate[i][name][:, lane:lane + 1]

        b_col = [col(i, 'beta', 2 * N_HEADS + ga[h]) for i, h in probs]
        eg_col = [col(i, 'e_gc', ga[h]) for i, h in probs]
        qh = [q_ref[rows_of(i), cols[h]] for i, h in probs]
        kh = [k_ref[rows_of(i), cols[h]] for i, h in probs]
        vh = [v_ref[rows_of(i), cols[h]] for i, h in probs]
        n = range(len(probs))
        gram = [_dot_nt(jnp.concatenate([qh[p], kh[p]], axis=0).astype(bf16), kh[p].astype(bf16)) for p in n]
        decay = [jnp.exp(jnp.where(incl, col(i, 'gc', ga[h]) - gate[i]['gc_t'][ga[h]:ga[h] + 1, :], -jnp.inf))
                 for i, h in probs]
        a_low = [jnp.where(strict, gram[p][CHUNK:] * b_col[p] * decay[p], 0.0) for p in n]
        qk = [(gram[p][:CHUNK] * decay[p]).astype(bf16) for p in n]
        t_inv = _unit_tri_inverse(a_low, eye, diag_blk)
        rhs = [jnp.concatenate([vh[p] * b_col[p], kh[p] * b_col[p] * eg_col[p]], axis=1) for p in n]
        uw = [_bdot(t_inv[p], rhs[p]) for p in n]
        w_qg = [jnp.concatenate([uw[p][:, HEAD_D:], qh[p] * eg_col[p]], axis=0).astype(bf16) for p in n]
        kd = [(kh[p] * col(i, 'e_rest', ga[h])).astype(bf16) for p, (i, h) in enumerate(probs)]
        e_last = [col(i, 'e_last', ga[h]) for i, h in probs]

        def per_chunk(lst):
            return [lst[i * N_HEADS:(i + 1) * N_HEADS] for i in range(len(chunks))]
        return {'u': per_chunk([x[:, :HEAD_D] for x in uw]), 'w_qg': per_chunk(w_qg), 'qk': per_chunk(qk),
                'kd': per_chunk(kd), 'e_last': per_chunk(e_last)}

    order = list(range(nchunks))[::-1] if reverse else list(range(nchunks))
    pre = prepare(order)
    state = [s_scr[h] for h in heads]
    for i, cc in enumerate(order):
        rows = slice(cc * CHUNK, (cc + 1) * CHUNK)
        s_b = [x.astype(bf16) for x in state]
        ws = [_dot(pre['w_qg'][i][h], s_b[h]) for h in heads]
        v_new = [(pre['u'][i][h] - ws[h][:CHUNK]).astype(bf16) for h in heads]
        for h in heads:
            o_ref[rows, cols[h]] = ws[h][CHUNK:] + _dot(pre['qk'][i][h], v_new[h])
        state = [state[h] * pre['e_last'][i][h] + _dot_tn(pre['kd'][i][h], v_new[h]) for h in heads]
    for h in heads:
        s_scr[h] = state[h]

    if emit_state:
        @pl.when(t == tiles_per_seq - 1)
        def _():
            sfin_ref[...] = s_scr[...]


def _gdn_scan(path, qkv, proj, prm, s0, layer, reverse, emit_state):
    tile = SEQ_TILE
    tps = path.seq_len // tile
    d = 1 if reverse else 0

    def tok(s, t):
        return s * tps + ((tps - 1 - t) if reverse else t)

    in_specs = [
        pl.BlockSpec((tile, GDN_W), lambda s, t: (tok(s, t), 0)),
        pl.BlockSpec((tile, GDN_W), lambda s, t: (tok(s, t), 1)),
        pl.BlockSpec((tile, GDN_W), lambda s, t: (tok(s, t), 2)),
        pl.BlockSpec((tile, LANES), lambda s, t: (tok(s, t), P_AB // LANES)),
        pl.BlockSpec((2, LANES), lambda s, t: (0, 0)),
    ]
    args = [qkv, qkv, qkv, proj, prm]
    if s0 is not None:
        in_specs.append(pl.BlockSpec((None, None, None, N_HEADS, HEAD_D, HEAD_D),
                                     lambda s, t: (s, layer, d, 0, 0, 0)))
        args.append(s0)
    out_specs = [pl.BlockSpec((tile, GDN_W), lambda s, t: (tok(s, t), 0))]
    out_shape = [jax.ShapeDtypeStruct((path.tokens, GDN_W), f32)]
    if emit_state:
        out_specs.append(pl.BlockSpec((None, N_HEADS, HEAD_D, HEAD_D), lambda s, t: (s, 0, 0, 0)))
        out_shape.append(jax.ShapeDtypeStruct((path.nseq, N_HEADS, HEAD_D, HEAD_D), f32))
    kern = functools.partial(_gdn_kernel, reverse=reverse, tiles_per_seq=tps, zero_init=s0 is None,
                             emit_state=emit_state)
    return pl.pallas_call(
        kern, grid=(path.nseq, tps), in_specs=in_specs, out_specs=out_specs, out_shape=out_shape,
        scratch_shapes=[pltpu.VMEM((N_HEADS, HEAD_D, HEAD_D), f32)],
        compiler_params=_cparams(("arbitrary", "arbitrary")), name="gdn_scan")(*args)


def _gdn_combine_kernel(of_ref, ob_ref, z_ref, g_ref, o_ref):
    g = g_ref[...]
    for hd in range(N_HEADS):
        cols = slice(hd * HEAD_D, (hd + 1) * HEAD_D)
        y = _rms(of_ref[:, cols] + ob_ref[:, cols], g)
        o_ref[:, cols] = (y * _silu(z_ref[:, cols])).astype(o_ref.dtype)


def _gdn_combine(o_f, o_b, proj, g):
    t = o_f.shape[0]
    tile = 512
    return pl.pallas_call(
        _gdn_combine_kernel, grid=(t // tile,),
        in_specs=[pl.BlockSpec((tile, GDN_W), lambda i: (i, 0)),
                  pl.BlockSpec((tile, GDN_W), lambda i: (i, 0)),
                  pl.BlockSpec((tile, GDN_W), lambda i: (i, P_Z // GDN_W)),
                  pl.BlockSpec((1, HEAD_D), lambda i: (0, 0))],
        out_specs=pl.BlockSpec((tile, GDN_W), lambda i: (i, 0)),
        out_shape=jax.ShapeDtypeStruct((t, GDN_W), bf16),
        compiler_params=_cparams(("arbitrary",)), name="gdn_combine")(o_f, o_b, proj, g.reshape(1, HEAD_D))


def _lru_kernel(*refs, zero_init, emit_state):
    it = iter(refs)
    lx_ref, lg_ref, cw_ref, cb_ref = next(it), next(it), next(it), next(it)
    wa_ref, wi_ref, ba_ref, bi_ref, lam_ref = next(it), next(it), next(it), next(it), next(it)
    h0_ref = None if zero_init else next(it)
    y_ref = next(it)
    hfin_ref = next(it) if emit_state else None
    a_scr, b_scr = next(it), next(it)

    seq = lx_ref.shape[0]
    blk = 256
    nblk = seq // blk
    w_cat = jnp.concatenate([wa_ref[0], wi_ref[0], wa_ref[1], wi_ref[1]], axis=1).astype(bf16)
    bias_cat = jnp.concatenate([ba_ref[0:1], bi_ref[0:1], ba_ref[1:2], bi_ref[1:2]], axis=1)
    sp = jax.nn.softplus(-lam_ref[...])
    cw = cw_ref[...]
    cb = cb_ref[...]

    def gates_body(i, carry):
        r0 = pl.multiple_of(i * blk, blk)
        r_top = pl.multiple_of(jnp.maximum(r0 - SUBLANES, 0), SUBLANES)
        r_bot = pl.multiple_of(jnp.minimum(r0 + blk, seq - SUBLANES), SUBLANES)
        top = lx_ref[pl.ds(r_top, SUBLANES), :] * jnp.where(i > 0, 1.0, 0.0)
        bot = lx_ref[pl.ds(r_bot, SUBLANES), :] * jnp.where(i < nblk - 1, 1.0, 0.0)
        win = jnp.concatenate([top, lx_ref[pl.ds(r0, blk), :], bot], axis=0)
        xl = _conv_window(win, cw) + cb
        pre = _dot(xl.astype(bf16), w_cat) + bias_cat
        for d in range(2):
            r = _sigmoid(pre[:, (2 * d) * LRU_BW:(2 * d + 1) * LRU_BW])
            ig = _sigmoid(pre[:, (2 * d + 1) * LRU_BW:(2 * d + 2) * LRU_BW])
            a = jnp.exp(-RG_C * r * sp[d:d + 1])
            a_scr[d, pl.ds(r0, blk), :] = a
            b_scr[d, pl.ds(r0, blk), :] = jnp.sqrt(1.0 - a * a) * (ig * xl)
        return carry

    lax.fori_loop(0, nblk, gates_body, 0)

    row = lax.broadcasted_iota(jnp.int32, (CHUNK, LRU_BW), 0)
    nchunks = seq // CHUNK

    def chunk_scan(a, b, h_prev, reverse):
        s = 1
        while s < CHUNK:
            if reverse:
                keep = row < CHUNK - s
                a_sh = jnp.where(keep, pltpu.roll(a, CHUNK - s, 0), 1.0)
                b_sh = jnp.where(keep, pltpu.roll(b, CHUNK - s, 0), 0.0)
            else:
                keep = row >= s
                a_sh = jnp.where(keep, pltpu.roll(a, s, 0), 1.0)
                b_sh = jnp.where(keep, pltpu.roll(b, s, 0), 0.0)
            b = a * b_sh + b
            a = a * a_sh
            s *= 2
        return b + a * h_prev

    def scan_body(c, carry):
        hf, hb = carry
        rf = pl.multiple_of(c * CHUNK, CHUNK)
        rb = pl.multiple_of((nchunks - 1 - c) * CHUNK, CHUNK)
        h_f = chunk_scan(a_scr[0, pl.ds(rf, CHUNK), :], b_scr[0, pl.ds(rf, CHUNK), :], hf, False)
        h_b = chunk_scan(a_scr[1, pl.ds(rb, CHUNK), :], b_scr[1, pl.ds(rb, CHUNK), :], hb, True)
        b_scr[0, pl.ds(rf, CHUNK), :] = h_f
        b_scr[1, pl.ds(rb, CHUNK), :] = h_b
        return h_f[CHUNK - 1:CHUNK, :], h_b[0:1, :]

    if zero_init:
        init = (jnp.zeros((1, LRU_BW), f32), jnp.zeros((1, LRU_BW), f32))
    else:
        init = (h0_ref[0:1, :], h0_ref[1:2, :])
    hf, hb = lax.fori_loop(0, nchunks, scan_body, init)
    if emit_state:
        hfin_ref[0:1, :] = hf
        hfin_ref[1:2, :] = hb

    def out_body(i, carry):
        r0 = pl.multiple_of(i * blk, blk)
        h = b_scr[0, pl.ds(r0, blk), :] + b_scr[1, pl.ds(r0, blk), :]
        y_ref[pl.ds(r0, blk), :] = (h * jax.nn.gelu(lg_ref[pl.ds(r0, blk), :])).astype(y_ref.dtype)
        return carry

    lax.fori_loop(0, nblk, out_body, 0)


def _lru(path, proj, lp, h0, layer, emit_state):
    seq = path.seq_len
    proj3 = proj.reshape(path.nseq, seq, P_COLS)
    bx, bg = P_LX // LRU_BW, P_LG // LRU_BW
    in_specs = [
        pl.BlockSpec((None, seq, LRU_BW), lambda s, b: (s, 0, bx + b)),
        pl.BlockSpec((None, seq, LRU_BW), lambda s, b: (s, 0, bg + b)),
        pl.BlockSpec((CONV_W, LRU_BW), lambda s, b: (0, b)),
        pl.BlockSpec((1, LRU_BW), lambda s, b: (0, b)),
        pl.BlockSpec((2, None, LRU_BW, LRU_BW), lambda s, b: (0, b, 0, 0)),
        pl.BlockSpec((2, None, LRU_BW, LRU_BW), lambda s, b: (0, b, 0, 0)),
        pl.BlockSpec((2, LRU_BW), lambda s, b: (0, b)),
        pl.BlockSpec((2, LRU_BW), lambda s, b: (0, b)),
        pl.BlockSpec((2, LRU_BW), lambda s, b: (0, b)),
    ]
    args = [proj3, proj3, lp['lru_conv_w'], lp['lru_conv_b'].reshape(1, LRU_W), lp['lru_wa'], lp['lru_wi'],
            lp['lru_ba'], lp['lru_bi'], lp['lru_lambda']]
    if h0 is not None:
        in_specs.append(pl.BlockSpec((None, None, 2, LRU_BW), lambda s, b: (s, layer, 0, b)))
        args.append(h0)
    out_specs = [pl.BlockSpec((None, seq, LRU_BW), lambda s, b: (s, 0, b))]
    out_shape = [jax.ShapeDtypeStruct((path.nseq, seq, LRU_W), bf16)]
    if emit_state:
        out_specs.append(pl.BlockSpec((None, 2, LRU_BW), lambda s, b: (s, 0, b)))
        out_shape.append(jax.ShapeDtypeStruct((path.nseq, 2, LRU_W), f32))
    kern = functools.partial(_lru_kernel, zero_init=h0 is None, emit_state=emit_state)
    outs = pl.pallas_call(
        kern, grid=(path.nseq, LRU_BLOCKS), in_specs=in_specs, out_specs=out_specs, out_shape=out_shape,
        scratch_shapes=[pltpu.VMEM((2, seq, LRU_BW), f32), pltpu.VMEM((2, seq, LRU_BW), f32)],
        compiler_params=_cparams(("arbitrary", "arbitrary")), name="rglru")(*args)
    y = outs[0].reshape(path.tokens, LRU_W)
    return (y, outs[1]) if emit_state else (y, None)


def _pad_cols(w, n):
    return jnp.pad(w, ((0, 0),) * (w.ndim - 1) + ((0, n - w.shape[-1]),))


def _pos_table():
    quarter = D_MODEL // 4
    freqs = jnp.exp(-math.log(10000.0) * jnp.arange(quarter, dtype=f32) / quarter)
    e = jnp.arange(GRID_W, dtype=f32)[:, None] * freqs
    return jnp.concatenate([jnp.sin(e), jnp.cos(e)], axis=-1)


def _layer(path, x, h, layer, rows, mod3, lp, ffn, s0_gdn, h0_lru, emit_state, next_norm_g):
    proj = _matmul(h, lp['w_in'], 1024, 896, f32)
    qkv = _qkv_prep(path, proj, lp['gdn_conv_w'])
    outs_f = _gdn_scan(path, qkv, proj, lp['gdn_prm'], s0_gdn, layer, False, emit_state)
    outs_b = _gdn_scan(path, qkv, proj, lp['gdn_prm'], s0_gdn, layer, True, emit_state)
    o_mix = _gdn_combine(outs_f[0], outs_b[0], proj, lp['gdn_norm_g'])
    y_lru, h_fin = _lru(path, proj, lp, h0_lru, layer, emit_state)
    h_next = None
    if ffn['kind'] == 'dense':
        x, h2 = _out_proj(path, o_mix, y_lru, lp['w_out'], x, mod3, layer, rows, lp['norm2_g'])
        act = _gate_up(h2, ffn['w1'], ffn['w3'], 1024, 512)
        outs = _down_proj(path, act, ffn['w2'], x, mod3, layer, rows, 512, ffn['w2'].shape[0] // 4, next_norm_g)
        x = outs[0]
        h_next = outs[1] if next_norm_g is not None else None
    else:
        x, h2, route = _out_proj(path, o_mix, y_lru, lp['w_out'], x, mod3, layer, rows, lp['norm2_g'],
                                 router=(ffn['wr'], ffn['br']), h_dtype=f32)
        tok_of_row, gate_of_row, expert_of_tile, n_used, pos = _moe_route(route, path.tokens)
        h_rows = _row_gather(h2, tok_of_row, n_used)
        act = _moe_gate_up(h_rows, ffn['w1'], ffn['w3'], gate_of_row, expert_of_tile, n_used,
                           ffn['w1'].shape[2] // 2)
        y_rows = _moe_down(act, ffn['w2'], expert_of_tile, n_used)
        x = _moe_combine(path, y_rows, pos, x, mod3, layer, rows, final_g=ffn.get('final_g'))
    s_fin = jnp.stack([outs_f[1], outs_b[1]], axis=1) if emit_state else None
    return x, h_next, s_fin, h_fin


def kernel(x_prompt, x_sample, state_gdn, state_lru, c, c_ctx, w_mod, b_mod, norm1_g, norm2_g, w_in, gdn_conv_w, gdn_A_log, gdn_dt_bias, gdn_norm_g, lru_conv_w, lru_conv_b, lru_wa, lru_ba, lru_wi, lru_bi, lru_lambda, w_out, ffd_w1, ffd_w3, ffd_w2, moe_wr, moe_br, moe_w1, moe_w3, moe_w2, final_g):
    depth = w_mod.shape[0]
    bp, sp_len, d = x_prompt.shape
    bs, ss_len, _ = x_sample.shape
    ctx = _Path(bp, sp_len, 0, False)
    lat = _Path(bs, ss_len, 1, True)

    rows = SUBLANES * ((1 + bs + SUBLANES - 1) // SUBLANES)
    cvec = jnp.zeros((rows, d), f32).at[0].set(c_ctx).at[1:1 + bs].set(c)
    mod3 = _modulation(cvec, w_mod, b_mod).reshape(depth * rows * 6, 1, d)

    pos = _pos_table()
    xp = x_prompt.reshape(ctx.tokens, d)
    xs = x_sample.reshape(lat.tokens, d)
    new_gdn, new_lru = [], []
    hp = hs = None
    for l in range(depth):
        w_in_l = w_in[l].astype(bf16)
        n_ab = 4 * N_HEADS
        w_in_p = jnp.concatenate([w_in_l[:, :P_AB + n_ab], jnp.zeros((d, LANES - n_ab), bf16),
                                  w_in_l[:, P_AB + n_ab:]], axis=1)
        prm = jnp.stack([_pad_cols(gdn_A_log[l].reshape(1, -1), LANES)[0],
                         _pad_cols(gdn_dt_bias[l].reshape(1, -1), LANES)[0]])
        lp = {
            'norm1_g': norm1_g[l], 'norm2_g': norm2_g[l], 'w_in': w_in_p, 'gdn_conv_w': gdn_conv_w[l],
            'gdn_prm': prm, 'gdn_norm_g': gdn_norm_g[l], 'lru_conv_w': lru_conv_w[l], 'lru_conv_b': lru_conv_b[l],
            'lru_wa': lru_wa[l], 'lru_ba': lru_ba[l], 'lru_wi': lru_wi[l], 'lru_bi': lru_bi[l],
            'lru_lambda': lru_lambda[l], 'w_out': w_out[l].astype(bf16),
        }
        j = l // 2
        if l % 2 == 0:
            f = ffd_w1.shape[2]
            fp = 512 * ((f + 511) // 512)
            ffn = {'kind': 'dense', 'w1': _pad_cols(ffd_w1[j].astype(bf16), fp),
                   'w3': _pad_cols(ffd_w3[j].astype(bf16), fp),
                   'w2': jnp.pad(ffd_w2[j].astype(bf16), ((0, fp - f), (0, 0)))}
        else:
            ffn = {'kind': 'moe', 'wr': _pad_cols(moe_wr[j], LANES), 'br': _pad_cols(moe_br[j].reshape(1, -1), LANES),
                   'w1': moe_w1[j].astype(bf16), 'w3': moe_w3[j].astype(bf16), 'w2': moe_w2[j].astype(bf16)}
            if l == depth - 1:
                ffn['final_g'] = final_g
        next_g = norm1_g[l + 1] if l + 1 < depth else None
        if hp is None:
            (hp,) = _norm_mod(ctx, xp, lp['norm1_g'], mod3, l, rows, 1, 0)
        if hs is None and l == 0:
            hs, xs = _norm_mod(lat, xs, lp['norm1_g'], mod3, l, rows, 1, 0, pos=pos)
        elif hs is None:
            (hs,) = _norm_mod(lat, xs, lp['norm1_g'], mod3, l, rows, 1, 0)
        xp, hp, s_g, s_l = _layer(ctx, xp, hp, l, rows, mod3, lp, ffn, None, None, True, next_g)
        new_gdn.append(s_g)
        new_lru.append(s_l)
        xs, hs, _, _ = _layer(lat, xs, hs, l, rows, mod3, lp, ffn, state_gdn, state_lru, False, next_g)
    if depth % 2 == 1:
        xp, xs = _final_norm(xp, final_g), _final_norm(xs, final_g)
    y_prompt = xp.reshape(bp, sp_len, d)
    y_sample = xs.reshape(bs, ss_len, d)
    return (y_prompt, y_sample, jnp.stack(new_gdn, axis=1), jnp.stack(new_lru, axis=1))
```

```python
import functools
import math

import jax
import jax.numpy as jnp
from jax import lax
from jax.experimental import pallas as pl
from jax.experimental.pallas import tpu as pltpu

f32 = jnp.float32
bf16 = jnp.bfloat16

D_MODEL = 2048
GRID_W = 64
N_HEADS = 8
HEAD_D = 128
GDN_W = N_HEADS * HEAD_D
LRU_W = 1024
LRU_BLOCKS = 8
LRU_BW = LRU_W // LRU_BLOCKS
CONV_W = 4
CONV_LEFT = 2
CHUNK = 64
RG_C = 8.0
N_EXPERTS = 8
EPS = 1e-6
LANES = 128
SUBLANES = 8

P_Q = 0
P_K = P_Q + GDN_W
P_V = P_K + GDN_W
P_Z = P_V + GDN_W
P_AB = P_Z + GDN_W
P_LX = P_AB + LANES
P_LG = P_LX + LRU_W
P_COLS = P_LG + LRU_W

SEQ_TILE = 256
VMEM_LIMIT = 48 * 1024 * 1024


def _cparams(sem):
    return pltpu.CompilerParams(dimension_semantics=sem, vmem_limit_bytes=VMEM_LIMIT)


def _dot(a, b):
    return jnp.dot(a, b, preferred_element_type=f32)


def _sigmoid(x):
    return 0.5 * jnp.tanh(0.5 * x) + 0.5


def _silu(x):
    return x * _sigmoid(x)


def _dot_nt(a, b, precision=None):
    return lax.dot_general(a, b, (((1,), (1,)), ((), ())), precision=precision, preferred_element_type=f32)


def _dot_tn(a, b):
    return lax.dot_general(a, b, (((0,), (0,)), ((), ())), preferred_element_type=f32)


def _mod_kernel(c_ref, w_ref, b_ref, o_ref):
    a = _silu(c_ref[...]).astype(bf16)
    o_ref[...] = _dot(a, w_ref[...].astype(bf16)) + b_ref[...]


def _modulation(cvec, w_mod, b_mod):
    depth, d, n = w_mod.shape
    tn = 1024
    return pl.pallas_call(
        _mod_kernel,
        grid=(depth, n // tn),
        in_specs=[
            pl.BlockSpec((cvec.shape[0], d), lambda l, j: (0, 0)),
            pl.BlockSpec((None, d, tn), lambda l, j: (l, 0, j)),
            pl.BlockSpec((None, 1, tn), lambda l, j: (l, 0, j)),
        ],
        out_specs=pl.BlockSpec((None, cvec.shape[0], tn), lambda l, j: (l, 0, j)),
        out_shape=jax.ShapeDtypeStruct((depth, cvec.shape[0], n), f32),
        compiler_params=_cparams(("arbitrary", "arbitrary")),
        name="modulation",
    )(cvec, w_mod, b_mod.reshape(depth, 1, n))


class _Path:
    def __init__(self, nseq, seq_len, mod_row0, per_seq_mod):
        self.nseq = nseq
        self.seq_len = seq_len
        self.tokens = nseq * seq_len
        self.mod_row0 = mod_row0
        self.per_seq_mod = per_seq_mod

    def mod_row(self, tile_idx, tile):
        if self.per_seq_mod:
            return self.mod_row0 + tile_idx // (self.seq_len // tile)
        return self.mod_row0


def _mod_spec(path, layer, rows, part, tile, width, col_of=None, grid_rank=1):
    def imap(*idx):
        i = idx[0]
        j = idx[col_of] if col_of is not None else 0
        return ((layer * rows + path.mod_row(i, tile)) * 6 + part, 0, j)
    return pl.BlockSpec((None, 1, width), imap)


def _rms(x, g):
    return x * lax.rsqrt(jnp.mean(x * x, axis=-1, keepdims=True) + EPS) * g


R_IDX, R_GATE = 0, 2


def _emit_modulated_norm(x, g_ref, scale_ref, shift_ref, h_ref, wr_ref=None, br_ref=None, route_ref=None):
    h = _rms(x, g_ref[...]) * (1.0 + scale_ref[...]) + shift_ref[...]
    h_ref[...] = h.astype(h_ref.dtype)
    if route_ref is None:
        return
    logits = _dot(h.astype(bf16), wr_ref[...].astype(bf16)) + br_ref[...]
    lane = lax.broadcasted_iota(jnp.int32, logits.shape, 1)
    valid = lane < N_EXPERTS
    logits = jnp.where(valid, logits, -jnp.inf)
    e = jnp.exp(logits - jnp.max(logits, axis=-1, keepdims=True))
    p = e / jnp.sum(e, axis=-1, keepdims=True)
    p = jnp.where(valid, p, -1.0)
    m1 = jnp.max(p, axis=-1, keepdims=True)
    i1 = jnp.min(jnp.where(p == m1, lane, LANES), axis=-1, keepdims=True)
    p2 = jnp.where(lane == i1, -1.0, p)
    m2 = jnp.max(p2, axis=-1, keepdims=True)
    i2 = jnp.min(jnp.where(p2 == m2, lane, LANES), axis=-1, keepdims=True)
    den = m1 + m2
    route_ref[...] = (jnp.where(lane == R_IDX, i1.astype(f32), 0.0) + jnp.where(lane == R_IDX + 1, i2.astype(f32), 0.0)
                      + jnp.where(lane == R_GATE, m1 / den, 0.0) + jnp.where(lane == R_GATE + 1, m2 / den, 0.0))


def _norm_io(path, g, mod3, layer, rows, part_scale, part_shift, tile, router, h_dtype):
    d = D_MODEL
    in_specs = [pl.BlockSpec((1, d), lambda *idx: (0, 0)),
                _mod_spec(path, layer, rows, part_scale, tile, d),
                _mod_spec(path, layer, rows, part_shift, tile, d)]
    args = [g.reshape(1, d), mod3, mod3]
    out_specs = [pl.BlockSpec((tile, d), lambda *idx: (idx[0], 0))]
    out_shape = [jax.ShapeDtypeStruct((path.tokens, d), h_dtype)]
    if router is not None:
        in_specs += [pl.BlockSpec((d, LANES), lambda *idx: (0, 0)), pl.BlockSpec((1, LANES), lambda *idx: (0, 0))]
        args += list(router)
        out_specs.append(pl.BlockSpec((tile, LANES), lambda *idx: (idx[0], 0)))
        out_shape.append(jax.ShapeDtypeStruct((path.tokens, LANES), f32))
    return in_specs, args, out_specs, out_shape


def _norm_mod_kernel(*refs, with_pos, with_router, tile, tiles_per_seq):
    it = iter(refs)
    x_ref, g_ref, scale_ref, shift_ref = next(it), next(it), next(it), next(it)
    pos_ref = next(it) if with_pos else None
    wr_ref, br_ref = (next(it), next(it)) if with_router else (None, None)
    h_ref = next(it)
    xo_ref = next(it) if with_pos else None
    gates_ref = next(it) if with_router else None

    x = x_ref[...]
    if with_pos:
        rows_per_tile = tile // GRID_W
        row0 = (pl.program_id(0) % tiles_per_seq) * rows_per_tile
        half = D_MODEL // 2
        col_part = pos_ref[...]
        pieces = []
        for r in range(rows_per_tile):
            row_part = jnp.broadcast_to(pos_ref[pl.ds(row0 + r, 1), :], (GRID_W, half))
            pieces.append(jnp.concatenate([row_part, col_part], axis=1))
        x = x + jnp.concatenate(pieces, axis=0)
        xo_ref[...] = x
    _emit_modulated_norm(x, g_ref, scale_ref, shift_ref, h_ref, wr_ref, br_ref, gates_ref)


def _norm_mod(path, x, g, mod3, layer, rows, part_scale, part_shift, pos=None, router=None, h_dtype=bf16):
    tile = SEQ_TILE
    n = path.tokens // tile
    d = D_MODEL
    in_specs = [
        pl.BlockSpec((tile, d), lambda i: (i, 0)),
        pl.BlockSpec((1, d), lambda i: (0, 0)),
        _mod_spec(path, layer, rows, part_scale, tile, d),
        _mod_spec(path, layer, rows, part_shift, tile, d),
    ]
    args = [x, g.reshape(1, d), mod3, mod3]
    out_specs = [pl.BlockSpec((tile, d), lambda i: (i, 0))]
    out_shape = [jax.ShapeDtypeStruct((path.tokens, d), h_dtype)]
    if pos is not None:
        in_specs.append(pl.BlockSpec(pos.shape, lambda i: (0, 0)))
        args.append(pos)
        out_specs.append(pl.BlockSpec((tile, d), lambda i: (i, 0)))
        out_shape.append(jax.ShapeDtypeStruct((path.tokens, d), f32))
    if router is not None:
        wr, br = router
        in_specs += [pl.BlockSpec((d, LANES), lambda i: (0, 0)), pl.BlockSpec((1, LANES), lambda i: (0, 0))]
        args += [wr, br]
        out_specs.append(pl.BlockSpec((tile, LANES), lambda i: (i, 0)))
        out_shape.append(jax.ShapeDtypeStruct((path.tokens, LANES), f32))
    kern = functools.partial(_norm_mod_kernel, with_pos=pos is not None, with_router=router is not None,
                             tile=tile, tiles_per_seq=path.seq_len // tile)
    return pl.pallas_call(kern, grid=(n,), in_specs=in_specs, out_specs=out_specs, out_shape=out_shape,
                          compiler_params=_cparams(("arbitrary",)), name="norm_mod")(*args)


def _final_norm_kernel(x_ref, g_ref, o_ref):
    o_ref[...] = _rms(x_ref[...], g_ref[...])


def _final_norm(x, g):
    t, d = x.shape
    tile = 512
    return pl.pallas_call(
        _final_norm_kernel, grid=(t // tile,),
        in_specs=[pl.BlockSpec((tile, d), lambda i: (i, 0)), pl.BlockSpec((1, d), lambda i: (0, 0))],
        out_specs=pl.BlockSpec((tile, d), lambda i: (i, 0)),
        out_shape=jax.ShapeDtypeStruct((t, d), f32),
        compiler_params=_cparams(("arbitrary",)), name="final_norm")(x, g.reshape(1, d))


def _mm_kernel(a_ref, w_ref, o_ref):
    o_ref[...] = _dot(a_ref[...], w_ref[...]).astype(o_ref.dtype)


def _matmul(a, w, tm, tn, out_dtype):
    m, k = a.shape
    n = w.shape[1]
    return pl.pallas_call(
        _mm_kernel, grid=(m // tm, n // tn),
        in_specs=[pl.BlockSpec((tm, k), lambda i, j: (i, 0)), pl.BlockSpec((k, tn), lambda i, j: (0, j))],
        out_specs=pl.BlockSpec((tm, tn), lambda i, j: (i, j)),
        out_shape=jax.ShapeDtypeStruct((m, n), out_dtype),
        compiler_params=_cparams(("arbitrary", "arbitrary")), name="matmul")(a, w)


def _out_proj_kernel(*refs, with_router):
    a1_ref, a2_ref, w1_ref, w2_ref, x_ref, gate_ref, g_ref, scale_ref, shift_ref = refs[:9]
    wr_ref, br_ref = refs[9:11] if with_router else (None, None)
    outs = refs[11:] if with_router else refs[9:]
    xo_ref, h_ref = outs[:2]
    route_ref = outs[2] if with_router else None
    acc = _dot(a1_ref[...], w1_ref[...]) + _dot(a2_ref[...], w2_ref[...])
    x = x_ref[...] + gate_ref[...] * acc
    xo_ref[...] = x
    _emit_modulated_norm(x, g_ref, scale_ref, shift_ref, h_ref, wr_ref, br_ref, route_ref)


def _out_proj(path, a1, a2, w, x, mod3, layer, rows, norm_g, router=None, h_dtype=bf16):
    tm = 512
    m, d = x.shape
    k1, k2 = a1.shape[1], a2.shape[1]
    n_in, n_args, n_out, n_shape = _norm_io(path, norm_g, mod3, layer, rows, 4, 3, tm, router, h_dtype)
    once = pl.Buffered(1)
    return pl.pallas_call(
        functools.partial(_out_proj_kernel, with_router=router is not None), grid=(m // tm,),
        in_specs=[
            pl.BlockSpec((tm, k1), lambda i: (i, 0)),
            pl.BlockSpec((tm, k2), lambda i: (i, 0)),
            pl.BlockSpec((k1, d), lambda i: (0, 0), pipeline_mode=once),
            pl.BlockSpec((k2, d), lambda i: (k1 // k2, 0), pipeline_mode=once),
            pl.BlockSpec((tm, d), lambda i: (i, 0)),
            _mod_spec(path, layer, rows, 2, tm, d),
        ] + n_in,
        out_specs=[pl.BlockSpec((tm, d), lambda i: (i, 0))] + n_out,
        out_shape=[jax.ShapeDtypeStruct((m, d), f32)] + n_shape,
        compiler_params=_cparams(("arbitrary",)), name="out_proj")(a1, a2, w, w, x, mod3, *n_args)


def _gate_up_kernel(h_ref, w1_ref, w3_ref, o_ref):
    h = h_ref[...]
    a = _dot(h, w1_ref[...])
    b = _dot(h, w3_ref[...])
    o_ref[...] = (_silu(a) * b).astype(o_ref.dtype)


def _gate_up(h, w1, w3, tm, tn):
    m, k = h.shape
    n = w1.shape[1]
    return pl.pallas_call(
        _gate_up_kernel, grid=(m // tm, n // tn),
        in_specs=[pl.BlockSpec((tm, k), lambda i, j: (i, 0)),
                  pl.BlockSpec((k, tn), lambda i, j: (0, j)),
                  pl.BlockSpec((k, tn), lambda i, j: (0, j))],
        out_specs=pl.BlockSpec((tm, tn), lambda i, j: (i, j)),
        out_shape=jax.ShapeDtypeStruct((m, n), bf16),
        compiler_params=_cparams(("arbitrary", "arbitrary")), name="gate_up")(h, w1, w3)


MOE_TILE = 256
DMA_UNROLL = 8
GATHER_ROWS = 4 * MOE_TILE


def _moe_route(route, n_tok):
    tm = MOE_TILE
    n_rows = 2 * n_tok + N_EXPERTS * tm
    e_idx = jnp.clip(route[:, R_IDX:R_IDX + 2].astype(jnp.int32), 0, N_EXPERTS - 1).reshape(-1)
    gate = route[:, R_GATE:R_GATE + 2].reshape(-1)
    one_hot = (e_idx[:, None] == jnp.arange(N_EXPERTS, dtype=jnp.int32)[None, :]).astype(jnp.int32)
    csum = jnp.cumsum(one_hot, axis=0)
    rank = jnp.take_along_axis(csum, e_idx[:, None], axis=1)[:, 0] - 1
    padded = ((csum[-1] + tm - 1) // tm) * tm
    ends = jnp.cumsum(padded)
    pos = (ends - padded)[e_idx] + rank
    tok_of_row = jnp.zeros((n_rows,), jnp.int32).at[pos].set(jnp.arange(2 * n_tok, dtype=jnp.int32) // 2)
    gate_of_row = jnp.zeros((n_rows,), f32).at[pos].set(gate)
    tile_start = jnp.arange(n_rows // tm, dtype=jnp.int32) * tm
    expert_of_tile = jnp.minimum(jnp.sum(tile_start[:, None] >= ends[None, :], axis=1), N_EXPERTS - 1)
    n_used = (ends[-1:] // tm).astype(jnp.int32)
    return tok_of_row, gate_of_row.reshape(n_rows, 1), expert_of_tile.astype(jnp.int32), n_used, pos.astype(jnp.int32)


def _row_copy(src_hbm, src_row, dst_vmem, dst_row, sem):
    return pltpu.make_async_copy(src_hbm.at[pl.ds(src_row, 1), :], dst_vmem.at[pl.ds(dst_row, 1), :], sem)


def _row_gather_kernel(tok_ref, nused_ref, h_hbm, o_ref, sem):
    i = pl.program_id(0)
    tg = o_ref.shape[0]
    used = i * (tg // MOE_TILE) < nused_ref[0]

    @pl.when(used)
    def _():
        def issue(r, c):
            _row_copy(h_hbm, tok_ref[i * tg + r], o_ref, r, sem).start()
            return c
        lax.fori_loop(0, tg, issue, 0, unroll=DMA_UNROLL)
        pltpu.make_async_copy(h_hbm.at[pl.ds(0, tg), :], o_ref, sem).wait()

    @pl.when(jnp.logical_not(used))
    def _():
        o_ref[...] = jnp.zeros_like(o_ref)


def _row_gather(h, tok_of_row, n_used):
    n_rows = tok_of_row.shape[0]
    d = h.shape[1]
    assert n_rows % GATHER_ROWS == 0
    return pl.pallas_call(
        _row_gather_kernel,
        grid_spec=pltpu.PrefetchScalarGridSpec(
            num_scalar_prefetch=2, grid=(n_rows // GATHER_ROWS,),
            in_specs=[pl.BlockSpec(memory_space=pl.ANY)],
            out_specs=pl.BlockSpec((GATHER_ROWS, d), lambda i, tok, nu: (i, 0)),
            scratch_shapes=[pltpu.SemaphoreType.DMA]),
        out_shape=jax.ShapeDtypeStruct((n_rows, d), h.dtype),
        compiler_params=_cparams(("arbitrary",)), name="moe_row_gather")(tok_of_row, n_used, h)


def _moe_gate_up_kernel(eot_ref, nused_ref, x_ref, w1_ref, w3_ref, g_ref, o_ref):
    i = pl.program_id(1)

    @pl.when(i < nused_ref[0])
    def _():
        x = x_ref[...].astype(bf16)
        a = _dot(x, w1_ref[...])
        b = _dot(x, w3_ref[...])
        o_ref[...] = (g_ref[...] * (_silu(a) * b)).astype(o_ref.dtype)

    @pl.when(i >= nused_ref[0])
    def _():
        o_ref[...] = jnp.zeros_like(o_ref)


def _moe_gate_up(rows, w1, w3, gate_of_row, expert_of_tile, n_used, tn):
    tm = MOE_TILE
    n_rows, k = rows.shape
    f = w1.shape[2]
    return pl.pallas_call(
        _moe_gate_up_kernel,
        grid_spec=pltpu.PrefetchScalarGridSpec(
            num_scalar_prefetch=2, grid=(f // tn, n_rows // tm),
            in_specs=[pl.BlockSpec((tm, k), lambda j, i, eot, nu: (i, 0)),
                      pl.BlockSpec((None, k, tn), lambda j, i, eot, nu: (eot[i], 0, j)),
                      pl.BlockSpec((None, k, tn), lambda j, i, eot, nu: (eot[i], 0, j)),
                      pl.BlockSpec((tm, 1), lambda j, i, eot, nu: (i, 0))],
            out_specs=pl.BlockSpec((tm, tn), lambda j, i, eot, nu: (i, j))),
        out_shape=jax.ShapeDtypeStruct((n_rows, f), bf16),
        compiler_params=_cparams(("arbitrary", "arbitrary")), name="moe_gate_up")(
            expert_of_tile, n_used, rows, w1, w3, gate_of_row)


def _moe_down_kernel(eot_ref, nused_ref, a_ref, w_ref, o_ref):
    i = pl.program_id(0)

    @pl.when(i < nused_ref[0])
    def _():
        o_ref[...] = _dot(a_ref[...], w_ref[...])

    @pl.when(i >= nused_ref[0])
    def _():
        o_ref[...] = jnp.zeros_like(o_ref)


def _moe_down(act, w2, expert_of_tile, n_used):
    tm = MOE_TILE
    n_rows, f = act.shape
    d = w2.shape[2]
    return pl.pallas_call(
        _moe_down_kernel,
        grid_spec=pltpu.PrefetchScalarGridSpec(
            num_scalar_prefetch=2, grid=(n_rows // tm,),
            in_specs=[pl.BlockSpec((tm, f), lambda i, eot, nu: (i, 0)),
                      pl.BlockSpec((None, f, d), lambda i, eot, nu: (eot[i], 0, 0))],
            out_specs=pl.BlockSpec((tm, d), lambda i, eot, nu: (i, 0))),
        out_shape=jax.ShapeDtypeStruct((n_rows, d), f32),
        compiler_params=_cparams(("arbitrary",)), name="moe_down")(expert_of_tile, n_used, act, w2)


def _moe_combine_kernel(pos_ref, y_hbm, x_ref, gate_ref, *rest):
    fg_ref = rest[0] if len(rest) == 4 else None
    o_ref, buf, sem = rest[-3:]
    i = pl.program_id(0)
    tm = o_ref.shape[0]

    def issue_tile(t, slot):
        def issue(r, c):
            a = 2 * (t * tm + r)
            _row_copy(y_hbm, pos_ref[a], buf.at[slot, 0], r, sem.at[slot]).start()
            _row_copy(y_hbm, pos_ref[a + 1], buf.at[slot, 1], r, sem.at[slot]).start()
            return c
        lax.fori_loop(0, tm, issue, 0, unroll=DMA_UNROLL)

    @pl.when(i == 0)
    def _():
        issue_tile(0, 0)

    @pl.when(i + 1 < pl.num_programs(0))
    def _():
        issue_tile(i + 1, (i + 1) % 2)

    slot = i % 2
    for b in range(2):
        pltpu.make_async_copy(y_hbm.at[pl.ds(0, tm), :], buf.at[slot, b], sem.at[slot]).wait()
    y = x_ref[...] + gate_ref[...] * (buf[slot, 0] + buf[slot, 1])
    o_ref[...] = y if fg_ref is None else _rms(y, fg_ref[...])


def _moe_combine(path, y_rows, pos, x, mod3, layer, rows, final_g=None):
    tm = SEQ_TILE
    m, d = x.shape
    in_specs = [pl.BlockSpec(memory_space=pl.ANY),
                pl.BlockSpec((tm, d), lambda i, pos: (i, 0)),
                pl.BlockSpec((None, 1, d), lambda i, pos: ((layer * rows + path.mod_row(i, tm)) * 6 + 5, 0, 0))]
    args = [pos, y_rows, x, mod3]
    if final_g is not None:
        in_specs.append(pl.BlockSpec((1, d), lambda i, pos: (0, 0)))
        args.append(final_g.reshape(1, d))
    return pl.pallas_call(
        _moe_combine_kernel,
        grid_spec=pltpu.PrefetchScalarGridSpec(
            num_scalar_prefetch=1, grid=(m // tm,), in_specs=in_specs,
            out_specs=pl.BlockSpec((tm, d), lambda i, pos: (i, 0)),
            scratch_shapes=[pltpu.VMEM((2, 2, tm, d), f32), pltpu.SemaphoreType.DMA((2,))]),
        out_shape=jax.ShapeDtypeStruct((m, d), f32),
        compiler_params=_cparams(("arbitrary",)), name="moe_combine")(*args)


def _down_kernel(*refs, nk, with_norm):
    a_ref, w_ref, x_ref, gate_ref = refs[:4]
    g_ref, scale_ref, shift_ref = refs[4:7] if with_norm else (None, None, None)
    outs = refs[7:] if with_norm else refs[4:]
    o_ref = outs[0]
    h_ref = outs[1] if with_norm else None
    acc_ref = outs[-1]
    kk = pl.program_id(1)

    @pl.when(kk == 0)
    def _():
        acc_ref[...] = jnp.zeros_like(acc_ref)

    acc_ref[...] += _dot(a_ref[...], w_ref[...])

    @pl.when(kk == nk - 1)
    def _():
        x = x_ref[...] + gate_ref[...] * acc_ref[...]
        o_ref[...] = x
        if with_norm:
            _emit_modulated_norm(x, g_ref, scale_ref, shift_ref, h_ref)


def _down_proj(path, act, w2, x, mod3, layer, rows, tm, tk, next_norm_g=None):
    m, d = x.shape
    kdim = act.shape[1]
    nk = kdim // tk
    n_in, n_args, n_out, n_shape = ([], [], [], [])
    if next_norm_g is not None:
        n_in, n_args, n_out, n_shape = _norm_io(path, next_norm_g, mod3, layer + 1, rows, 1, 0, tm, None, bf16)
    return pl.pallas_call(
        functools.partial(_down_kernel, nk=nk, with_norm=next_norm_g is not None), grid=(m // tm, nk),
        in_specs=[pl.BlockSpec((tm, tk), lambda i, k: (i, k)),
                  pl.BlockSpec((tk, d), lambda i, k: (k, 0)),
                  pl.BlockSpec((tm, d), lambda i, k: (i, 0)),
                  _mod_spec(path, layer, rows, 5, tm, d)] + n_in,
        out_specs=[pl.BlockSpec((tm, d), lambda i, k: (i, 0))] + n_out,
        out_shape=[jax.ShapeDtypeStruct((m, d), f32)] + n_shape,
        scratch_shapes=[pltpu.VMEM((tm, d), f32)],
        compiler_params=_cparams(("arbitrary", "arbitrary")), name="down_proj")(act, w2, x, mod3, *n_args)


def _conv_window(win, w):
    n = win.shape[0] - 2 * SUBLANES
    acc = None
    for j in range(CONV_W):
        off = SUBLANES + j - CONV_LEFT
        term = win[off:off + n] * w[j:j + 1]
        acc = term if acc is None else acc + term
    return acc


def _qkv_prep_kernel(prev_ref, cur_ref, next_ref, w_ref, o_ref, *, tiles_per_seq):
    t = pl.program_id(0) % tiles_per_seq
    kind = pl.program_id(1)
    has_prev = (t > 0).astype(f32)
    has_next = (t < tiles_per_seq - 1).astype(f32)
    scale = jnp.where(kind == 0, HEAD_D ** -0.5, 1.0).astype(f32)
    tile = cur_ref.shape[0]
    for c in range(tile // CHUNK):
        r0 = c * CHUNK
        for hd in range(N_HEADS):
            cols = slice(hd * HEAD_D, (hd + 1) * HEAD_D)
            top = prev_ref[:, cols] * has_prev if c == 0 else cur_ref[r0 - SUBLANES:r0, cols]
            bot = (next_ref[:, cols] * has_next if r0 + CHUNK == tile
                   else cur_ref[r0 + CHUNK:r0 + CHUNK + SUBLANES, cols])
            win = jnp.concatenate([top, cur_ref[r0:r0 + CHUNK, cols], bot], axis=0)
            y = _silu(_conv_window(win, w_ref[:, cols]))
            nrm = lax.rsqrt(jnp.sum(y * y, axis=-1, keepdims=True) + EPS) * scale
            y = y * jnp.where(kind < 2, nrm, 1.0)
            o_ref[r0:r0 + CHUNK, cols] = y


def _qkv_prep(path, proj, conv_w):
    tile = SEQ_TILE
    n = path.tokens // tile
    tps = path.seq_len // tile
    rb = tile // SUBLANES
    last = path.tokens // SUBLANES - 1
    return pl.pallas_call(
        functools.partial(_qkv_prep_kernel, tiles_per_seq=tps), grid=(n, 3),
        in_specs=[
            pl.BlockSpec((SUBLANES, GDN_W), lambda i, j: (jnp.maximum(i * rb - 1, 0), j)),
            pl.BlockSpec((tile, GDN_W), lambda i, j: (i, j)),
            pl.BlockSpec((SUBLANES, GDN_W), lambda i, j: (jnp.minimum((i + 1) * rb, last), j)),
            pl.BlockSpec((CONV_W, GDN_W), lambda i, j: (0, j)),
        ],
        out_specs=pl.BlockSpec((tile, GDN_W), lambda i, j: (i, j)),
        out_shape=jax.ShapeDtypeStruct((path.tokens, 3 * GDN_W), f32),
        compiler_params=_cparams(("arbitrary", "arbitrary")), name="qkv_prep")(proj, proj, proj, conv_w)


INV_BLOCK = 16


def _bdot(a, b):
    return _dot(a.astype(bf16), b.astype(bf16))


def _unit_tri_inverse(a_list, eye, diag_blk):
    p = [-jnp.where(diag_blk, a, 0.0) for a in a_list]
    x = [eye + pi for pi in p]
    span = 2
    while span < INV_BLOCK:
        p = [_bdot(pi, pi) for pi in p]
        x = [xi + _bdot(xi, pi) for xi, pi in zip(x, p)]
        span *= 2
    ia = [(eye + a).astype(bf16) for a in a_list]
    span = INV_BLOCK
    while span < a_list[0].shape[0]:
        r = [eye - _dot(iai, xi.astype(bf16)) for iai, xi in zip(ia, x)]
        x = [xi + _bdot(xi, ri) for xi, ri in zip(x, r)]
        span *= 2
    return x


def _gdn_kernel(*refs, reverse, tiles_per_seq, zero_init, emit_state):
    it = iter(refs)
    q_ref, k_ref, v_ref, ab_ref, prm_ref = next(it), next(it), next(it), next(it), next(it)
    s0_ref = None if zero_init else next(it)
    o_ref = next(it)
    sfin_ref = next(it) if emit_state else None
    s_scr = next(it)

    t = pl.program_id(1)
    nchunks = q_ref.shape[0] // CHUNK

    @pl.when(t == 0)
    def _():
        if zero_init:
            s_scr[...] = jnp.zeros_like(s_scr)
        else:
            s_scr[...] = s0_ref[...]

    ri = lax.broadcasted_iota(jnp.int32, (CHUNK, CHUNK), 0)
    ci = lax.broadcasted_iota(jnp.int32, (CHUNK, CHUNK), 1)
    incl = (ri <= ci) if reverse else (ri >= ci)
    strict = (ri < ci) if reverse else (ri > ci)
    eye = (ri == ci).astype(f32)
    diag_blk = (ri // INV_BLOCK) == (ci // INV_BLOCK)
    tri = incl.astype(f32)
    last_row = 0 if reverse else CHUNK - 1
    dir_off = N_HEADS if reverse else 0
    neg_a = -jnp.exp(prm_ref[0:1, :])
    dt_bias = prm_ref[1:2, :]

    heads = range(N_HEADS)
    cols = [slice(hd * HEAD_D, (hd + 1) * HEAD_D) for hd in heads]
    ga = [dir_off + hd for hd in heads]

    def prepare(chunks):
        gate = []
        for cc in chunks:
            rows = slice(cc * CHUNK, (cc + 1) * CHUNK)
            ab = ab_ref[rows, :]
            g = neg_a * jax.nn.softplus(ab + dt_bias)
            beta = _sigmoid(ab)
            gc = jnp.dot(tri, g, precision=lax.Precision.HIGHEST, preferred_element_type=f32)
            g_last = gc[last_row:last_row + 1, :]
            gate.append({'beta': beta, 'gc': gc, 'gc_t': gc.T, 'e_gc': jnp.exp(gc),
                         'e_rest': jnp.exp(g_last - gc), 'e_last': jnp.exp(g_last)})
        probs = [(i, h) for i in range(len(chunks)) for h in heads]

        def rows_of(i):
            return slice(chunks[i] * CHUNK, (chunks[i] + 1) * CHUNK)

        def col(i, name, lane):
            return gate[i][name][:, lane:lane + 1]

        b_col = [col(i, 'beta', 2 * N_HEADS + ga[h]) for i, h in probs]
        eg_col = [col(i, 'e_gc', ga[h]) for i, h in probs]
        qh = [q_ref[rows_of(i), cols[h]] for i, h in probs]
        kh = [k_ref[rows_of(i), cols[h]] for i, h in probs]
        vh = [v_ref[rows_of(i), cols[h]] for i, h in probs]
        n = range(len(probs))
        gram = [_dot_nt(jnp.concatenate([qh[p], kh[p]], axis=0).astype(bf16), kh[p].astype(bf16)) for p in n]
        decay = [jnp.exp(jnp.where(incl, col(i, 'gc', ga[h]) - gate[i]['gc_t'][ga[h]:ga[h] + 1, :], -jnp.inf))
                 for i, h in probs]
        a_low = [jnp.where(strict, gram[p][CHUNK:] * b_col[p] * decay[p], 0.0) for p in n]
        qk = [(gram[p][:CHUNK] * decay[p]).astype(bf16) for p in n]
        t_inv = _unit_tri_inverse(a_low, eye, diag_blk)
        rhs = [jnp.concatenate([vh[p] * b_col[p], kh[p] * b_col[p] * eg_col[p]], axis=1) for p in n]
        uw = [_bdot(t_inv[p], rhs[p]) for p in n]
        w_qg = [jnp.concatenate([uw[p][:, HEAD_D:], qh[p] * eg_col[p]], axis=0).astype(bf16) for p in n]
        kd = [(kh[p] * col(i, 'e_rest', ga[h])).astype(bf16) for p, (i, h) in enumerate(probs)]
        e_last = [col(i, 'e_last', ga[h]) for i, h in probs]

        def per_chunk(lst):
            return [lst[i * N_HEADS:(i + 1) * N_HEADS] for i in range(len(chunks))]
        return {'u': per_chunk([x[:, :HEAD_D] for x in uw]), 'w_qg': per_chunk(w_qg), 'qk': per_chunk(qk),
                'kd': per_chunk(kd), 'e_last': per_chunk(e_last)}

    order = list(range(nchunks))[::-1] if reverse else list(range(nchunks))
    pre = prepare(order)
    state = [s_scr[h] for h in heads]
    for i, cc in enumerate(order):
        rows = slice(cc * CHUNK, (cc + 1) * CHUNK)
        s_b = [x.astype(bf16) for x in state]
        ws = [_dot(pre['w_qg'][i][h], s_b[h]) for h in heads]
        v_new = [(pre['u'][i][h] - ws[h][:CHUNK]).astype(bf16) for h in heads]
        for h in heads:
            o_ref[rows, cols[h]] = ws[h][CHUNK:] + _dot(pre['qk'][i][h], v_new[h])
        state = [state[h] * pre['e_last'][i][h] + _dot_tn(pre['kd'][i][h], v_new[h]) for h in heads]
    for h in heads:
        s_scr[h] = state[h]

    if emit_state:
        @pl.when(t == tiles_per_seq - 1)
        def _():
            sfin_ref[...] = s_scr[...]


def _gdn_scan(path, qkv, proj, prm, s0, layer, reverse, emit_state):
    tile = SEQ_TILE
    tps = path.seq_len // tile
    d = 1 if reverse else 0

    def tok(s, t):
        return s * tps + ((tps - 1 - t) if reverse else t)

    in_specs = [
        pl.BlockSpec((tile, GDN_W), lambda s, t: (tok(s, t), 0)),
        pl.BlockSpec((tile, GDN_W), lambda s, t: (tok(s, t), 1)),
        pl.BlockSpec((tile, GDN_W), lambda s, t: (tok(s, t), 2)),
        pl.BlockSpec((tile, LANES), lambda s, t: (tok(s, t), P_AB // LANES)),
        pl.BlockSpec((2, LANES), lambda s, t: (0, 0)),
    ]
    args = [qkv, qkv, qkv, proj, prm]
    if s0 is not None:
        in_specs.append(pl.BlockSpec((None, None, None, N_HEADS, HEAD_D, HEAD_D),
                                     lambda s, t: (s, layer, d, 0, 0, 0)))
        args.append(s0)
    out_specs = [pl.BlockSpec((tile, GDN_W), lambda s, t: (tok(s, t), 0))]
    out_shape = [jax.ShapeDtypeStruct((path.tokens, GDN_W), f32)]
    if emit_state:
        out_specs.append(pl.BlockSpec((None, N_HEADS, HEAD_D, HEAD_D), lambda s, t: (s, 0, 0, 0)))
        out_shape.append(jax.ShapeDtypeStruct((path.nseq, N_HEADS, HEAD_D, HEAD_D), f32))
    kern = functools.partial(_gdn_kernel, reverse=reverse, tiles_per_seq=tps, zero_init=s0 is None,
                             emit_state=emit_state)
    return pl.pallas_call(
        kern, grid=(path.nseq, tps), in_specs=in_specs, out_specs=out_specs, out_shape=out_shape,
        scratch_shapes=[pltpu.VMEM((N_HEADS, HEAD_D, HEAD_D), f32)],
        compiler_params=_cparams(("arbitrary", "arbitrary")), name="gdn_scan")(*args)


def _gdn_combine_kernel(of_ref, ob_ref, z_ref, g_ref, o_ref):
    g = g_ref[...]
    for hd in range(N_HEADS):
        cols = slice(hd * HEAD_D, (hd + 1) * HEAD_D)
        y = _rms(of_ref[:, cols] + ob_ref[:, cols], g)
        o_ref[:, cols] = (y * _silu(z_ref[:, cols])).astype(o_ref.dtype)


def _gdn_combine(o_f, o_b, proj, g):
    t = o_f.shape[0]
    tile = 512
    return pl.pallas_call(
        _gdn_combine_kernel, grid=(t // tile,),
        in_specs=[pl.BlockSpec((tile, GDN_W), lambda i: (i, 0)),
                  pl.BlockSpec((tile, GDN_W), lambda i: (i, 0)),
                  pl.BlockSpec((tile, GDN_W), lambda i: (i, P_Z // GDN_W)),
                  pl.BlockSpec((1, HEAD_D), lambda i: (0, 0))],
        out_specs=pl.BlockSpec((tile, GDN_W), lambda i: (i, 0)),
        out_shape=jax.ShapeDtypeStruct((t, GDN_W), bf16),
        compiler_params=_cparams(("arbitrary",)), name="gdn_combine")(o_f, o_b, proj, g.reshape(1, HEAD_D))


def _lru_kernel(*refs, zero_init, emit_state):
    it = iter(refs)
    lx_ref, lg_ref, cw_ref, cb_ref = next(it), next(it), next(it), next(it)
    wa_ref, wi_ref, ba_ref, bi_ref, lam_ref = next(it), next(it), next(it), next(it), next(it)
    h0_ref = None if zero_init else next(it)
    y_ref = next(it)
    hfin_ref = next(it) if emit_state else None
    a_scr, b_scr = next(it), next(it)

    seq = lx_ref.shape[0]
    blk = 256
    nblk = seq // blk
    w_cat = jnp.concatenate([wa_ref[0], wi_ref[0], wa_ref[1], wi_ref[1]], axis=1).astype(bf16)
    bias_cat = jnp.concatenate([ba_ref[0:1], bi_ref[0:1], ba_ref[1:2], bi_ref[1:2]], axis=1)
    sp = jax.nn.softplus(-lam_ref[...])
    cw = cw_ref[...]
    cb = cb_ref[...]

    def gates_body(i, carry):
        r0 = pl.multiple_of(i * blk, blk)
        r_top = pl.multiple_of(jnp.maximum(r0 - SUBLANES, 0), SUBLANES)
        r_bot = pl.multiple_of(jnp.minimum(r0 + blk, seq - SUBLANES), SUBLANES)
        top = lx_ref[pl.ds(r_top, SUBLANES), :] * jnp.where(i > 0, 1.0, 0.0)
        bot = lx_ref[pl.ds(r_bot, SUBLANES), :] * jnp.where(i < nblk - 1, 1.0, 0.0)
        win = jnp.concatenate([top, lx_ref[pl.ds(r0, blk), :], bot], axis=0)
        xl = _conv_window(win, cw) + cb
        pre = _dot(xl.astype(bf16), w_cat) + bias_cat
        for d in range(2):
            r = _sigmoid(pre[:, (2 * d) * LRU_BW:(2 * d + 1) * LRU_BW])
            ig = _sigmoid(pre[:, (2 * d + 1) * LRU_BW:(2 * d + 2) * LRU_BW])
            a = jnp.exp(-RG_C * r * sp[d:d + 1])
            a_scr[d, pl.ds(r0, blk), :] = a
            b_scr[d, pl.ds(r0, blk), :] = jnp.sqrt(1.0 - a * a) * (ig * xl)
        return carry

    lax.fori_loop(0, nblk, gates_body, 0)

    row = lax.broadcasted_iota(jnp.int32, (CHUNK, LRU_BW), 0)
    nchunks = seq // CHUNK

    def chunk_scan(a, b, h_prev, reverse):
        s = 1
        while s < CHUNK:
            if reverse:
                keep = row < CHUNK - s
                a_sh = jnp.where(keep, pltpu.roll(a, CHUNK - s, 0), 1.0)
                b_sh = jnp.where(keep, pltpu.roll(b, CHUNK - s, 0), 0.0)
            else:
                keep = row >= s
                a_sh = jnp.where(keep, pltpu.roll(a, s, 0), 1.0)
                b_sh = jnp.where(keep, pltpu.roll(b, s, 0), 0.0)
            b = a * b_sh + b
            a = a * a_sh
            s *= 2
        return b + a * h_prev

    def scan_body(c, carry):
        hf, hb = carry
        rf = pl.multiple_of(c * CHUNK, CHUNK)
        rb = pl.multiple_of((nchunks - 1 - c) * CHUNK, CHUNK)
        h_f = chunk_scan(a_scr[0, pl.ds(rf, CHUNK), :], b_scr[0, pl.ds(rf, CHUNK), :], hf, False)
        h_b = chunk_scan(a_scr[1, pl.ds(rb, CHUNK), :], b_scr[1, pl.ds(rb, CHUNK), :], hb, True)
        b_scr[0, pl.ds(rf, CHUNK), :] = h_f
        b_scr[1, pl.ds(rb, CHUNK), :] = h_b
        return h_f[CHUNK - 1:CHUNK, :], h_b[0:1, :]

    if zero_init:
        init = (jnp.zeros((1, LRU_BW), f32), jnp.zeros((1, LRU_BW), f32))
    else:
        init = (h0_ref[0:1, :], h0_ref[1:2, :])
    hf, hb = lax.fori_loop(0, nchunks, scan_body, init)
    if emit_state:
        hfin_ref[0:1, :] = hf
        hfin_ref[1:2, :] = hb

    def out_body(i, carry):
        r0 = pl.multiple_of(i * blk, blk)
        h = b_scr[0, pl.ds(r0, blk), :] + b_scr[1, pl.ds(r0, blk), :]
        y_ref[pl.ds(r0, blk), :] = (h * jax.nn.gelu(lg_ref[pl.ds(r0, blk), :])).astype(y_ref.dtype)
        return carry

    lax.fori_loop(0, nblk, out_body, 0)


def _lru(path, proj, lp, h0, layer, emit_state):
    seq = path.seq_len
    proj3 = proj.reshape(path.nseq, seq, P_COLS)
    bx, bg = P_LX // LRU_BW, P_LG // LRU_BW
    in_specs = [
        pl.BlockSpec((None, seq, LRU_BW), lambda s, b: (s, 0, bx + b)),
        pl.BlockSpec((None, seq, LRU_BW), lambda s, b: (s, 0, bg + b)),
        pl.BlockSpec((CONV_W, LRU_BW), lambda s, b: (0, b)),
        pl.BlockSpec((1, LRU_BW), lambda s, b: (0, b)),
        pl.BlockSpec((2, None, LRU_BW, LRU_BW), lambda s, b: (0, b, 0, 0)),
        pl.BlockSpec((2, None, LRU_BW, LRU_BW), lambda s, b: (0, b, 0, 0)),
        pl.BlockSpec((2, LRU_BW), lambda s, b: (0, b)),
        pl.BlockSpec((2, LRU_BW), lambda s, b: (0, b)),
        pl.BlockSpec((2, LRU_BW), lambda s, b: (0, b)),
    ]
    args = [proj3, proj3, lp['lru_conv_w'], lp['lru_conv_b'].reshape(1, LRU_W), lp['lru_wa'], lp['lru_wi'],
            lp['lru_ba'], lp['lru_bi'], lp['lru_lambda']]
    if h0 is not None:
        in_specs.append(pl.BlockSpec((None, None, 2, LRU_BW), lambda s, b: (s, layer, 0, b)))
        args.append(h0)
    out_specs = [pl.BlockSpec((None, seq, LRU_BW), lambda s, b: (s, 0, b))]
    out_shape = [jax.ShapeDtypeStruct((path.nseq, seq, LRU_W), bf16)]
    if emit_state:
        out_specs.append(pl.BlockSpec((None, 2, LRU_BW), lambda s, b: (s, 0, b)))
        out_shape.append(jax.ShapeDtypeStruct((path.nseq, 2, LRU_W), f32))
    kern = functools.partial(_lru_kernel, zero_init=h0 is None, emit_state=emit_state)
    outs = pl.pallas_call(
        kern, grid=(path.nseq, LRU_BLOCKS), in_specs=in_specs, out_specs=out_specs, out_shape=out_shape,
        scratch_shapes=[pltpu.VMEM((2, seq, LRU_BW), f32), pltpu.VMEM((2, seq, LRU_BW), f32)],
        compiler_params=_cparams(("arbitrary", "arbitrary")), name="rglru")(*args)
    y = outs[0].reshape(path.tokens, LRU_W)
    return (y, outs[1]) if emit_state else (y, None)


def _pad_cols(w, n):
    return jnp.pad(w, ((0, 0),) * (w.ndim - 1) + ((0, n - w.shape[-1]),))


def _pos_table():
    quarter = D_MODEL // 4
    freqs = jnp.exp(-math.log(10000.0) * jnp.arange(quarter, dtype=f32) / quarter)
    e = jnp.arange(GRID_W, dtype=f32)[:, None] * freqs
    return jnp.concatenate([jnp.sin(e), jnp.cos(e)], axis=-1)


def _layer(path, x, h, layer, rows, mod3, lp, ffn, s0_gdn, h0_lru, emit_state, next_norm_g):
    proj = _matmul(h, lp['w_in'], 1024, 896, f32)
    qkv = _qkv_prep(path, proj, lp['gdn_conv_w'])
    outs_f = _gdn_scan(path, qkv, proj, lp['gdn_prm'], s0_gdn, layer, False, emit_state)
    outs_b = _gdn_scan(path, qkv, proj, lp['gdn_prm'], s0_gdn, layer, True, emit_state)
    o_mix = _gdn_combine(outs_f[0], outs_b[0], proj, lp['gdn_norm_g'])
    y_lru, h_fin = _lru(path, proj, lp, h0_lru, layer, emit_state)
    h_next = None
    if ffn['kind'] == 'dense':
        x, h2 = _out_proj(path, o_mix, y_lru, lp['w_out'], x, mod3, layer, rows, lp['norm2_g'])
        act = _gate_up(h2, ffn['w1'], ffn['w3'], 1024, 512)
        outs = _down_proj(path, act, ffn['w2'], x, mod3, layer, rows, 512, ffn['w2'].shape[0] // 4, next_norm_g)
        x = outs[0]
        h_next = outs[1] if next_norm_g is not None else None
    else:
        x, h2, route = _out_proj(path, o_mix, y_lru, lp['w_out'], x, mod3, layer, rows, lp['norm2_g'],
                                 router=(ffn['wr'], ffn['br']), h_dtype=f32)
        tok_of_row, gate_of_row, expert_of_tile, n_used, pos = _moe_route(route, path.tokens)
        h_rows = _row_gather(h2, tok_of_row, n_used)
        act = _moe_gate_up(h_rows, ffn['w1'], ffn['w3'], gate_of_row, expert_of_tile, n_used,
                           ffn['w1'].shape[2] // 2)
        y_rows = _moe_down(act, ffn['w2'], expert_of_tile, n_used)
        x = _moe_combine(path, y_rows, pos, x, mod3, layer, rows, final_g=ffn.get('final_g'))
    s_fin = jnp.stack([outs_f[1], outs_b[1]], axis=1) if emit_state else None
    return x, h_next, s_fin, h_fin


def kernel(x_prompt, x_sample, state_gdn, state_lru, c, c_ctx, w_mod, b_mod, norm1_g, norm2_g, w_in, gdn_conv_w, gdn_A_log, gdn_dt_bias, gdn_norm_g, lru_conv_w, lru_conv_b, lru_wa, lru_ba, lru_wi, lru_bi, lru_lambda, w_out, ffd_w1, ffd_w3, ffd_w2, moe_wr, moe_br, moe_w1, moe_w3, moe_w2, final_g):
    depth = w_mod.shape[0]
    bp, sp_len, d = x_prompt.shape
    bs, ss_len, _ = x_sample.shape
    ctx = _Path(bp, sp_len, 0, False)
    lat = _Path(bs, ss_len, 1, True)

    rows = SUBLANES * ((1 + bs + SUBLANES - 1) // SUBLANES)
    cvec = jnp.zeros((rows, d), f32).at[0].set(c_ctx).at[1:1 + bs].set(c)
    mod3 = _modulation(cvec, w_mod, b_mod).reshape(depth * rows * 6, 1, d)

    pos = _pos_table()
    xp = x_prompt.reshape(ctx.tokens, d)
    xs = x_sample.reshape(lat.tokens, d)
    new_gdn, new_lru = [], []
    hp = hs = None
    for l in range(depth):
        w_in_l = w_in[l].astype(bf16)
        n_ab = 4 * N_HEADS
        w_in_p = jnp.concatenate([w_in_l[:, :P_AB + n_ab], jnp.zeros((d, LANES - n_ab), bf16),
                                  w_in_l[:, P_AB + n_ab:]], axis=1)
        prm = jnp.stack([_pad_cols(gdn_A_log[l].reshape(1, -1), LANES)[0],
                         _pad_cols(gdn_dt_bias[l].reshape(1, -1), LANES)[0]])
        lp = {
            'norm1_g': norm1_g[l], 'norm2_g': norm2_g[l], 'w_in': w_in_p, 'gdn_conv_w': gdn_conv_w[l],
            'gdn_prm': prm, 'gdn_norm_g': gdn_norm_g[l], 'lru_conv_w': lru_conv_w[l], 'lru_conv_b': lru_conv_b[l],
            'lru_wa': lru_wa[l], 'lru_ba': lru_ba[l], 'lru_wi': lru_wi[l], 'lru_bi': lru_bi[l],
            'lru_lambda': lru_lambda[l], 'w_out': w_out[l].astype(bf16),
        }
        j = l // 2
        if l % 2 == 0:
            f = ffd_w1.shape[2]
            fp = 512 * ((f + 511) // 512)
            ffn = {'kind': 'dense', 'w1': _pad_cols(ffd_w1[j].astype(bf16), fp),
                   'w3': _pad_cols(ffd_w3[j].astype(bf16), fp),
                   'w2': jnp.pad(ffd_w2[j].astype(bf16), ((0, fp - f), (0, 0)))}
        else:
            ffn = {'kind': 'moe', 'wr': _pad_cols(moe_wr[j], LANES), 'br': _pad_cols(moe_br[j].reshape(1, -1), LANES),
                   'w1': moe_w1[j].astype(bf16), 'w3': moe_w3[j].astype(bf16), 'w2': moe_w2[j].astype(bf16)}
            if l == depth - 1:
                ffn['final_g'] = final_g
        next_g = norm1_g[l + 1] if l + 1 < depth else None
        if hp is None:
            (hp,) = _norm_mod(ctx, xp, lp['norm1_g'], mod3, l, rows, 1, 0)
        if hs is None and l == 0:
            hs, xs = _norm_mod(lat, xs, lp['norm1_g'], mod3, l, rows, 1, 0, pos=pos)
        elif hs is None:
            (hs,) = _norm_mod(lat, xs, lp['norm1_g'], mod3, l, rows, 1, 0)
        xp, hp, s_g, s_l = _layer(ctx, xp, hp, l, rows, mod3, lp, ffn, None, None, True, next_g)
        new_gdn.append(s_g)
        new_lru.append(s_l)
        xs, hs, _, _ = _layer(lat, xs, hs, l, rows, mod3, lp, ffn, state_gdn, state_lru, False, next_g)
    if depth % 2 == 1:
        xp, xs = _final_norm(xp, final_g), _final_norm(xs, final_g)
    y_prompt = xp.reshape(bp, sp_len, d)
    y_sample = xs.reshape(bs, ss_len, d)
    return (y_prompt, y_sample, jnp.stack(new_gdn, axis=1), jnp.stack(new_lru, axis=1))
```

```python
import functools
import math

import jax
import jax.numpy as jnp
from jax import lax
from jax.experimental import pallas as pl
from jax.experimental.pallas import tpu as pltpu

f32 = jnp.float32
bf16 = jnp.bfloat16

D_MODEL = 2048
GRID_W = 64
N_HEADS = 8
HEAD_D = 128
GDN_W = N_HEADS * HEAD_D
LRU_W = 1024
LRU_BLOCKS = 8
LRU_BW = LRU_W // LRU_BLOCKS
CONV_W = 4
CONV_LEFT = 2
CHUNK = 64
RG_C = 8.0
N_EXPERTS = 8
EPS = 1e-6
LANES = 128
SUBLANES = 8

P_Q = 0
P_K = P_Q + GDN_W
P_V = P_K + GDN_W
P_Z = P_V + GDN_W
P_AB = P_Z + GDN_W
P_LX = P_AB + LANES
P_LG = P_LX + LRU_W
P_COLS = P_LG + LRU_W

SEQ_TILE = 256
VMEM_LIMIT = 48 * 1024 * 1024


def _cparams(sem):
    return pltpu.CompilerParams(dimension_semantics=sem, vmem_limit_bytes=VMEM_LIMIT)


def _dot(a, b):
    return jnp.dot(a, b, preferred_element_type=f32)


def _sigmoid(x):
    return 0.5 * jnp.tanh(0.5 * x) + 0.5


def _silu(x):
    return x * _sigmoid(x)


def _dot_nt(a, b, precision=None):
    return lax.dot_general(a, b, (((1,), (1,)), ((), ())), precision=precision, preferred_element_type=f32)


def _dot_tn(a, b):
    return lax.dot_general(a, b, (((0,), (0,)), ((), ())), preferred_element_type=f32)


def _mod_kernel(c_ref, w_ref, b_ref, o_ref):
    a = _silu(c_ref[...]).astype(bf16)
    o_ref[...] = _dot(a, w_ref[...].astype(bf16)) + b_ref[...]


def _modulation(cvec, w_mod, b_mod):
    depth, d, n = w_mod.shape
    tn = 1024
    return pl.pallas_call(
        _mod_kernel,
        grid=(depth, n // tn),
        in_specs=[
            pl.BlockSpec((cvec.shape[0], d), lambda l, j: (0, 0)),
            pl.BlockSpec((None, d, tn), lambda l, j: (l, 0, j)),
            pl.BlockSpec((None, 1, tn), lambda l, j: (l, 0, j)),
        ],
        out_specs=pl.BlockSpec((None, cvec.shape[0], tn), lambda l, j: (l, 0, j)),
        out_shape=jax.ShapeDtypeStruct((depth, cvec.shape[0], n), f32),
        compiler_params=_cparams(("arbitrary", "arbitrary")),
        name="modulation",
    )(cvec, w_mod, b_mod.reshape(depth, 1, n))


class _Path:
    def __init__(self, nseq, seq_len, mod_row0, per_seq_mod):
        self.nseq = nseq
        self.seq_len = seq_len
        self.tokens = nseq * seq_len
        self.mod_row0 = mod_row0
        self.per_seq_mod = per_seq_mod

    def mod_row(self, tile_idx, tile):
        if self.per_seq_mod:
            return self.mod_row0 + tile_idx // (self.seq_len // tile)
        return self.mod_row0


def _mod_spec(path, layer, rows, part, tile, width, col_of=None, grid_rank=1):
    def imap(*idx):
        i = idx[0]
        j = idx[col_of] if col_of is not None else 0
        return ((layer * rows + path.mod_row(i, tile)) * 6 + part, 0, j)
    return pl.BlockSpec((None, 1, width), imap)


def _rms(x, g):
    return x * lax.rsqrt(jnp.mean(x * x, axis=-1, keepdims=True) + EPS) * g


R_IDX, R_GATE = 0, 2


def _emit_modulated_norm(x, g_ref, scale_ref, shift_ref, h_ref, wr_ref=None, br_ref=None, route_ref=None):
    h = _rms(x, g_ref[...]) * (1.0 + scale_ref[...]) + shift_ref[...]
    h_ref[...] = h.astype(h_ref.dtype)
    if route_ref is None:
        return
    logits = _dot(h.astype(bf16), wr_ref[...].astype(bf16)) + br_ref[...]
    lane = lax.broadcasted_iota(jnp.int32, logits.shape, 1)
    valid = lane < N_EXPERTS
    logits = jnp.where(valid, logits, -jnp.inf)
    e = jnp.exp(logits - jnp.max(logits, axis=-1, keepdims=True))
    p = e / jnp.sum(e, axis=-1, keepdims=True)
    p = jnp.where(valid, p, -1.0)
    m1 = jnp.max(p, axis=-1, keepdims=True)
    i1 = jnp.min(jnp.where(p == m1, lane, LANES), axis=-1, keepdims=True)
    p2 = jnp.where(lane == i1, -1.0, p)
    m2 = jnp.max(p2, axis=-1, keepdims=True)
    i2 = jnp.min(jnp.where(p2 == m2, lane, LANES), axis=-1, keepdims=True)
    den = m1 + m2
    route_ref[...] = (jnp.where(lane == R_IDX, i1.astype(f32), 0.0) + jnp.where(lane == R_IDX + 1, i2.astype(f32), 0.0)
                      + jnp.where(lane == R_GATE, m1 / den, 0.0) + jnp.where(lane == R_GATE + 1, m2 / den, 0.0))


def _norm_io(path, g, mod3, layer, rows, part_scale, part_shift, tile, router, h_dtype):
    d = D_MODEL
    in_specs = [pl.BlockSpec((1, d), lambda *idx: (0, 0)),
                _mod_spec(path, layer, rows, part_scale, tile, d),
                _mod_spec(path, layer, rows, part_shift, tile, d)]
    args = [g.reshape(1, d), mod3, mod3]
    out_specs = [pl.BlockSpec((tile, d), lambda *idx: (idx[0], 0))]
    out_shape = [jax.ShapeDtypeStruct((path.tokens, d), h_dtype)]
    if router is not None:
        in_specs += [pl.BlockSpec((d, LANES), lambda *idx: (0, 0)), pl.BlockSpec((1, LANES), lambda *idx: (0, 0))]
        args += list(router)
        out_specs.append(pl.BlockSpec((tile, LANES), lambda *idx: (idx[0], 0)))
        out_shape.append(jax.ShapeDtypeStruct((path.tokens, LANES), f32))
    return in_specs, args, out_specs, out_shape


def _norm_mod_kernel(*refs, with_pos, with_router, tile, tiles_per_seq):
    it = iter(refs)
    x_ref, g_ref, scale_ref, shift_ref = next(it), next(it), next(it), next(it)
    pos_ref = next(it) if with_pos else None
    wr_ref, br_ref = (next(it), next(it)) if with_router else (None, None)
    h_ref = next(it)
    xo_ref = next(it) if with_pos else None
    gates_ref = next(it) if with_router else None

    x = x_ref[...]
    if with_pos:
        rows_per_tile = tile // GRID_W
        row0 = (pl.program_id(0) % tiles_per_seq) * rows_per_tile
        half = D_MODEL // 2
        col_part = pos_ref[...]
        pieces = []
        for r in range(rows_per_tile):
            row_part = jnp.broadcast_to(pos_ref[pl.ds(row0 + r, 1), :], (GRID_W, half))
            pieces.append(jnp.concatenate([row_part, col_part], axis=1))
        x = x + jnp.concatenate(pieces, axis=0)
        xo_ref[...] = x
    _emit_modulated_norm(x, g_ref, scale_ref, shift_ref, h_ref, wr_ref, br_ref, gates_ref)


def _norm_mod(path, x, g, mod3, layer, rows, part_scale, part_shift, pos=None, router=None, h_dtype=bf16):
    tile = SEQ_TILE
    n = path.tokens // tile
    d = D_MODEL
    in_specs = [
        pl.BlockSpec((tile, d), lambda i: (i, 0)),
        pl.BlockSpec((1, d), lambda i: (0, 0)),
        _mod_spec(path, layer, rows, part_scale, tile, d),
        _mod_spec(path, layer, rows, part_shift, tile, d),
    ]
    args = [x, g.reshape(1, d), mod3, mod3]
    out_specs = [pl.BlockSpec((tile, d), lambda i: (i, 0))]
    out_shape = [jax.ShapeDtypeStruct((path.tokens, d), h_dtype)]
    if pos is not None:
        in_specs.append(pl.BlockSpec(pos.shape, lambda i: (0, 0)))
        args.append(pos)
        out_specs.append(pl.BlockSpec((tile, d), lambda i: (i, 0)))
        out_shape.append(jax.ShapeDtypeStruct((path.tokens, d), f32))
    if router is not None:
        wr, br = router
        in_specs += [pl.BlockSpec((d, LANES), lambda i: (0, 0)), pl.BlockSpec((1, LANES), lambda i: (0, 0))]
        args += [wr, br]
        out_specs.append(pl.BlockSpec((tile, LANES), lambda i: (i, 0)))
        out_shape.append(jax.ShapeDtypeStruct((path.tokens, LANES), f32))
    kern = functools.partial(_norm_mod_kernel, with_pos=pos is not None, with_router=router is not None,
                             tile=tile, tiles_per_seq=path.seq_len // tile)
    return pl.pallas_call(kern, grid=(n,), in_specs=in_specs, out_specs=out_specs, out_shape=out_shape,
                          compiler_params=_cparams(("arbitrary",)), name="norm_mod")(*args)


def _final_norm_kernel(x_ref, g_ref, o_ref):
    o_ref[...] = _rms(x_ref[...], g_ref[...])


def _final_norm(x, g):
    t, d = x.shape
    tile = 512
    return pl.pallas_call(
        _final_norm_kernel, grid=(t // tile,),
        in_specs=[pl.BlockSpec((tile, d), lambda i: (i, 0)), pl.BlockSpec((1, d), lambda i: (0, 0))],
        out_specs=pl.BlockSpec((tile, d), lambda i: (i, 0)),
        out_shape=jax.ShapeDtypeStruct((t, d), f32),
        compiler_params=_cparams(("arbitrary",)), name="final_norm")(x, g.reshape(1, d))


def _mm_kernel(a_ref, w_ref, o_ref):
    o_ref[...] = _dot(a_ref[...], w_ref[...]).astype(o_ref.dtype)


def _matmul(a, w, tm, tn, out_dtype):
    m, k = a.shape
    n = w.shape[1]
    return pl.pallas_call(
        _mm_kernel, grid=(m // tm, n // tn),
        in_specs=[pl.BlockSpec((tm, k), lambda i, j: (i, 0)), pl.BlockSpec((k, tn), lambda i, j: (0, j))],
        out_specs=pl.BlockSpec((tm, tn), lambda i, j: (i, j)),
        out_shape=jax.ShapeDtypeStruct((m, n), out_dtype),
        compiler_params=_cparams(("arbitrary", "arbitrary")), name="matmul")(a, w)


def _out_proj_kernel(*refs, with_router):
    a1_ref, a2_ref, w1_ref, w2_ref, x_ref, gate_ref, g_ref, scale_ref, shift_ref = refs[:9]
    wr_ref, br_ref = refs[9:11] if with_router else (None, None)
    outs = refs[11:] if with_router else refs[9:]
    xo_ref, h_ref = outs[:2]
    route_ref = outs[2] if with_router else None
    acc = _dot(a1_ref[...], w1_ref[...]) + _dot(a2_ref[...], w2_ref[...])
    x = x_ref[...] + gate_ref[...] * acc
    xo_ref[...] = x
    _emit_modulated_norm(x, g_ref, scale_ref, shift_ref, h_ref, wr_ref, br_ref, route_ref)


def _out_proj(path, a1, a2, w, x, mod3, layer, rows, norm_g, router=None, h_dtype=bf16):
    tm = 512
    m, d = x.shape
    k1, k2 = a1.shape[1], a2.shape[1]
    n_in, n_args, n_out, n_shape = _norm_io(path, norm_g, mod3, layer, rows, 4, 3, tm, router, h_dtype)
    once = pl.Buffered(1)
    return pl.pallas_call(
        functools.partial(_out_proj_kernel, with_router=router is not None), grid=(m // tm,),
        in_specs=[
            pl.BlockSpec((tm, k1), lambda i: (i, 0)),
            pl.BlockSpec((tm, k2), lambda i: (i, 0)),
            pl.BlockSpec((k1, d), lambda i: (0, 0), pipeline_mode=once),
            pl.BlockSpec((k2, d), lambda i: (k1 // k2, 0), pipeline_mode=once),
            pl.BlockSpec((tm, d), lambda i: (i, 0)),
            _mod_spec(path, layer, rows, 2, tm, d),
        ] + n_in,
        out_specs=[pl.BlockSpec((tm, d), lambda i: (i, 0))] + n_out,
        out_shape=[jax.ShapeDtypeStruct((m, d), f32)] + n_shape,
        compiler_params=_cparams(("arbitrary",)), name="out_proj")(a1, a2, w, w, x, mod3, *n_args)


def _gate_up_kernel(h_ref, w1_ref, w3_ref, o_ref):
    h = h_ref[...]
    a = _dot(h, w1_ref[...])
    b = _dot(h, w3_ref[...])
    o_ref[...] = (_silu(a) * b).astype(o_ref.dtype)


def _gate_up(h, w1, w3, tm, tn):
    m, k = h.shape
    n = w1.shape[1]
    return pl.pallas_call(
        _gate_up_kernel, grid=(m // tm, n // tn),
        in_specs=[pl.BlockSpec((tm, k), lambda i, j: (i, 0)),
                  pl.BlockSpec((k, tn), lambda i, j: (0, j)),
                  pl.BlockSpec((k, tn), lambda i, j: (0, j))],
        out_specs=pl.BlockSpec((tm, tn), lambda i, j: (i, j)),
        out_shape=jax.ShapeDtypeStruct((m, n), bf16),
        compiler_params=_cparams(("arbitrary", "arbitrary")), name="gate_up")(h, w1, w3)


MOE_TILE = 256
DMA_UNROLL = 8
GATHER_ROWS = 4 * MOE_TILE


def _moe_route(route, n_tok):
    tm = MOE_TILE
    n_rows = 2 * n_tok + N_EXPERTS * tm
    e_idx = jnp.clip(route[:, R_IDX:R_IDX + 2].astype(jnp.int32), 0, N_EXPERTS - 1).reshape(-1)
    gate = route[:, R_GATE:R_GATE + 2].reshape(-1)
    one_hot = (e_idx[:, None] == jnp.arange(N_EXPERTS, dtype=jnp.int32)[None, :]).astype(jnp.int32)
    csum = jnp.cumsum(one_hot, axis=0)
    rank = jnp.take_along_axis(csum, e_idx[:, None], axis=1)[:, 0] - 1
    padded = ((csum[-1] + tm - 1) // tm) * tm
    ends = jnp.cumsum(padded)
    pos = (ends - padded)[e_idx] + rank
    tok = (jnp.arange(2 * n_tok, dtype=jnp.int32) // 2).astype(f32)
    table = jnp.zeros((n_rows, 2), f32).at[pos].set(jnp.stack([tok, gate], axis=1))
    tok_of_row = table[:, 0].astype(jnp.int32)
    gate_of_row = table[:, 1]
    tile_start = jnp.arange(n_rows // tm, dtype=jnp.int32) * tm
    expert_of_tile = jnp.minimum(jnp.sum(tile_start[:, None] >= ends[None, :], axis=1), N_EXPERTS - 1)
    n_used = (ends[-1:] // tm).astype(jnp.int32)
    return tok_of_row, gate_of_row.reshape(n_rows, 1), expert_of_tile.astype(jnp.int32), n_used, pos.astype(jnp.int32)


def _row_copy(src_hbm, src_row, dst_vmem, dst_row, sem):
    return pltpu.make_async_copy(src_hbm.at[pl.ds(src_row, 1), :], dst_vmem.at[pl.ds(dst_row, 1), :], sem)


def _row_gather_kernel(tok_ref, nused_ref, h_hbm, o_ref, sem):
    i = pl.program_id(0)
    tg = o_ref.shape[0]
    used = i * (tg // MOE_TILE) < nused_ref[0]

    @pl.when(used)
    def _():
        def issue(r, c):
            _row_copy(h_hbm, tok_ref[i * tg + r], o_ref, r, sem).start()
            return c
        lax.fori_loop(0, tg, issue, 0, unroll=DMA_UNROLL)
        pltpu.make_async_copy(h_hbm.at[pl.ds(0, tg), :], o_ref, sem).wait()

    @pl.when(jnp.logical_not(used))
    def _():
        o_ref[...] = jnp.zeros_like(o_ref)


def _row_gather(h, tok_of_row, n_used):
    n_rows = tok_of_row.shape[0]
    d = h.shape[1]
    assert n_rows % GATHER_ROWS == 0
    return pl.pallas_call(
        _row_gather_kernel,
        grid_spec=pltpu.PrefetchScalarGridSpec(
            num_scalar_prefetch=2, grid=(n_rows // GATHER_ROWS,),
            in_specs=[pl.BlockSpec(memory_space=pl.ANY)],
            out_specs=pl.BlockSpec((GATHER_ROWS, d), lambda i, tok, nu: (i, 0)),
            scratch_shapes=[pltpu.SemaphoreType.DMA]),
        out_shape=jax.ShapeDtypeStruct((n_rows, d), h.dtype),
        compiler_params=_cparams(("arbitrary",)), name="moe_row_gather")(tok_of_row, n_used, h)


def _moe_gate_up_kernel(eot_ref, nused_ref, x_ref, w1_ref, w3_ref, g_ref, o_ref):
    i = pl.program_id(1)

    @pl.when(i < nused_ref[0])
    def _():
        x = x_ref[...].astype(bf16)
        a = _dot(x, w1_ref[...])
        b = _dot(x, w3_ref[...])
        o_ref[...] = (g_ref[...] * (_silu(a) * b)).astype(o_ref.dtype)

    @pl.when(i >= nused_ref[0])
    def _():
        o_ref[...] = jnp.zeros_like(o_ref)


def _moe_gate_up(rows, w1, w3, gate_of_row, expert_of_tile, n_used, tn):
    tm = MOE_TILE
    n_rows, k = rows.shape
    f = w1.shape[2]
    return pl.pallas_call(
        _moe_gate_up_kernel,
        grid_spec=pltpu.PrefetchScalarGridSpec(
            num_scalar_prefetch=2, grid=(f // tn, n_rows // tm),
            in_specs=[pl.BlockSpec((tm, k), lambda j, i, eot, nu: (i, 0)),
                      pl.BlockSpec((None, k, tn), lambda j, i, eot, nu: (eot[i], 0, j)),
                      pl.BlockSpec((None, k, tn), lambda j, i, eot, nu: (eot[i], 0, j)),
                      pl.BlockSpec((tm, 1), lambda j, i, eot, nu: (i, 0))],
            out_specs=pl.BlockSpec((tm, tn), lambda j, i, eot, nu: (i, j))),
        out_shape=jax.ShapeDtypeStruct((n_rows, f), bf16),
        compiler_params=_cparams(("arbitrary", "arbitrary")), name="moe_gate_up")(
            expert_of_tile, n_used, rows, w1, w3, gate_of_row)


def _moe_down_kernel(eot_ref, nused_ref, a_ref, w_ref, o_ref):
    i = pl.program_id(0)

    @pl.when(i < nused_ref[0])
    def _():
        o_ref[...] = _dot(a_ref[...], w_ref[...])

    @pl.when(i >= nused_ref[0])
    def _():
        o_ref[...] = jnp.zeros_like(o_ref)


def _moe_down(act, w2, expert_of_tile, n_used):
    tm = MOE_TILE
    n_rows, f = act.shape
    d = w2.shape[2]
    return pl.pallas_call(
        _moe_down_kernel,
        grid_spec=pltpu.PrefetchScalarGridSpec(
            num_scalar_prefetch=2, grid=(n_rows // tm,),
            in_specs=[pl.BlockSpec((tm, f), lambda i, eot, nu: (i, 0)),
                      pl.BlockSpec((None, f, d), lambda i, eot, nu: (eot[i], 0, 0))],
            out_specs=pl.BlockSpec((tm, d), lambda i, eot, nu: (i, 0))),
        out_shape=jax.ShapeDtypeStruct((n_rows, d), f32),
        compiler_params=_cparams(("arbitrary",)), name="moe_down")(expert_of_tile, n_used, act, w2)


def _moe_combine_kernel(pos_ref, y_hbm, x_ref, gate_ref, *rest):
    fg_ref = rest[0] if len(rest) == 4 else None
    o_ref, buf, sem = rest[-3:]
    i = pl.program_id(0)
    tm = o_ref.shape[0]

    def issue_tile(t, slot):
        def issue(r, c):
            a = 2 * (t * tm + r)
            _row_copy(y_hbm, pos_ref[a], buf.at[slot, 0], r, sem.at[slot]).start()
            _row_copy(y_hbm, pos_ref[a + 1], buf.at[slot, 1], r, sem.at[slot]).start()
            return c
        lax.fori_loop(0, tm, issue, 0, unroll=DMA_UNROLL)

    @pl.when(i == 0)
    def _():
        issue_tile(0, 0)

    @pl.when(i + 1 < pl.num_programs(0))
    def _():
        issue_tile(i + 1, (i + 1) % 2)

    slot = i % 2
    for b in range(2):
        pltpu.make_async_copy(y_hbm.at[pl.ds(0, tm), :], buf.at[slot, b], sem.at[slot]).wait()
    y = x_ref[...] + gate_ref[...] * (buf[slot, 0] + buf[slot, 1])
    o_ref[...] = y if fg_ref is None else _rms(y, fg_ref[...])


def _moe_combine(path, y_rows, pos, x, mod3, layer, rows, final_g=None):
    tm = SEQ_TILE
    m, d = x.shape
    in_specs = [pl.BlockSpec(memory_space=pl.ANY),
                pl.BlockSpec((tm, d), lambda i, pos: (i, 0)),
                pl.BlockSpec((None, 1, d), lambda i, pos: ((layer * rows + path.mod_row(i, tm)) * 6 + 5, 0, 0))]
    args = [pos, y_rows, x, mod3]
    if final_g is not None:
        in_specs.append(pl.BlockSpec((1, d), lambda i, pos: (0, 0)))
        args.append(final_g.reshape(1, d))
    return pl.pallas_call(
        _moe_combine_kernel,
        grid_spec=pltpu.PrefetchScalarGridSpec(
            num_scalar_prefetch=1, grid=(m // tm,), in_specs=in_specs,
            out_specs=pl.BlockSpec((tm, d), lambda i, pos: (i, 0)),
            scratch_shapes=[pltpu.VMEM((2, 2, tm, d), f32), pltpu.SemaphoreType.DMA((2,))]),
        out_shape=jax.ShapeDtypeStruct((m, d), f32),
        compiler_params=_cparams(("arbitrary",)), name="moe_combine")(*args)


def _down_kernel(*refs, nk, with_norm):
    a_ref, w_ref, x_ref, gate_ref = refs[:4]
    g_ref, scale_ref, shift_ref = refs[4:7] if with_norm else (None, None, None)
    outs = refs[7:] if with_norm else refs[4:]
    o_ref = outs[0]
    h_ref = outs[1] if with_norm else None
    acc_ref = outs[-1]
    kk = pl.program_id(1)

    @pl.when(kk == 0)
    def _():
        acc_ref[...] = jnp.zeros_like(acc_ref)

    acc_ref[...] += _dot(a_ref[...], w_ref[...])

    @pl.when(kk == nk - 1)
    def _():
        x = x_ref[...] + gate_ref[...] * acc_ref[...]
        o_ref[...] = x
        if with_norm:
            _emit_modulated_norm(x, g_ref, scale_ref, shift_ref, h_ref)


def _down_proj(path, act, w2, x, mod3, layer, rows, tm, tk, next_norm_g=None):
    m, d = x.shape
    kdim = act.shape[1]
    nk = kdim // tk
    n_in, n_args, n_out, n_shape = ([], [], [], [])
    if next_norm_g is not None:
        n_in, n_args, n_out, n_shape = _norm_io(path, next_norm_g, mod3, layer + 1, rows, 1, 0, tm, None, bf16)
    return pl.pallas_call(
        functools.partial(_down_kernel, nk=nk, with_norm=next_norm_g is not None), grid=(m // tm, nk),
        in_specs=[pl.BlockSpec((tm, tk), lambda i, k: (i, k)),
                  pl.BlockSpec((tk, d), lambda i, k: (k, 0)),
                  pl.BlockSpec((tm, d), lambda i, k: (i, 0)),
                  _mod_spec(path, layer, rows, 5, tm, d)] + n_in,
        out_specs=[pl.BlockSpec((tm, d), lambda i, k: (i, 0))] + n_out,
        out_shape=[jax.ShapeDtypeStruct((m, d), f32)] + n_shape,
        scratch_shapes=[pltpu.VMEM((tm, d), f32)],
        compiler_params=_cparams(("arbitrary", "arbitrary")), name="down_proj")(act, w2, x, mod3, *n_args)


def _conv_window(win, w):
    n = win.shape[0] - 2 * SUBLANES
    acc = None
    for j in range(CONV_W):
        off = SUBLANES + j - CONV_LEFT
        term = win[off:off + n] * w[j:j + 1]
        acc = term if acc is None else acc + term
    return acc


def _qkv_prep_kernel(prev_ref, cur_ref, next_ref, w_ref, o_ref, *, tiles_per_seq):
    t = pl.program_id(0) % tiles_per_seq
    kind = pl.program_id(1)
    has_prev = (t > 0).astype(f32)
    has_next = (t < tiles_per_seq - 1).astype(f32)
    scale = jnp.where(kind == 0, HEAD_D ** -0.5, 1.0).astype(f32)
    tile = cur_ref.shape[0]
    for c in range(tile // CHUNK):
        r0 = c * CHUNK
        for hd in range(N_HEADS):
            cols = slice(hd * HEAD_D, (hd + 1) * HEAD_D)
            top = prev_ref[:, cols] * has_prev if c == 0 else cur_ref[r0 - SUBLANES:r0, cols]
            bot = (next_ref[:, cols] * has_next if r0 + CHUNK == tile
                   else cur_ref[r0 + CHUNK:r0 + CHUNK + SUBLANES, cols])
            win = jnp.concatenate([top, cur_ref[r0:r0 + CHUNK, cols], bot], axis=0)
            y = _silu(_conv_window(win, w_ref[:, cols]))
            nrm = lax.rsqrt(jnp.sum(y * y, axis=-1, keepdims=True) + EPS) * scale
            y = y * jnp.where(kind < 2, nrm, 1.0)
            o_ref[r0:r0 + CHUNK, cols] = y


def _qkv_prep(path, proj, conv_w):
    tile = SEQ_TILE
    n = path.tokens // tile
    tps = path.seq_len // tile
    rb = tile // SUBLANES
    last = path.tokens // SUBLANES - 1
    return pl.pallas_call(
        functools.partial(_qkv_prep_kernel, tiles_per_seq=tps), grid=(n, 3),
        in_specs=[
            pl.BlockSpec((SUBLANES, GDN_W), lambda i, j: (jnp.maximum(i * rb - 1, 0), j)),
            pl.BlockSpec((tile, GDN_W), lambda i, j: (i, j)),
            pl.BlockSpec((SUBLANES, GDN_W), lambda i, j: (jnp.minimum((i + 1) * rb, last), j)),
            pl.BlockSpec((CONV_W, GDN_W), lambda i, j: (0, j)),
        ],
        out_specs=pl.BlockSpec((tile, GDN_W), lambda i, j: (i, j)),
        out_shape=jax.ShapeDtypeStruct((path.tokens, 3 * GDN_W), f32),
        compiler_params=_cparams(("arbitrary", "arbitrary")), name="qkv_prep")(proj, proj, proj, conv_w)


INV_BLOCK = 16


def _bdot(a, b):
    return _dot(a.astype(bf16), b.astype(bf16))


def _unit_tri_inverse(a_list, eye, diag_blk):
    p = [-jnp.where(diag_blk, a, 0.0) for a in a_list]
    x = [eye + pi for pi in p]
    span = 2
    while span < INV_BLOCK:
        p = [_bdot(pi, pi) for pi in p]
        x = [xi + _bdot(xi, pi) for xi, pi in zip(x, p)]
        span *= 2
    ia = [(eye + a).astype(bf16) for a in a_list]
    span = INV_BLOCK
    while span < a_list[0].shape[0]:
        r = [eye - _dot(iai, xi.astype(bf16)) for iai, xi in zip(ia, x)]
        x = [xi + _bdot(xi, ri) for xi, ri in zip(x, r)]
        span *= 2
    return x


def _gdn_kernel(*refs, reverse, tiles_per_seq, zero_init, emit_state, combine):
    it = iter(refs)
    q_ref, k_ref, v_ref, ab_ref, prm_ref = next(it), next(it), next(it), next(it), next(it)
    s0_ref = None if zero_init else next(it)
    other_ref, z_ref, gn_ref = (next(it), next(it), next(it)) if combine else (None, None, None)
    o_ref = next(it)
    sfin_ref = next(it) if emit_state else None
    s_scr = next(it)

    t = pl.program_id(1)
    nchunks = q_ref.shape[0] // CHUNK

    @pl.when(t == 0)
    def _():
        if zero_init:
            s_scr[...] = jnp.zeros_like(s_scr)
        else:
            s_scr[...] = s0_ref[...]

    ri = lax.broadcasted_iota(jnp.int32, (CHUNK, CHUNK), 0)
    ci = lax.broadcasted_iota(jnp.int32, (CHUNK, CHUNK), 1)
    incl = (ri <= ci) if reverse else (ri >= ci)
    strict = (ri < ci) if reverse else (ri > ci)
    eye = (ri == ci).astype(f32)
    diag_blk = (ri // INV_BLOCK) == (ci // INV_BLOCK)
    tri = incl.astype(f32)
    last_row = 0 if reverse else CHUNK - 1
    dir_off = N_HEADS if reverse else 0
    neg_a = -jnp.exp(prm_ref[0:1, :])
    dt_bias = prm_ref[1:2, :]

    heads = range(N_HEADS)
    cols = [slice(hd * HEAD_D, (hd + 1) * HEAD_D) for hd in heads]
    ga = [dir_off + hd for hd in heads]

    def prepare(chunks):
        gate = []
        for cc in chunks:
            rows = slice(cc * CHUNK, (cc + 1) * CHUNK)
            ab = ab_ref[rows, :]
            g = neg_a * jax.nn.softplus(ab + dt_bias)
            beta = _sigmoid(ab)
            gc = jnp.dot(tri, g, precision=lax.Precision.HIGHEST, preferred_element_type=f32)
            g_last = gc[last_row:last_row + 1, :]
            gate.append({'beta': beta, 'gc': gc, 'gc_t': gc.T, 'e_gc': jnp.exp(gc),
                         'e_rest': jnp.exp(g_last - gc), 'e_last': jnp.exp(g_last)})
        probs = [(i, h) for i in range(len(chunks)) for h in heads]

        def rows_of(i):
            return slice(chunks[i] * CHUNK, (chunks[i] + 1) * CHUNK)

        def col(i, name, lane):
            return gate[i][name][:, lane:lane + 1]

        b_col = [col(i, 'beta', 2 * N_HEADS + ga[h]) for i, h in probs]
        eg_col = [col(i, 'e_gc', ga[h]) for i, h in probs]
        qh = [q_ref[rows_of(i), cols[h]] for i, h in probs]
        kh = [k_ref[rows_of(i), cols[h]] for i, h in probs]
        vh = [v_ref[rows_of(i), cols[h]] for i, h in probs]
        n = range(len(probs))
        gram = [_dot_nt(jnp.concatenate([qh[p], kh[p]], axis=0).astype(bf16), kh[p].astype(bf16)) for p in n]
        decay = [jnp.exp(jnp.where(incl, col(i, 'gc', ga[h]) - gate[i]['gc_t'][ga[h]:ga[h] + 1, :], -jnp.inf))
                 for i, h in probs]
        a_low = [jnp.where(strict, gram[p][CHUNK:] * b_col[p] * decay[p], 0.0) for p in n]
        qk = [(gram[p][:CHUNK] * decay[p]).astype(bf16) for p in n]
        t_inv = _unit_tri_inverse(a_low, eye, diag_blk)
        rhs = [jnp.concatenate([vh[p] * b_col[p], kh[p] * b_col[p] * eg_col[p]], axis=1) for p in n]
        uw = [_bdot(t_inv[p], rhs[p]) for p in n]
        w_qg = [jnp.concatenate([uw[p][:, HEAD_D:], qh[p] * eg_col[p]], axis=0).astype(bf16) for p in n]
        kd = [(kh[p] * col(i, 'e_rest', ga[h])).astype(bf16) for p, (i, h) in enumerate(probs)]
        e_last = [col(i, 'e_last', ga[h]) for i, h in probs]

        def per_chunk(lst):
            return [lst[i * N_HEADS:(i + 1) * N_HEADS] for i in range(len(chunks))]
        return {'u': per_chunk([x[:, :HEAD_D] for x in uw]), 'w_qg': per_chunk(w_qg), 'qk': per_chunk(qk),
                'kd': per_chunk(kd), 'e_last': per_chunk(e_last)}

    order = list(range(nchunks))[::-1] if reverse else list(range(nchunks))
    pre = prepare(order)
    state = [s_scr[h] for h in heads]
    for i, cc in enumerate(order):
        rows = slice(cc * CHUNK, (cc + 1) * CHUNK)
        s_b = [x.astype(bf16) for x in state]
        ws = [_dot(pre['w_qg'][i][h], s_b[h]) for h in heads]
        v_new = [(pre['u'][i][h] - ws[h][:CHUNK]).astype(bf16) for h in heads]
        for h in heads:
            o_h = ws[h][CHUNK:] + _dot(pre['qk'][i][h], v_new[h])
            if combine:
                y = _rms(other_ref[rows, cols[h]] + o_h, gn_ref[...])
                o_h = y * _silu(z_ref[rows, cols[h]])
            o_ref[rows, cols[h]] = o_h.astype(o_ref.dtype)
        state = [state[h] * pre['e_last'][i][h] + _dot_tn(pre['kd'][i][h], v_new[h]) for h in heads]
    for h in heads:
        s_scr[h] = state[h]

    if emit_state:
        @pl.when(t == tiles_per_seq - 1)
        def _():
            sfin_ref[...] = s_scr[...]


def _gdn_scan(path, qkv, proj, prm, s0, layer, reverse, emit_state, combine_with=None, norm_g=None):
    tile = SEQ_TILE
    tps = path.seq_len // tile
    d = 1 if reverse else 0

    def tok(s, t):
        return s * tps + ((tps - 1 - t) if reverse else t)

    in_specs = [
        pl.BlockSpec((tile, GDN_W), lambda s, t: (tok(s, t), 0)),
        pl.BlockSpec((tile, GDN_W), lambda s, t: (tok(s, t), 1)),
        pl.BlockSpec((tile, GDN_W), lambda s, t: (tok(s, t), 2)),
        pl.BlockSpec((tile, LANES), lambda s, t: (tok(s, t), P_AB // LANES)),
        pl.BlockSpec((2, LANES), lambda s, t: (0, 0)),
    ]
    args = [qkv, qkv, qkv, proj, prm]
    if s0 is not None:
        in_specs.append(pl.BlockSpec((None, None, None, N_HEADS, HEAD_D, HEAD_D),
                                     lambda s, t: (s, layer, d, 0, 0, 0)))
        args.append(s0)
    combine = combine_with is not None
    if combine:
        in_specs += [pl.BlockSpec((tile, GDN_W), lambda s, t: (tok(s, t), 0)),
                     pl.BlockSpec((tile, GDN_W), lambda s, t: (tok(s, t), P_Z // GDN_W)),
                     pl.BlockSpec((1, HEAD_D), lambda s, t: (0, 0))]
        args += [combine_with, proj, norm_g.reshape(1, HEAD_D)]
    out_specs = [pl.BlockSpec((tile, GDN_W), lambda s, t: (tok(s, t), 0))]
    out_shape = [jax.ShapeDtypeStruct((path.tokens, GDN_W), bf16 if combine else f32)]
    if emit_state:
        out_specs.append(pl.BlockSpec((None, N_HEADS, HEAD_D, HEAD_D), lambda s, t: (s, 0, 0, 0)))
        out_shape.append(jax.ShapeDtypeStruct((path.nseq, N_HEADS, HEAD_D, HEAD_D), f32))
    kern = functools.partial(_gdn_kernel, reverse=reverse, tiles_per_seq=tps, zero_init=s0 is None,
                             emit_state=emit_state, combine=combine)
    return pl.pallas_call(
        kern, grid=(path.nseq, tps), in_specs=in_specs, out_specs=out_specs, out_shape=out_shape,
        scratch_shapes=[pltpu.VMEM((N_HEADS, HEAD_D, HEAD_D), f32)],
        compiler_params=_cparams(("arbitrary", "arbitrary")), name="gdn_scan")(*args)


def _lru_kernel(*refs, zero_init, emit_state):
    it = iter(refs)
    lx_ref, lg_ref, cw_ref, cb_ref = next(it), next(it), next(it), next(it)
    wa_ref, wi_ref, ba_ref, bi_ref, lam_ref = next(it), next(it), next(it), next(it), next(it)
    h0_ref = None if zero_init else next(it)
    y_ref = next(it)
    hfin_ref = next(it) if emit_state else None
    a_scr, b_scr = next(it), next(it)

    seq = lx_ref.shape[0]
    blk = 256
    nblk = seq // blk
    w_cat = jnp.concatenate([wa_ref[0], wi_ref[0], wa_ref[1], wi_ref[1]], axis=1).astype(bf16)
    bias_cat = jnp.concatenate([ba_ref[0:1], bi_ref[0:1], ba_ref[1:2], bi_ref[1:2]], axis=1)
    sp = jax.nn.softplus(-lam_ref[...])
    cw = cw_ref[...]
    cb = cb_ref[...]

    def gates_body(i, carry):
        r0 = pl.multiple_of(i * blk, blk)
        r_top = pl.multiple_of(jnp.maximum(r0 - SUBLANES, 0), SUBLANES)
        r_bot = pl.multiple_of(jnp.minimum(r0 + blk, seq - SUBLANES), SUBLANES)
        top = lx_ref[pl.ds(r_top, SUBLANES), :] * jnp.where(i > 0, 1.0, 0.0)
        bot = lx_ref[pl.ds(r_bot, SUBLANES), :] * jnp.where(i < nblk - 1, 1.0, 0.0)
        win = jnp.concatenate([top, lx_ref[pl.ds(r0, blk), :], bot], axis=0)
        xl = _conv_window(win, cw) + cb
        pre = _dot(xl.astype(bf16), w_cat) + bias_cat
        for d in range(2):
            r = _sigmoid(pre[:, (2 * d) * LRU_BW:(2 * d + 1) * LRU_BW])
            ig = _sigmoid(pre[:, (2 * d + 1) * LRU_BW:(2 * d + 2) * LRU_BW])
            a = jnp.exp(-RG_C * r * sp[d:d + 1])
            a_scr[d, pl.ds(r0, blk), :] = a
            b_scr[d, pl.ds(r0, blk), :] = jnp.sqrt(1.0 - a * a) * (ig * xl)
        return carry

    lax.fori_loop(0, nblk, gates_body, 0)

    groups = CHUNK // SUBLANES
    row = lax.broadcasted_iota(jnp.int32, (groups, SUBLANES, LRU_BW), 1)
    nchunks = seq // CHUNK

    def chunk_scan(a, b, h_prev, reverse):
        a = a.reshape(groups, SUBLANES, LRU_BW)
        b = b.reshape(groups, SUBLANES, LRU_BW)
        s = 1
        while s < SUBLANES:
            if reverse:
                keep = row < SUBLANES - s
                a_sh = jnp.where(keep, pltpu.roll(a, SUBLANES - s, 1), 1.0)
                b_sh = jnp.where(keep, pltpu.roll(b, SUBLANES - s, 1), 0.0)
            else:
                keep = row >= s
                a_sh = jnp.where(keep, pltpu.roll(a, s, 1), 1.0)
                b_sh = jnp.where(keep, pltpu.roll(b, s, 1), 0.0)
            b = a * b_sh + b
            a = a * a_sh
            s *= 2
        out = [None] * groups
        for g in (range(groups - 1, -1, -1) if reverse else range(groups)):
            out[g] = b[g] + a[g] * h_prev
            h_prev = out[g][0:1, :] if reverse else out[g][SUBLANES - 1:SUBLANES, :]
        return jnp.concatenate(out, axis=0)

    def scan_body(c, carry):
        hf, hb = carry
        rf = pl.multiple_of(c * CHUNK, CHUNK)
        rb = pl.multiple_of((nchunks - 1 - c) * CHUNK, CHUNK)
        h_f = chunk_scan(a_scr[0, pl.ds(rf, CHUNK), :], b_scr[0, pl.ds(rf, CHUNK), :], hf, False)
        h_b = chunk_scan(a_scr[1, pl.ds(rb, CHUNK), :], b_scr[1, pl.ds(rb, CHUNK), :], hb, True)
        b_scr[0, pl.ds(rf, CHUNK), :] = h_f
        b_scr[1, pl.ds(rb, CHUNK), :] = h_b
        return h_f[CHUNK - 1:CHUNK, :], h_b[0:1, :]

    if zero_init:
        init = (jnp.zeros((1, LRU_BW), f32), jnp.zeros((1, LRU_BW), f32))
    else:
        init = (h0_ref[0:1, :], h0_ref[1:2, :])
    hf, hb = lax.fori_loop(0, nchunks, scan_body, init)
    if emit_state:
        hfin_ref[0:1, :] = hf
        hfin_ref[1:2, :] = hb

    def out_body(i, carry):
        r0 = pl.multiple_of(i * blk, blk)
        h = b_scr[0, pl.ds(r0, blk), :] + b_scr[1, pl.ds(r0, blk), :]
        y_ref[pl.ds(r0, blk), :] = (h * jax.nn.gelu(lg_ref[pl.ds(r0, blk), :])).astype(y_ref.dtype)
        return carry

    lax.fori_loop(0, nblk, out_body, 0)


def _lru(path, proj, lp, h0, layer, emit_state):
    seq = path.seq_len
    proj3 = proj.reshape(path.nseq, seq, P_COLS)
    bx, bg = P_LX // LRU_BW, P_LG // LRU_BW
    in_specs = [
        pl.BlockSpec((None, seq, LRU_BW), lambda s, b: (s, 0, bx + b)),
        pl.BlockSpec((None, seq, LRU_BW), lambda s, b: (s, 0, bg + b)),
        pl.BlockSpec((CONV_W, LRU_BW), lambda s, b: (0, b)),
        pl.BlockSpec((1, LRU_BW), lambda s, b: (0, b)),
        pl.BlockSpec((2, None, LRU_BW, LRU_BW), lambda s, b: (0, b, 0, 0)),
        pl.BlockSpec((2, None, LRU_BW, LRU_BW), lambda s, b: (0, b, 0, 0)),
        pl.BlockSpec((2, LRU_BW), lambda s, b: (0, b)),
        pl.BlockSpec((2, LRU_BW), lambda s, b: (0, b)),
        pl.BlockSpec((2, LRU_BW), lambda s, b: (0, b)),
    ]
    args = [proj3, proj3, lp['lru_conv_w'], lp['lru_conv_b'].reshape(1, LRU_W), lp['lru_wa'], lp['lru_wi'],
            lp['lru_ba'], lp['lru_bi'], lp['lru_lambda']]
    if h0 is not None:
        in_specs.append(pl.BlockSpec((None, None, 2, LRU_BW), lambda s, b: (s, layer, 0, b)))
        args.append(h0)
    out_specs = [pl.BlockSpec((None, seq, LRU_BW), lambda s, b: (s, 0, b))]
    out_shape = [jax.ShapeDtypeStruct((path.nseq, seq, LRU_W), bf16)]
    if emit_state:
        out_specs.append(pl.BlockSpec((None, 2, LRU_BW), lambda s, b: (s, 0, b)))
        out_shape.append(jax.ShapeDtypeStruct((path.nseq, 2, LRU_W), f32))
    kern = functools.partial(_lru_kernel, zero_init=h0 is None, emit_state=emit_state)
    outs = pl.pallas_call(
        kern, grid=(path.nseq, LRU_BLOCKS), in_specs=in_specs, out_specs=out_specs, out_shape=out_shape,
        scratch_shapes=[pltpu.VMEM((2, seq, LRU_BW), f32), pltpu.VMEM((2, seq, LRU_BW), f32)],
        compiler_params=_cparams(("arbitrary", "arbitrary")), name="rglru")(*args)
    y = outs[0].reshape(path.tokens, LRU_W)
    return (y, outs[1]) if emit_state else (y, None)


def _pad_cols(w, n):
    return jnp.pad(w, ((0, 0),) * (w.ndim - 1) + ((0, n - w.shape[-1]),))


def _pos_table():
    quarter = D_MODEL // 4
    freqs = jnp.exp(-math.log(10000.0) * jnp.arange(quarter, dtype=f32) / quarter)
    e = jnp.arange(GRID_W, dtype=f32)[:, None] * freqs
    return jnp.concatenate([jnp.sin(e), jnp.cos(e)], axis=-1)


def _layer(path, x, h, layer, rows, mod3, lp, ffn, s0_gdn, h0_lru, emit_state, next_norm_g):
    proj = _matmul(h, lp['w_in'], 1024, 896, f32)
    qkv = _qkv_prep(path, proj, lp['gdn_conv_w'])
    outs_f = _gdn_scan(path, qkv, proj, lp['gdn_prm'], s0_gdn, layer, False, emit_state)
    outs_b = _gdn_scan(path, qkv, proj, lp['gdn_prm'], s0_gdn, layer, True, emit_state,
                       combine_with=outs_f[0], norm_g=lp['gdn_norm_g'])
    o_mix = outs_b[0]
    y_lru, h_fin = _lru(path, proj, lp, h0_lru, layer, emit_state)
    h_next = None
    if ffn['kind'] == 'dense':
        x, h2 = _out_proj(path, o_mix, y_lru, lp['w_out'], x, mod3, layer, rows, lp['norm2_g'])
        act = _gate_up(h2, ffn['w1'], ffn['w3'], 1024, 512)
        outs = _down_proj(path, act, ffn['w2'], x, mod3, layer, rows, 512, ffn['w2'].shape[0] // 4, next_norm_g)
        x = outs[0]
        h_next = outs[1] if next_norm_g is not None else None
    else:
        x, h2, route = _out_proj(path, o_mix, y_lru, lp['w_out'], x, mod3, layer, rows, lp['norm2_g'],
                                 router=(ffn['wr'], ffn['br']), h_dtype=f32)
        tok_of_row, gate_of_row, expert_of_tile, n_used, pos = _moe_route(route, path.tokens)
        h_rows = _row_gather(h2, tok_of_row, n_used)
        act = _moe_gate_up(h_rows, ffn['w1'], ffn['w3'], gate_of_row, expert_of_tile, n_used,
                           ffn['w1'].shape[2] // 2)
        y_rows = _moe_down(act, ffn['w2'], expert_of_tile, n_used)
        x = _moe_combine(path, y_rows, pos, x, mod3, layer, rows, final_g=ffn.get('final_g'))
    s_fin = jnp.stack([outs_f[1], outs_b[1]], axis=1) if emit_state else None
    return x, h_next, s_fin, h_fin


def kernel(x_prompt, x_sample, state_gdn, state_lru, c, c_ctx, w_mod, b_mod, norm1_g, norm2_g, w_in, gdn_conv_w, gdn_A_log, gdn_dt_bias, gdn_norm_g, lru_conv_w, lru_conv_b, lru_wa, lru_ba, lru_wi, lru_bi, lru_lambda, w_out, ffd_w1, ffd_w3, ffd_w2, moe_wr, moe_br, moe_w1, moe_w3, moe_w2, final_g):
    depth = w_mod.shape[0]
    bp, sp_len, d = x_prompt.shape
    bs, ss_len, _ = x_sample.shape
    ctx = _Path(bp, sp_len, 0, False)
    lat = _Path(bs, ss_len, 1, True)

    rows = SUBLANES * ((1 + bs + SUBLANES - 1) // SUBLANES)
    cvec = jnp.zeros((rows, d), f32).at[0].set(c_ctx).at[1:1 + bs].set(c)
    mod3 = _modulation(cvec, w_mod, b_mod).reshape(depth * rows * 6, 1, d)

    pos = _pos_table()
    xp = x_prompt.reshape(ctx.tokens, d)
    xs = x_sample.reshape(lat.tokens, d)
    new_gdn, new_lru = [], []
    hp = hs = None
    for l in range(depth):
        w_in_l = w_in[l].astype(bf16)
        n_ab = 4 * N_HEADS
        w_in_p = jnp.concatenate([w_in_l[:, :P_AB + n_ab], jnp.zeros((d, LANES - n_ab), bf16),
                                  w_in_l[:, P_AB + n_ab:]], axis=1)
        prm = jnp.stack([_pad_cols(gdn_A_log[l].reshape(1, -1), LANES)[0],
                         _pad_cols(gdn_dt_bias[l].reshape(1, -1), LANES)[0]])
        lp = {
            'norm1_g': norm1_g[l], 'norm2_g': norm2_g[l], 'w_in': w_in_p, 'gdn_conv_w': gdn_conv_w[l],
            'gdn_prm': prm, 'gdn_norm_g': gdn_norm_g[l], 'lru_conv_w': lru_conv_w[l], 'lru_conv_b': lru_conv_b[l],
            'lru_wa': lru_wa[l], 'lru_ba': lru_ba[l], 'lru_wi': lru_wi[l], 'lru_bi': lru_bi[l],
            'lru_lambda': lru_lambda[l], 'w_out': w_out[l].astype(bf16),
        }
        j = l // 2
        if l % 2 == 0:
            f = ffd_w1.shape[2]
            fp = 512 * ((f + 511) // 512)
            ffn = {'kind': 'dense', 'w1': _pad_cols(ffd_w1[j].astype(bf16), fp),
                   'w3': _pad_cols(ffd_w3[j].astype(bf16), fp),
                   'w2': jnp.pad(ffd_w2[j].astype(bf16), ((0, fp - f), (0, 0)))}
        else:
            ffn = {'kind': 'moe', 'wr': _pad_cols(moe_wr[j], LANES), 'br': _pad_cols(moe_br[j].reshape(1, -1), LANES),
                   'w1': moe_w1[j].astype(bf16), 'w3': moe_w3[j].astype(bf16), 'w2': moe_w2[j].astype(bf16)}
            if l == depth - 1:
                ffn['final_g'] = final_g
        next_g = norm1_g[l + 1] if l + 1 < depth else None
        if hp is None:
            (hp,) = _norm_mod(ctx, xp, lp['norm1_g'], mod3, l, rows, 1, 0)
        if hs is None and l == 0:
            hs, xs = _norm_mod(lat, xs, lp['norm1_g'], mod3, l, rows, 1, 0, pos=pos)
        elif hs is None:
            (hs,) = _norm_mod(lat, xs, lp['norm1_g'], mod3, l, rows, 1, 0)
        xp, hp, s_g, s_l = _layer(ctx, xp, hp, l, rows, mod3, lp, ffn, None, None, True, next_g)
        new_gdn.append(s_g)
        new_lru.append(s_l)
        xs, hs, _, _ = _layer(lat, xs, hs, l, rows, mod3, lp, ffn, state_gdn, state_lru, False, next_g)
    if depth % 2 == 1:
        xp, xs = _final_norm(xp, final_g), _final_norm(xs, final_g)
    y_prompt = xp.reshape(bp, sp_len, d)
    y_sample = xs.reshape(bs, ss_len, d)
    return (y_prompt, y_sample, jnp.stack(new_gdn, axis=1), jnp.stack(new_lru, axis=1))
```

```python
import functools
import math

import jax
import jax.numpy as jnp
from jax import lax
from jax.experimental import pallas as pl
from jax.experimental.pallas import tpu as pltpu

f32 = jnp.float32
bf16 = jnp.bfloat16

D_MODEL = 2048
GRID_W = 64
N_HEADS = 8
HEAD_D = 128
GDN_W = N_HEADS * HEAD_D
LRU_W = 1024
LRU_BLOCKS = 8
LRU_BW = LRU_W // LRU_BLOCKS
CONV_W = 4
CONV_LEFT = 2
CHUNK = 64
RG_C = 8.0
N_EXPERTS = 8
EPS = 1e-6
LANES = 128
SUBLANES = 8

P_Q = 0
P_K = P_Q + GDN_W
P_V = P_K + GDN_W
P_Z = P_V + GDN_W
P_AB = P_Z + GDN_W
P_LX = P_AB + LANES
P_LG = P_LX + LRU_W
P_COLS = P_LG + LRU_W

SEQ_TILE = 256
VMEM_LIMIT = 48 * 1024 * 1024
VMEM_LIMIT_BIG = 56 * 1024 * 1024


def _cparams(sem, big_tiles=False):
    return pltpu.CompilerParams(dimension_semantics=sem,
                                vmem_limit_bytes=VMEM_LIMIT_BIG if big_tiles else VMEM_LIMIT)


def _dot(a, b):
    return jnp.dot(a, b, preferred_element_type=f32)


def _sigmoid(x):
    return 0.5 * jnp.tanh(0.5 * x) + 0.5


def _silu(x):
    return x * _sigmoid(x)


def _dot_nt(a, b, precision=None):
    return lax.dot_general(a, b, (((1,), (1,)), ((), ())), precision=precision, preferred_element_type=f32)


def _dot_tn(a, b):
    return lax.dot_general(a, b, (((0,), (0,)), ((), ())), preferred_element_type=f32)


def _mod_kernel(c_ref, w_ref, b_ref, o_ref):
    a = _silu(c_ref[...]).astype(bf16)
    o_ref[...] = _dot(a, w_ref[...].astype(bf16)) + b_ref[...]


def _modulation(cvec, w_mod, b_mod):
    depth, d, n = w_mod.shape
    tn = 1024
    return pl.pallas_call(
        _mod_kernel,
        grid=(depth, n // tn),
        in_specs=[
            pl.BlockSpec((cvec.shape[0], d), lambda l, j: (0, 0)),
            pl.BlockSpec((None, d, tn), lambda l, j: (l, 0, j)),
            pl.BlockSpec((None, 1, tn), lambda l, j: (l, 0, j)),
        ],
        out_specs=pl.BlockSpec((None, cvec.shape[0], tn), lambda l, j: (l, 0, j)),
        out_shape=jax.ShapeDtypeStruct((depth, cvec.shape[0], n), f32),
        compiler_params=_cparams(("arbitrary", "arbitrary")),
        name="modulation",
    )(cvec, w_mod, b_mod.reshape(depth, 1, n))


class _Path:
    def __init__(self, nseq, seq_len, mod_row0, per_seq_mod):
        self.nseq = nseq
        self.seq_len = seq_len
        self.tokens = nseq * seq_len
        self.mod_row0 = mod_row0
        self.per_seq_mod = per_seq_mod

    def mod_row(self, tile_idx, tile):
        if self.per_seq_mod:
            return self.mod_row0 + tile_idx // (self.seq_len // tile)
        return self.mod_row0


def _mod_spec(path, layer, rows, part, tile, width, col_of=None, grid_rank=1):
    def imap(*idx):
        i = idx[0]
        j = idx[col_of] if col_of is not None else 0
        return ((layer * rows + path.mod_row(i, tile)) * 6 + part, 0, j)
    return pl.BlockSpec((None, 1, width), imap)


def _rms(x, g):
    return x * lax.rsqrt(jnp.mean(x * x, axis=-1, keepdims=True) + EPS) * g


R_IDX, R_GATE = 0, 2


def _emit_modulated_norm(x, g_ref, scale_ref, shift_ref, h_ref, wr_ref=None, br_ref=None, route_ref=None):
    h = _rms(x, g_ref[...]) * (1.0 + scale_ref[...]) + shift_ref[...]
    h_ref[...] = h.astype(h_ref.dtype)
    if route_ref is None:
        return
    logits = _dot(h.astype(bf16), wr_ref[...].astype(bf16)) + br_ref[...]
    lane = lax.broadcasted_iota(jnp.int32, logits.shape, 1)
    valid = lane < N_EXPERTS
    logits = jnp.where(valid, logits, -jnp.inf)
    e = jnp.exp(logits - jnp.max(logits, axis=-1, keepdims=True))
    p = e / jnp.sum(e, axis=-1, keepdims=True)
    p = jnp.where(valid, p, -1.0)
    m1 = jnp.max(p, axis=-1, keepdims=True)
    i1 = jnp.min(jnp.where(p == m1, lane, LANES), axis=-1, keepdims=True)
    p2 = jnp.where(lane == i1, -1.0, p)
    m2 = jnp.max(p2, axis=-1, keepdims=True)
    i2 = jnp.min(jnp.where(p2 == m2, lane, LANES), axis=-1, keepdims=True)
    den = m1 + m2
    route_ref[...] = (jnp.where(lane == R_IDX, i1.astype(f32), 0.0) + jnp.where(lane == R_IDX + 1, i2.astype(f32), 0.0)
                      + jnp.where(lane == R_GATE, m1 / den, 0.0) + jnp.where(lane == R_GATE + 1, m2 / den, 0.0))


def _norm_io(path, g, mod3, layer, rows, part_scale, part_shift, tile, router, h_dtype):
    d = D_MODEL
    in_specs = [pl.BlockSpec((1, d), lambda *idx: (0, 0)),
                _mod_spec(path, layer, rows, part_scale, tile, d),
                _mod_spec(path, layer, rows, part_shift, tile, d)]
    args = [g.reshape(1, d), mod3, mod3]
    out_specs = [pl.BlockSpec((tile, d), lambda *idx: (idx[0], 0))]
    out_shape = [jax.ShapeDtypeStruct((path.tokens, d), h_dtype)]
    if router is not None:
        in_specs += [pl.BlockSpec((d, LANES), lambda *idx: (0, 0)), pl.BlockSpec((1, LANES), lambda *idx: (0, 0))]
        args += list(router)
        out_specs.append(pl.BlockSpec((tile, LANES), lambda *idx: (idx[0], 0)))
        out_shape.append(jax.ShapeDtypeStruct((path.tokens, LANES), f32))
    return in_specs, args, out_specs, out_shape


def _norm_mod_kernel(*refs, with_pos, with_router, tile, tiles_per_seq):
    it = iter(refs)
    x_ref, g_ref, scale_ref, shift_ref = next(it), next(it), next(it), next(it)
    pos_ref = next(it) if with_pos else None
    wr_ref, br_ref = (next(it), next(it)) if with_router else (None, None)
    h_ref = next(it)
    xo_ref = next(it) if with_pos else None
    gates_ref = next(it) if with_router else None

    x = x_ref[...]
    if with_pos:
        rows_per_tile = tile // GRID_W
        row0 = (pl.program_id(0) % tiles_per_seq) * rows_per_tile
        half = D_MODEL // 2
        col_part = pos_ref[...]
        pieces = []
        for r in range(rows_per_tile):
            row_part = jnp.broadcast_to(pos_ref[pl.ds(row0 + r, 1), :], (GRID_W, half))
            pieces.append(jnp.concatenate([row_part, col_part], axis=1))
        x = x + jnp.concatenate(pieces, axis=0)
        xo_ref[...] = x
    _emit_modulated_norm(x, g_ref, scale_ref, shift_ref, h_ref, wr_ref, br_ref, gates_ref)


def _norm_mod(path, x, g, mod3, layer, rows, part_scale, part_shift, pos=None, router=None, h_dtype=bf16):
    tile = SEQ_TILE
    n = path.tokens // tile
    d = D_MODEL
    in_specs = [
        pl.BlockSpec((tile, d), lambda i: (i, 0)),
        pl.BlockSpec((1, d), lambda i: (0, 0)),
        _mod_spec(path, layer, rows, part_scale, tile, d),
        _mod_spec(path, layer, rows, part_shift, tile, d),
    ]
    args = [x, g.reshape(1, d), mod3, mod3]
    out_specs = [pl.BlockSpec((tile, d), lambda i: (i, 0))]
    out_shape = [jax.ShapeDtypeStruct((path.tokens, d), h_dtype)]
    if pos is not None:
        in_specs.append(pl.BlockSpec(pos.shape, lambda i: (0, 0)))
        args.append(pos)
        out_specs.append(pl.BlockSpec((tile, d), lambda i: (i, 0)))
        out_shape.append(jax.ShapeDtypeStruct((path.tokens, d), f32))
    if router is not None:
        wr, br = router
        in_specs += [pl.BlockSpec((d, LANES), lambda i: (0, 0)), pl.BlockSpec((1, LANES), lambda i: (0, 0))]
        args += [wr, br]
        out_specs.append(pl.BlockSpec((tile, LANES), lambda i: (i, 0)))
        out_shape.append(jax.ShapeDtypeStruct((path.tokens, LANES), f32))
    kern = functools.partial(_norm_mod_kernel, with_pos=pos is not None, with_router=router is not None,
                             tile=tile, tiles_per_seq=path.seq_len // tile)
    return pl.pallas_call(kern, grid=(n,), in_specs=in_specs, out_specs=out_specs, out_shape=out_shape,
                          compiler_params=_cparams(("arbitrary",)), name="norm_mod")(*args)


def _final_norm_kernel(x_ref, g_ref, o_ref):
    o_ref[...] = _rms(x_ref[...], g_ref[...])


def _final_norm(x, g):
    t, d = x.shape
    tile = 512
    return pl.pallas_call(
        _final_norm_kernel, grid=(t // tile,),
        in_specs=[pl.BlockSpec((tile, d), lambda i: (i, 0)), pl.BlockSpec((1, d), lambda i: (0, 0))],
        out_specs=pl.BlockSpec((tile, d), lambda i: (i, 0)),
        out_shape=jax.ShapeDtypeStruct((t, d), f32),
        compiler_params=_cparams(("arbitrary",)), name="final_norm")(x, g.reshape(1, d))


def _mm_kernel(a_ref, w_ref, o_ref):
    o_ref[...] = _dot(a_ref[...], w_ref[...]).astype(o_ref.dtype)


def _matmul(a, w, tm, tn, out_dtype):
    m, k = a.shape
    n = w.shape[1]
    return pl.pallas_call(
        _mm_kernel, grid=(m // tm, n // tn),
        in_specs=[pl.BlockSpec((tm, k), lambda i, j: (i, 0)), pl.BlockSpec((k, tn), lambda i, j: (0, j))],
        out_specs=pl.BlockSpec((tm, tn), lambda i, j: (i, j)),
        out_shape=jax.ShapeDtypeStruct((m, n), out_dtype),
        compiler_params=_cparams(("arbitrary", "arbitrary"), big_tiles=True), name="matmul")(a, w)


def _out_proj_kernel(*refs, with_router):
    a1_ref, a2_ref, w1_ref, w2_ref, x_ref, gate_ref, g_ref, scale_ref, shift_ref = refs[:9]
    wr_ref, br_ref = refs[9:11] if with_router else (None, None)
    outs = refs[11:] if with_router else refs[9:]
    xo_ref, h_ref = outs[:2]
    route_ref = outs[2] if with_router else None
    acc = _dot(a1_ref[...], w1_ref[...]) + _dot(a2_ref[...], w2_ref[...])
    x = x_ref[...] + gate_ref[...] * acc
    xo_ref[...] = x
    _emit_modulated_norm(x, g_ref, scale_ref, shift_ref, h_ref, wr_ref, br_ref, route_ref)


def _out_proj(path, a1, a2, w, x, mod3, layer, rows, norm_g, router=None, h_dtype=bf16):
    tm = 512
    m, d = x.shape
    k1, k2 = a1.shape[1], a2.shape[1]
    n_in, n_args, n_out, n_shape = _norm_io(path, norm_g, mod3, layer, rows, 4, 3, tm, router, h_dtype)
    once = pl.Buffered(1)
    return pl.pallas_call(
        functools.partial(_out_proj_kernel, with_router=router is not None), grid=(m // tm,),
        in_specs=[
            pl.BlockSpec((tm, k1), lambda i: (i, 0)),
            pl.BlockSpec((tm, k2), lambda i: (i, 0)),
            pl.BlockSpec((k1, d), lambda i: (0, 0), pipeline_mode=once),
            pl.BlockSpec((k2, d), lambda i: (k1 // k2, 0), pipeline_mode=once),
            pl.BlockSpec((tm, d), lambda i: (i, 0)),
            _mod_spec(path, layer, rows, 2, tm, d),
        ] + n_in,
        out_specs=[pl.BlockSpec((tm, d), lambda i: (i, 0))] + n_out,
        out_shape=[jax.ShapeDtypeStruct((m, d), f32)] + n_shape,
        compiler_params=_cparams(("arbitrary",)), name="out_proj")(a1, a2, w, w, x, mod3, *n_args)


def _gate_up_kernel(h_ref, w1_ref, w3_ref, o_ref):
    h = h_ref[...]
    a = _dot(h, w1_ref[...])
    b = _dot(h, w3_ref[...])
    o_ref[...] = (_silu(a) * b).astype(o_ref.dtype)


def _gate_up(h, w1, w3, tm, tn):
    m, k = h.shape
    n = w1.shape[1]
    return pl.pallas_call(
        _gate_up_kernel, grid=(m // tm, n // tn),
        in_specs=[pl.BlockSpec((tm, k), lambda i, j: (i, 0)),
                  pl.BlockSpec((k, tn), lambda i, j: (0, j)),
                  pl.BlockSpec((k, tn), lambda i, j: (0, j))],
        out_specs=pl.BlockSpec((tm, tn), lambda i, j: (i, j)),
        out_shape=jax.ShapeDtypeStruct((m, n), bf16),
        compiler_params=_cparams(("arbitrary", "arbitrary"), big_tiles=True), name="gate_up")(h, w1, w3)


MOE_TILE = 512
DMA_UNROLL = 8
GATHER_ROWS = 2 * MOE_TILE


def _moe_route(route, n_tok):
    tm = MOE_TILE
    n_rows = 2 * n_tok + N_EXPERTS * tm
    e_idx = jnp.clip(route[:, R_IDX:R_IDX + 2].astype(jnp.int32), 0, N_EXPERTS - 1).reshape(-1)
    gate = route[:, R_GATE:R_GATE + 2].reshape(-1)
    one_hot = (e_idx[:, None] == jnp.arange(N_EXPERTS, dtype=jnp.int32)[None, :]).astype(jnp.int32)
    csum = jnp.cumsum(one_hot, axis=0)
    rank = jnp.take_along_axis(csum, e_idx[:, None], axis=1)[:, 0] - 1
    padded = ((csum[-1] + tm - 1) // tm) * tm
    ends = jnp.cumsum(padded)
    pos = (ends - padded)[e_idx] + rank
    tok = (jnp.arange(2 * n_tok, dtype=jnp.int32) // 2).astype(f32)
    table = jnp.zeros((n_rows, 2), f32).at[pos].set(jnp.stack([tok, gate], axis=1))
    tok_of_row = table[:, 0].astype(jnp.int32)
    gate_of_row = table[:, 1]
    tile_start = jnp.arange(n_rows // tm, dtype=jnp.int32) * tm
    expert_of_tile = jnp.minimum(jnp.sum(tile_start[:, None] >= ends[None, :], axis=1), N_EXPERTS - 1)
    n_used = (ends[-1:] // tm).astype(jnp.int32)
    return tok_of_row, gate_of_row.reshape(n_rows, 1), expert_of_tile.astype(jnp.int32), n_used, pos.astype(jnp.int32)


def _row_copy(src_hbm, src_row, dst_vmem, dst_row, sem):
    return pltpu.make_async_copy(src_hbm.at[pl.ds(src_row, 1), :], dst_vmem.at[pl.ds(dst_row, 1), :], sem)


def _row_gather_kernel(tok_ref, nused_ref, h_hbm, o_ref, sem):
    i = pl.program_id(0)
    tg = o_ref.shape[0]
    used = i * (tg // MOE_TILE) < nused_ref[0]

    @pl.when(used)
    def _():
        def issue(r, c):
            _row_copy(h_hbm, tok_ref[i * tg + r], o_ref, r, sem).start()
            return c
        lax.fori_loop(0, tg, issue, 0, unroll=DMA_UNROLL)
        pltpu.make_async_copy(h_hbm.at[pl.ds(0, tg), :], o_ref, sem).wait()

    @pl.when(jnp.logical_not(used))
    def _():
        o_ref[...] = jnp.zeros_like(o_ref)


def _row_gather(h, tok_of_row, n_used):
    n_rows = tok_of_row.shape[0]
    d = h.shape[1]
    assert n_rows % GATHER_ROWS == 0
    return pl.pallas_call(
        _row_gather_kernel,
        grid_spec=pltpu.PrefetchScalarGridSpec(
            num_scalar_prefetch=2, grid=(n_rows // GATHER_ROWS,),
            in_specs=[pl.BlockSpec(memory_space=pl.ANY)],
            out_specs=pl.BlockSpec((GATHER_ROWS, d), lambda i, tok, nu: (i, 0)),
            scratch_shapes=[pltpu.SemaphoreType.DMA]),
        out_shape=jax.ShapeDtypeStruct((n_rows, d), h.dtype),
        compiler_params=_cparams(("arbitrary",)), name="moe_row_gather")(tok_of_row, n_used, h)


def _moe_gate_up_kernel(eot_ref, nused_ref, x_ref, w1_ref, w3_ref, g_ref, o_ref):
    i = pl.program_id(1)

    @pl.when(i < nused_ref[0])
    def _():
        x = x_ref[...].astype(bf16)
        a = _dot(x, w1_ref[...])
        b = _dot(x, w3_ref[...])
        o_ref[...] = (g_ref[...] * (_silu(a) * b)).astype(o_ref.dtype)

    @pl.when(i >= nused_ref[0])
    def _():
        o_ref[...] = jnp.zeros_like(o_ref)


def _moe_gate_up(rows, w1, w3, gate_of_row, expert_of_tile, n_used, tn):
    tm = MOE_TILE
    n_rows, k = rows.shape
    f = w1.shape[2]
    return pl.pallas_call(
        _moe_gate_up_kernel,
        grid_spec=pltpu.PrefetchScalarGridSpec(
            num_scalar_prefetch=2, grid=(f // tn, n_rows // tm),
            in_specs=[pl.BlockSpec((tm, k), lambda j, i, eot, nu: (i, 0)),
                      pl.BlockSpec((None, k, tn), lambda j, i, eot, nu: (eot[i], 0, j)),
                      pl.BlockSpec((None, k, tn), lambda j, i, eot, nu: (eot[i], 0, j)),
                      pl.BlockSpec((tm, 1), lambda j, i, eot, nu: (i, 0))],
            out_specs=pl.BlockSpec((tm, tn), lambda j, i, eot, nu: (i, j))),
        out_shape=jax.ShapeDtypeStruct((n_rows, f), bf16),
        compiler_params=_cparams(("arbitrary", "arbitrary")), name="moe_gate_up")(
            expert_of_tile, n_used, rows, w1, w3, gate_of_row)


def _moe_down_kernel(eot_ref, nused_ref, a_ref, w_ref, o_ref):
    i = pl.program_id(0)

    @pl.when(i < nused_ref[0])
    def _():
        o_ref[...] = _dot(a_ref[...], w_ref[...])

    @pl.when(i >= nused_ref[0])
    def _():
        o_ref[...] = jnp.zeros_like(o_ref)


def _moe_down(act, w2, expert_of_tile, n_used):
    tm = MOE_TILE
    n_rows, f = act.shape
    d = w2.shape[2]
    return pl.pallas_call(
        _moe_down_kernel,
        grid_spec=pltpu.PrefetchScalarGridSpec(
            num_scalar_prefetch=2, grid=(n_rows // tm,),
            in_specs=[pl.BlockSpec((tm, f), lambda i, eot, nu: (i, 0)),
                      pl.BlockSpec((None, f, d), lambda i, eot, nu: (eot[i], 0, 0))],
            out_specs=pl.BlockSpec((tm, d), lambda i, eot, nu: (i, 0))),
        out_shape=jax.ShapeDtypeStruct((n_rows, d), f32),
        compiler_params=_cparams(("arbitrary",)), name="moe_down")(expert_of_tile, n_used, act, w2)


def _moe_combine_kernel(pos_ref, y_hbm, x_ref, gate_ref, *rest):
    fg_ref = rest[0] if len(rest) == 4 else None
    o_ref, buf, sem = rest[-3:]
    i = pl.program_id(0)
    tm = o_ref.shape[0]

    def issue_tile(t, slot):
        def issue(r, c):
            a = 2 * (t * tm + r)
            _row_copy(y_hbm, pos_ref[a], buf.at[slot, 0], r, sem.at[slot]).start()
            _row_copy(y_hbm, pos_ref[a + 1], buf.at[slot, 1], r, sem.at[slot]).start()
            return c
        lax.fori_loop(0, tm, issue, 0, unroll=DMA_UNROLL)

    @pl.when(i == 0)
    def _():
        issue_tile(0, 0)

    @pl.when(i + 1 < pl.num_programs(0))
    def _():
        issue_tile(i + 1, (i + 1) % 2)

    slot = i % 2
    for b in range(2):
        pltpu.make_async_copy(y_hbm.at[pl.ds(0, tm), :], buf.at[slot, b], sem.at[slot]).wait()
    y = x_ref[...] + gate_ref[...] * (buf[slot, 0] + buf[slot, 1])
    o_ref[...] = y if fg_ref is None else _rms(y, fg_ref[...])


def _moe_combine(path, y_rows, pos, x, mod3, layer, rows, final_g=None):
    tm = SEQ_TILE
    m, d = x.shape
    in_specs = [pl.BlockSpec(memory_space=pl.ANY),
                pl.BlockSpec((tm, d), lambda i, pos: (i, 0)),
                pl.BlockSpec((None, 1, d), lambda i, pos: ((layer * rows + path.mod_row(i, tm)) * 6 + 5, 0, 0))]
    args = [pos, y_rows, x, mod3]
    if final_g is not None:
        in_specs.append(pl.BlockSpec((1, d), lambda i, pos: (0, 0)))
        args.append(final_g.reshape(1, d))
    return pl.pallas_call(
        _moe_combine_kernel,
        grid_spec=pltpu.PrefetchScalarGridSpec(
            num_scalar_prefetch=1, grid=(m // tm,), in_specs=in_specs,
            out_specs=pl.BlockSpec((tm, d), lambda i, pos: (i, 0)),
            scratch_shapes=[pltpu.VMEM((2, 2, tm, d), f32), pltpu.SemaphoreType.DMA((2,))]),
        out_shape=jax.ShapeDtypeStruct((m, d), f32),
        compiler_params=_cparams(("arbitrary",)), name="moe_combine")(*args)


def _down_kernel(*refs, nk, with_norm):
    a_ref, w_ref, x_ref, gate_ref = refs[:4]
    g_ref, scale_ref, shift_ref = refs[4:7] if with_norm else (None, None, None)
    outs = refs[7:] if with_norm else refs[4:]
    o_ref = outs[0]
    h_ref = outs[1] if with_norm else None
    acc_ref = outs[-1]
    kk = pl.program_id(1)

    @pl.when(kk == 0)
    def _():
        acc_ref[...] = jnp.zeros_like(acc_ref)

    acc_ref[...] += _dot(a_ref[...], w_ref[...])

    @pl.when(kk == nk - 1)
    def _():
        x = x_ref[...] + gate_ref[...] * acc_ref[...]
        o_ref[...] = x
        if with_norm:
            _emit_modulated_norm(x, g_ref, scale_ref, shift_ref, h_ref)


def _down_proj(path, act, w2, x, mod3, layer, rows, tm, tk, next_norm_g=None):
    m, d = x.shape
    kdim = act.shape[1]
    nk = kdim // tk
    n_in, n_args, n_out, n_shape = ([], [], [], [])
    if next_norm_g is not None:
        n_in, n_args, n_out, n_shape = _norm_io(path, next_norm_g, mod3, layer + 1, rows, 1, 0, tm, None, bf16)
    return pl.pallas_call(
        functools.partial(_down_kernel, nk=nk, with_norm=next_norm_g is not None), grid=(m // tm, nk),
        in_specs=[pl.BlockSpec((tm, tk), lambda i, k: (i, k)),
                  pl.BlockSpec((tk, d), lambda i, k: (k, 0)),
                  pl.BlockSpec((tm, d), lambda i, k: (i, 0)),
                  _mod_spec(path, layer, rows, 5, tm, d)] + n_in,
        out_specs=[pl.BlockSpec((tm, d), lambda i, k: (i, 0))] + n_out,
        out_shape=[jax.ShapeDtypeStruct((m, d), f32)] + n_shape,
        scratch_shapes=[pltpu.VMEM((tm, d), f32)],
        compiler_params=_cparams(("arbitrary", "arbitrary"), big_tiles=True), name="down_proj")(act, w2, x, mod3, *n_args)


def _conv_window(win, w):
    n = win.shape[0] - 2 * SUBLANES
    acc = None
    for j in range(CONV_W):
        off = SUBLANES + j - CONV_LEFT
        term = win[off:off + n] * w[j:j + 1]
        acc = term if acc is None else acc + term
    return acc


def _qkv_prep_kernel(prev_ref, cur_ref, next_ref, w_ref, o_ref, *, tiles_per_seq):
    t = pl.program_id(0) % tiles_per_seq
    kind = pl.program_id(1)
    has_prev = (t > 0).astype(f32)
    has_next = (t < tiles_per_seq - 1).astype(f32)
    scale = jnp.where(kind == 0, HEAD_D ** -0.5, 1.0).astype(f32)
    tile = cur_ref.shape[0]
    for c in range(tile // CHUNK):
        r0 = c * CHUNK
        for hd in range(N_HEADS):
            cols = slice(hd * HEAD_D, (hd + 1) * HEAD_D)
            top = prev_ref[:, cols] * has_prev if c == 0 else cur_ref[r0 - SUBLANES:r0, cols]
            bot = (next_ref[:, cols] * has_next if r0 + CHUNK == tile
                   else cur_ref[r0 + CHUNK:r0 + CHUNK + SUBLANES, cols])
            win = jnp.concatenate([top, cur_ref[r0:r0 + CHUNK, cols], bot], axis=0)
            y = _silu(_conv_window(win, w_ref[:, cols]))
            nrm = lax.rsqrt(jnp.sum(y * y, axis=-1, keepdims=True) + EPS) * scale
            y = y * jnp.where(kind < 2, nrm, 1.0)
            o_ref[r0:r0 + CHUNK, cols] = y


def _qkv_prep(path, proj, conv_w):
    tile = SEQ_TILE
    n = path.tokens // tile
    tps = path.seq_len // tile
    rb = tile // SUBLANES
    last = path.tokens // SUBLANES - 1
    return pl.pallas_call(
        functools.partial(_qkv_prep_kernel, tiles_per_seq=tps), grid=(n, 3),
        in_specs=[
            pl.BlockSpec((SUBLANES, GDN_W), lambda i, j: (jnp.maximum(i * rb - 1, 0), j)),
            pl.BlockSpec((tile, GDN_W), lambda i, j: (i, j)),
            pl.BlockSpec((SUBLANES, GDN_W), lambda i, j: (jnp.minimum((i + 1) * rb, last), j)),
            pl.BlockSpec((CONV_W, GDN_W), lambda i, j: (0, j)),
        ],
        out_specs=pl.BlockSpec((tile, GDN_W), lambda i, j: (i, j)),
        out_shape=jax.ShapeDtypeStruct((path.tokens, 3 * GDN_W), f32),
        compiler_params=_cparams(("arbitrary", "arbitrary")), name="qkv_prep")(proj, proj, proj, conv_w)


INV_BLOCK = 16


def _bdot(a, b):
    return _dot(a.astype(bf16), b.astype(bf16))


def _unit_tri_inverse(a_list, eye, diag_blk):
    p = [-jnp.where(diag_blk, a, 0.0) for a in a_list]
    x = [eye + pi for pi in p]
    span = 2
    while span < INV_BLOCK:
        p = [_bdot(pi, pi) for pi in p]
        x = [xi + _bdot(xi, pi) for xi, pi in zip(x, p)]
        span *= 2
    ia = [(eye + a).astype(bf16) for a in a_list]
    span = INV_BLOCK
    while span < a_list[0].shape[0]:
        r = [eye - _dot(iai, xi.astype(bf16)) for iai, xi in zip(ia, x)]
        x = [xi + _bdot(xi, ri) for xi, ri in zip(x, r)]
        span *= 2
    return x


def _gdn_kernel(*refs, reverse, tiles_per_seq, zero_init, emit_state, combine):
    it = iter(refs)
    q_ref, k_ref, v_ref, ab_ref, prm_ref = next(it), next(it), next(it), next(it), next(it)
    s0_ref = None if zero_init else next(it)
    other_ref, z_ref, gn_ref = (next(it), next(it), next(it)) if combine else (None, None, None)
    o_ref = next(it)
    sfin_ref = next(it) if emit_state else None
    s_scr = next(it)

    t = pl.program_id(1)
    nchunks = q_ref.shape[0] // CHUNK

    @pl.when(t == 0)
    def _():
        if zero_init:
            s_scr[...] = jnp.zeros_like(s_scr)
        else:
            s_scr[...] = s0_ref[...]

    ri = lax.broadcasted_iota(jnp.int32, (CHUNK, CHUNK), 0)
    ci = lax.broadcasted_iota(jnp.int32, (CHUNK, CHUNK), 1)
    incl = (ri <= ci) if reverse else (ri >= ci)
    strict = (ri < ci) if reverse else (ri > ci)
    eye = (ri == ci).astype(f32)
    diag_blk = (ri // INV_BLOCK) == (ci // INV_BLOCK)
    tri = incl.astype(f32)
    last_row = 0 if reverse else CHUNK - 1
    dir_off = N_HEADS if reverse else 0
    neg_a = -jnp.exp(prm_ref[0:1, :])
    dt_bias = prm_ref[1:2, :]

    heads = range(N_HEADS)
    cols = [slice(hd * HEAD_D, (hd + 1) * HEAD_D) for hd in heads]
    ga = [dir_off + hd for hd in heads]

    def prepare(chunks):
        gate = []
        for cc in chunks:
            rows = slice(cc * CHUNK, (cc + 1) * CHUNK)
            ab = ab_ref[rows, :]
            g = neg_a * jax.nn.softplus(ab + dt_bias)
            beta = _sigmoid(ab)
            gc = jnp.dot(tri, g, precision=lax.Precision.HIGHEST, preferred_element_type=f32)
            g_last = gc[last_row:last_row + 1, :]
            gate.append({'beta': beta, 'gc': gc, 'gc_t': gc.T, 'e_gc': jnp.exp(gc),
                         'e_rest': jnp.exp(g_last - gc), 'e_last': jnp.exp(g_last)})
        probs = [(i, h) for i in range(len(chunks)) for h in heads]

        def rows_of(i):
            return slice(chunks[i] * CHUNK, (chunks[i] + 1) * CHUNK)

        def col(i, name, lane):
            return gate[i][name][:, lane:lane + 1]

        b_col = [col(i, 'beta', 2 * N_HEADS + ga[h]) for i, h in probs]
        eg_col = [col(i, 'e_gc', ga[h]) for i, h in probs]
        qh = [q_ref[rows_of(i), cols[h]] for i, h in probs]
        kh = [k_ref[rows_of(i), cols[h]] for i, h in probs]
        vh = [v_ref[rows_of(i), cols[h]] for i, h in probs]
        n = range(len(probs))
        gram = [_dot_nt(jnp.concatenate([qh[p], kh[p]], axis=0).astype(bf16), kh[p].astype(bf16)) for p in n]
        decay = [jnp.exp(jnp.where(incl, col(i, 'gc', ga[h]) - gate[i]['gc_t'][ga[h]:ga[h] + 1, :], -jnp.inf))
                 for i, h in probs]
        a_low = [jnp.where(strict, gram[p][CHUNK:] * b_col[p] * decay[p], 0.0) for p in n]
        qk = [(gram[p][:CHUNK] * decay[p]).astype(bf16) for p in n]
        t_inv = _unit_tri_inverse(a_low, eye, diag_blk)
        rhs = [jnp.concatenate([vh[p] * b_col[p], kh[p] * b_col[p] * eg_col[p]], axis=1) for p in n]
        uw = [_bdot(t_inv[p], rhs[p]) for p in n]
        w_qg = [jnp.concatenate([uw[p][:, HEAD_D:], qh[p] * eg_col[p]], axis=0).astype(bf16) for p in n]
        kd = [(kh[p] * col(i, 'e_rest', ga[h])).astype(bf16) for p, (i, h) in enumerate(probs)]
        e_last = [col(i, 'e_last', ga[h]) for i, h in probs]

        def per_chunk(lst):
            return [lst[i * N_HEADS:(i + 1) * N_HEADS] for i in range(len(chunks))]
        return {'u': per_chunk([x[:, :HEAD_D] for x in uw]), 'w_qg': per_chunk(w_qg), 'qk': per_chunk(qk),
                'kd': per_chunk(kd), 'e_last': per_chunk(e_last)}

    order = list(range(nchunks))[::-1] if reverse else list(range(nchunks))
    pre = prepare(order)
    state = [s_scr[h] for h in heads]
    for i, cc in enumerate(order):
        rows = slice(cc * CHUNK, (cc + 1) * CHUNK)
        s_b = [x.astype(bf16) for x in state]
        ws = [_dot(pre['w_qg'][i][h], s_b[h]) for h in heads]
        v_new = [(pre['u'][i][h] - ws[h][:CHUNK]).astype(bf16) for h in heads]
        for h in heads:
            o_h = ws[h][CHUNK:] + _dot(pre['qk'][i][h], v_new[h])
            if combine:
                y = _rms(other_ref[rows, cols[h]] + o_h, gn_ref[...])
                o_h = y * _silu(z_ref[rows, cols[h]])
            o_ref[rows, cols[h]] = o_h.astype(o_ref.dtype)
        state = [state[h] * pre['e_last'][i][h] + _dot_tn(pre['kd'][i][h], v_new[h]) for h in heads]
    for h in heads:
        s_scr[h] = state[h]

    if emit_state:
        @pl.when(t == tiles_per_seq - 1)
        def _():
            sfin_ref[...] = s_scr[...]


def _gdn_scan(path, qkv, proj, prm, s0, layer, reverse, emit_state, combine_with=None, norm_g=None):
    tile = SEQ_TILE
    tps = path.seq_len // tile
    d = 1 if reverse else 0

    def tok(s, t):
        return s * tps + ((tps - 1 - t) if reverse else t)

    in_specs = [
        pl.BlockSpec((tile, GDN_W), lambda s, t: (tok(s, t), 0)),
        pl.BlockSpec((tile, GDN_W), lambda s, t: (tok(s, t), 1)),
        pl.BlockSpec((tile, GDN_W), lambda s, t: (tok(s, t), 2)),
        pl.BlockSpec((tile, LANES), lambda s, t: (tok(s, t), P_AB // LANES)),
        pl.BlockSpec((2, LANES), lambda s, t: (0, 0)),
    ]
    args = [qkv, qkv, qkv, proj, prm]
    if s0 is not None:
        in_specs.append(pl.BlockSpec((None, None, None, N_HEADS, HEAD_D, HEAD_D),
                                     lambda s, t: (s, layer, d, 0, 0, 0)))
        args.append(s0)
    combine = combine_with is not None
    if combine:
        in_specs += [pl.BlockSpec((tile, GDN_W), lambda s, t: (tok(s, t), 0)),
                     pl.BlockSpec((tile, GDN_W), lambda s, t: (tok(s, t), P_Z // GDN_W)),
                     pl.BlockSpec((1, HEAD_D), lambda s, t: (0, 0))]
        args += [combine_with, proj, norm_g.reshape(1, HEAD_D)]
    out_specs = [pl.BlockSpec((tile, GDN_W), lambda s, t: (tok(s, t), 0))]
    out_shape = [jax.ShapeDtypeStruct((path.tokens, GDN_W), bf16 if combine else f32)]
    if emit_state:
        out_specs.append(pl.BlockSpec((None, N_HEADS, HEAD_D, HEAD_D), lambda s, t: (s, 0, 0, 0)))
        out_shape.append(jax.ShapeDtypeStruct((path.nseq, N_HEADS, HEAD_D, HEAD_D), f32))
    kern = functools.partial(_gdn_kernel, reverse=reverse, tiles_per_seq=tps, zero_init=s0 is None,
                             emit_state=emit_state, combine=combine)
    return pl.pallas_call(
        kern, grid=(path.nseq, tps), in_specs=in_specs, out_specs=out_specs, out_shape=out_shape,
        scratch_shapes=[pltpu.VMEM((N_HEADS, HEAD_D, HEAD_D), f32)],
        compiler_params=_cparams(("arbitrary", "arbitrary")), name="gdn_scan")(*args)


def _lru_kernel(*refs, zero_init, emit_state):
    it = iter(refs)
    lx_ref, lg_ref, cw_ref, cb_ref = next(it), next(it), next(it), next(it)
    wa_ref, wi_ref, ba_ref, bi_ref, lam_ref = next(it), next(it), next(it), next(it), next(it)
    h0_ref = None if zero_init else next(it)
    y_ref = next(it)
    hfin_ref = next(it) if emit_state else None
    a_scr, b_scr = next(it), next(it)

    seq = lx_ref.shape[0]
    blk = 256
    nblk = seq // blk
    w_cat = jnp.concatenate([wa_ref[0], wi_ref[0], wa_ref[1], wi_ref[1]], axis=1).astype(bf16)
    bias_cat = jnp.concatenate([ba_ref[0:1], bi_ref[0:1], ba_ref[1:2], bi_ref[1:2]], axis=1)
    sp = jax.nn.softplus(-lam_ref[...])
    cw = cw_ref[...]
    cb = cb_ref[...]

    def gates_body(i, carry):
        r0 = pl.multiple_of(i * blk, blk)
        r_top = pl.multiple_of(jnp.maximum(r0 - SUBLANES, 0), SUBLANES)
        r_bot = pl.multiple_of(jnp.minimum(r0 + blk, seq - SUBLANES), SUBLANES)
        top = lx_ref[pl.ds(r_top, SUBLANES), :] * jnp.where(i > 0, 1.0, 0.0)
        bot = lx_ref[pl.ds(r_bot, SUBLANES), :] * jnp.where(i < nblk - 1, 1.0, 0.0)
        win = jnp.concatenate([top, lx_ref[pl.ds(r0, blk), :], bot], axis=0)
        xl = _conv_window(win, cw) + cb
        pre = _dot(xl.astype(bf16), w_cat) + bias_cat
        for d in range(2):
            r = _sigmoid(pre[:, (2 * d) * LRU_BW:(2 * d + 1) * LRU_BW])
            ig = _sigmoid(pre[:, (2 * d + 1) * LRU_BW:(2 * d + 2) * LRU_BW])
            a = jnp.exp(-RG_C * r * sp[d:d + 1])
            a_scr[d, pl.ds(r0, blk), :] = a
            b_scr[d, pl.ds(r0, blk), :] = jnp.sqrt(1.0 - a * a) * (ig * xl)
        return carry

    lax.fori_loop(0, nblk, gates_body, 0)

    groups = CHUNK // SUBLANES
    row = lax.broadcasted_iota(jnp.int32, (groups, SUBLANES, LRU_BW), 1)
    nchunks = seq // CHUNK

    def chunk_scan(a, b, h_prev, reverse):
        a = a.reshape(groups, SUBLANES, LRU_BW)
        b = b.reshape(groups, SUBLANES, LRU_BW)
        s = 1
        while s < SUBLANES:
            if reverse:
                keep = row < SUBLANES - s
                a_sh = jnp.where(keep, pltpu.roll(a, SUBLANES - s, 1), 1.0)
                b_sh = jnp.where(keep, pltpu.roll(b, SUBLANES - s, 1), 0.0)
            else:
                keep = row >= s
                a_sh = jnp.where(keep, pltpu.roll(a, s, 1), 1.0)
                b_sh = jnp.where(keep, pltpu.roll(b, s, 1), 0.0)
            b = a * b_sh + b
            a = a * a_sh
            s *= 2
        out = [None] * groups
        for g in (range(groups - 1, -1, -1) if reverse else range(groups)):
            out[g] = b[g] + a[g] * h_prev
            h_prev = out[g][0:1, :] if reverse else out[g][SUBLANES - 1:SUBLANES, :]
        return jnp.concatenate(out, axis=0)

    def scan_body(c, carry):
        hf, hb = carry
        rf = pl.multiple_of(c * CHUNK, CHUNK)
        rb = pl.multiple_of((nchunks - 1 - c) * CHUNK, CHUNK)
        h_f = chunk_scan(a_scr[0, pl.ds(rf, CHUNK), :], b_scr[0, pl.ds(rf, CHUNK), :], hf, False)
        h_b = chunk_scan(a_scr[1, pl.ds(rb, CHUNK), :], b_scr[1, pl.ds(rb, CHUNK), :], hb, True)
        b_scr[0, pl.ds(rf, CHUNK), :] = h_f
        b_scr[1, pl.ds(rb, CHUNK), :] = h_b
        return h_f[CHUNK - 1:CHUNK, :], h_b[0:1, :]

    if zero_init:
        init = (jnp.zeros((1, LRU_BW), f32), jnp.zeros((1, LRU_BW), f32))
    else:
        init = (h0_ref[0:1, :], h0_ref[1:2, :])
    hf, hb = lax.fori_loop(0, nchunks, scan_body, init)
    if emit_state:
        hfin_ref[0:1, :] = hf
        hfin_ref[1:2, :] = hb

    def out_body(i, carry):
        r0 = pl.multiple_of(i * blk, blk)
        h = b_scr[0, pl.ds(r0, blk), :] + b_scr[1, pl.ds(r0, blk), :]
        y_ref[pl.ds(r0, blk), :] = (h * jax.nn.gelu(lg_ref[pl.ds(r0, blk), :])).astype(y_ref.dtype)
        return carry

    lax.fori_loop(0, nblk, out_body, 0)


def _lru(path, proj, lp, h0, layer, emit_state):
    seq = path.seq_len
    proj3 = proj.reshape(path.nseq, seq, P_COLS)
    bx, bg = P_LX // LRU_BW, P_LG // LRU_BW
    in_specs = [
        pl.BlockSpec((None, seq, LRU_BW), lambda s, b: (s, 0, bx + b)),
        pl.BlockSpec((None, seq, LRU_BW), lambda s, b: (s, 0, bg + b)),
        pl.BlockSpec((CONV_W, LRU_BW), lambda s, b: (0, b)),
        pl.BlockSpec((1, LRU_BW), lambda s, b: (0, b)),
        pl.BlockSpec((2, None, LRU_BW, LRU_BW), lambda s, b: (0, b, 0, 0)),
        pl.BlockSpec((2, None, LRU_BW, LRU_BW), lambda s, b: (0, b, 0, 0)),
        pl.BlockSpec((2, LRU_BW), lambda s, b: (0, b)),
        pl.BlockSpec((2, LRU_BW), lambda s, b: (0, b)),
        pl.BlockSpec((2, LRU_BW), lambda s, b: (0, b)),
    ]
    args = [proj3, proj3, lp['lru_conv_w'], lp['lru_conv_b'].reshape(1, LRU_W), lp['lru_wa'], lp['lru_wi'],
            lp['lru_ba'], lp['lru_bi'], lp['lru_lambda']]
    if h0 is not None:
        in_specs.append(pl.BlockSpec((None, None, 2, LRU_BW), lambda s, b: (s, layer, 0, b)))
        args.append(h0)
    out_specs = [pl.BlockSpec((None, seq, LRU_BW), lambda s, b: (s, 0, b))]
    out_shape = [jax.ShapeDtypeStruct((path.nseq, seq, LRU_W), bf16)]
    if emit_state:
        out_specs.append(pl.BlockSpec((None, 2, LRU_BW), lambda s, b: (s, 0, b)))
        out_shape.append(jax.ShapeDtypeStruct((path.nseq, 2, LRU_W), f32))
    kern = functools.partial(_lru_kernel, zero_init=h0 is None, emit_state=emit_state)
    outs = pl.pallas_call(
        kern, grid=(path.nseq, LRU_BLOCKS), in_specs=in_specs, out_specs=out_specs, out_shape=out_shape,
        scratch_shapes=[pltpu.VMEM((2, seq, LRU_BW), f32), pltpu.VMEM((2, seq, LRU_BW), f32)],
        compiler_params=_cparams(("arbitrary", "arbitrary")), name="rglru")(*args)
    y = outs[0].reshape(path.tokens, LRU_W)
    return (y, outs[1]) if emit_state else (y, None)


def _pad_cols(w, n):
    return jnp.pad(w, ((0, 0),) * (w.ndim - 1) + ((0, n - w.shape[-1]),))


def _pos_table():
    quarter = D_MODEL // 4
    freqs = jnp.exp(-math.log(10000.0) * jnp.arange(quarter, dtype=f32) / quarter)
    e = jnp.arange(GRID_W, dtype=f32)[:, None] * freqs
    return jnp.concatenate([jnp.sin(e), jnp.cos(e)], axis=-1)


def _layer(path, x, h, layer, rows, mod3, lp, ffn, s0_gdn, h0_lru, emit_state, next_norm_g):
    proj = _matmul(h, lp['w_in'], 2048, 896, f32)
    qkv = _qkv_prep(path, proj, lp['gdn_conv_w'])
    outs_f = _gdn_scan(path, qkv, proj, lp['gdn_prm'], s0_gdn, layer, False, emit_state)
    outs_b = _gdn_scan(path, qkv, proj, lp['gdn_prm'], s0_gdn, layer, True, emit_state,
                       combine_with=outs_f[0], norm_g=lp['gdn_norm_g'])
    o_mix = outs_b[0]
    y_lru, h_fin = _lru(path, proj, lp, h0_lru, layer, emit_state)
    h_next = None
    if ffn['kind'] == 'dense':
        x, h2 = _out_proj(path, o_mix, y_lru, lp['w_out'], x, mod3, layer, rows, lp['norm2_g'])
        act = _gate_up(h2, ffn['w1'], ffn['w3'], 2048, 512)
        outs = _down_proj(path, act, ffn['w2'], x, mod3, layer, rows, 512, ffn['w2'].shape[0] // 2, next_norm_g)
        x = outs[0]
        h_next = outs[1] if next_norm_g is not None else None
    else:
        x, h2, route = _out_proj(path, o_mix, y_lru, lp['w_out'], x, mod3, layer, rows, lp['norm2_g'],
                                 router=(ffn['wr'], ffn['br']), h_dtype=f32)
        tok_of_row, gate_of_row, expert_of_tile, n_used, pos = _moe_route(route, path.tokens)
        h_rows = _row_gather(h2, tok_of_row, n_used)
        act = _moe_gate_up(h_rows, ffn['w1'], ffn['w3'], gate_of_row, expert_of_tile, n_used,
                           ffn['w1'].shape[2] // 2)
        y_rows = _moe_down(act, ffn['w2'], expert_of_tile, n_used)
        x = _moe_combine(path, y_rows, pos, x, mod3, layer, rows, final_g=ffn.get('final_g'))
    s_fin = jnp.stack([outs_f[1], outs_b[1]], axis=1) if emit_state else None
    return x, h_next, s_fin, h_fin


def kernel(x_prompt, x_sample, state_gdn, state_lru, c, c_ctx, w_mod, b_mod, norm1_g, norm2_g, w_in, gdn_conv_w, gdn_A_log, gdn_dt_bias, gdn_norm_g, lru_conv_w, lru_conv_b, lru_wa, lru_ba, lru_wi, lru_bi, lru_lambda, w_out, ffd_w1, ffd_w3, ffd_w2, moe_wr, moe_br, moe_w1, moe_w3, moe_w2, final_g):
    depth = w_mod.shape[0]
    bp, sp_len, d = x_prompt.shape
    bs, ss_len, _ = x_sample.shape
    ctx = _Path(bp, sp_len, 0, False)
    lat = _Path(bs, ss_len, 1, True)

    rows = SUBLANES * ((1 + bs + SUBLANES - 1) // SUBLANES)
    cvec = jnp.zeros((rows, d), f32).at[0].set(c_ctx).at[1:1 + bs].set(c)
    mod3 = _modulation(cvec, w_mod, b_mod).reshape(depth * rows * 6, 1, d)

    pos = _pos_table()
    xp = x_prompt.reshape(ctx.tokens, d)
    xs = x_sample.reshape(lat.tokens, d)
    new_gdn, new_lru = [], []
    hp = hs = None
    for l in range(depth):
        w_in_l = w_in[l].astype(bf16)
        n_ab = 4 * N_HEADS
        w_in_p = jnp.concatenate([w_in_l[:, :P_AB + n_ab], jnp.zeros((d, LANES - n_ab), bf16),
                                  w_in_l[:, P_AB + n_ab:]], axis=1)
        prm = jnp.stack([_pad_cols(gdn_A_log[l].reshape(1, -1), LANES)[0],
                         _pad_cols(gdn_dt_bias[l].reshape(1, -1), LANES)[0]])
        lp = {
            'norm1_g': norm1_g[l], 'norm2_g': norm2_g[l], 'w_in': w_in_p, 'gdn_conv_w': gdn_conv_w[l],
            'gdn_prm': prm, 'gdn_norm_g': gdn_norm_g[l], 'lru_conv_w': lru_conv_w[l], 'lru_conv_b': lru_conv_b[l],
            'lru_wa': lru_wa[l], 'lru_ba': lru_ba[l], 'lru_wi': lru_wi[l], 'lru_bi': lru_bi[l],
            'lru_lambda': lru_lambda[l], 'w_out': w_out[l].astype(bf16),
        }
        j = l // 2
        if l % 2 == 0:
            f = ffd_w1.shape[2]
            fp = 512 * ((f + 511) // 512)
            ffn = {'kind': 'dense', 'w1': _pad_cols(ffd_w1[j].astype(bf16), fp),
                   'w3': _pad_cols(ffd_w3[j].astype(bf16), fp),
                   'w2': jnp.pad(ffd_w2[j].astype(bf16), ((0, fp - f), (0, 0)))}
        else:
            ffn = {'kind': 'moe', 'wr': _pad_cols(moe_wr[j], LANES), 'br': _pad_cols(moe_br[j].reshape(1, -1), LANES),
                   'w1': moe_w1[j].astype(bf16), 'w3': moe_w3[j].astype(bf16), 'w2': moe_w2[j].astype(bf16)}
            if l == depth - 1:
                ffn['final_g'] = final_g
        next_g = norm1_g[l + 1] if l + 1 < depth else None
        if hp is None:
            (hp,) = _norm_mod(ctx, xp, lp['norm1_g'], mod3, l, rows, 1, 0)
        if hs is None and l == 0:
            hs, xs = _norm_mod(lat, xs, lp['norm1_g'], mod3, l, rows, 1, 0, pos=pos)
        elif hs is None:
            (hs,) = _norm_mod(lat, xs, lp['norm1_g'], mod3, l, rows, 1, 0)
        xp, hp, s_g, s_l = _layer(ctx, xp, hp, l, rows, mod3, lp, ffn, None, None, True, next_g)
        new_gdn.append(s_g)
        new_lru.append(s_l)
        xs, hs, _, _ = _layer(lat, xs, hs, l, rows, mod3, lp, ffn, state_gdn, state_lru, False, next_g)
    if depth % 2 == 1:
        xp, xs = _final_norm(xp, final_g), _final_norm(xs, final_g)
    y_prompt = xp.reshape(bp, sp_len, d)
    y_sample = xs.reshape(bs, ss_len, d)
    return (y_prompt, y_sample, jnp.stack(new_gdn, axis=1), jnp.stack(new_lru, axis=1))
```

```python
import functools
import math

import jax
import jax.numpy as jnp
from jax import lax
from jax.experimental import pallas as pl
from jax.experimental.pallas import tpu as pltpu

f32 = jnp.float32
bf16 = jnp.bfloat16

D_MODEL = 2048
GRID_W = 64
N_HEADS = 8
HEAD_D = 128
GDN_W = N_HEADS * HEAD_D
LRU_W = 1024
LRU_BLOCKS = 8
LRU_BW = LRU_W // LRU_BLOCKS
CONV_W = 4
CONV_LEFT = 2
CHUNK = 64
RG_C = 8.0
N_EXPERTS = 8
EPS = 1e-6
LANES = 128
SUBLANES = 8

P_Q = 0
P_K = P_Q + GDN_W
P_V = P_K + GDN_W
P_Z = P_V + GDN_W
P_AB = P_Z + GDN_W
P_LX = P_AB + LANES
P_LG = P_LX + LRU_W
P_COLS = P_LG + LRU_W

SEQ_TILE = 256
VMEM_LIMIT = 48 * 1024 * 1024
VMEM_LIMIT_BIG = 56 * 1024 * 1024


def _cparams(sem, big_tiles=False):
    return pltpu.CompilerParams(dimension_semantics=sem,
                                vmem_limit_bytes=VMEM_LIMIT_BIG if big_tiles else VMEM_LIMIT)


def _dot(a, b):
    return jnp.dot(a, b, preferred_element_type=f32)


def _sigmoid(x):
    return 0.5 * jnp.tanh(0.5 * x) + 0.5


def _silu(x):
    return x * _sigmoid(x)


def _dot_nt(a, b, precision=None):
    return lax.dot_general(a, b, (((1,), (1,)), ((), ())), precision=precision, preferred_element_type=f32)


def _dot_tn(a, b):
    return lax.dot_general(a, b, (((0,), (0,)), ((), ())), preferred_element_type=f32)


def _mod_kernel(c_ref, w_ref, b_ref, o_ref):
    a = _silu(c_ref[...]).astype(bf16)
    o_ref[...] = _dot(a, w_ref[...].astype(bf16)) + b_ref[...]


def _modulation(cvec, w_mod, b_mod):
    depth, d, n = w_mod.shape
    tn = 1024
    return pl.pallas_call(
        _mod_kernel,
        grid=(depth, n // tn),
        in_specs=[
            pl.BlockSpec((cvec.shape[0], d), lambda l, j: (0, 0)),
            pl.BlockSpec((None, d, tn), lambda l, j: (l, 0, j)),
            pl.BlockSpec((None, 1, tn), lambda l, j: (l, 0, j)),
        ],
        out_specs=pl.BlockSpec((None, cvec.shape[0], tn), lambda l, j: (l, 0, j)),
        out_shape=jax.ShapeDtypeStruct((depth, cvec.shape[0], n), f32),
        compiler_params=_cparams(("arbitrary", "arbitrary")),
        name="modulation",
    )(cvec, w_mod, b_mod.reshape(depth, 1, n))


class _Path:
    def __init__(self, nseq, seq_len, mod_row0, per_seq_mod):
        self.nseq = nseq
        self.seq_len = seq_len
        self.tokens = nseq * seq_len
        self.mod_row0 = mod_row0
        self.per_seq_mod = per_seq_mod

    def mod_row(self, tile_idx, tile):
        if self.per_seq_mod:
            return self.mod_row0 + tile_idx // (self.seq_len // tile)
        return self.mod_row0


def _mod_spec(path, layer, rows, part, tile, width, col_of=None, grid_rank=1):
    def imap(*idx):
        i = idx[0]
        j = idx[col_of] if col_of is not None else 0
        return ((layer * rows + path.mod_row(i, tile)) * 6 + part, 0, j)
    return pl.BlockSpec((None, 1, width), imap)


def _rms(x, g):
    return x * lax.rsqrt(jnp.mean(x * x, axis=-1, keepdims=True) + EPS) * g


R_IDX, R_GATE = 0, 2


def _emit_modulated_norm(x, g_ref, scale_ref, shift_ref, h_ref, wr_ref=None, br_ref=None, route_ref=None):
    h = _rms(x, g_ref[...]) * (1.0 + scale_ref[...]) + shift_ref[...]
    h_ref[...] = h.astype(h_ref.dtype)
    if route_ref is None:
        return
    logits = _dot(h.astype(bf16), wr_ref[...].astype(bf16)) + br_ref[...]
    lane = lax.broadcasted_iota(jnp.int32, logits.shape, 1)
    valid = lane < N_EXPERTS
    logits = jnp.where(valid, logits, -jnp.inf)
    e = jnp.exp(logits - jnp.max(logits, axis=-1, keepdims=True))
    p = e / jnp.sum(e, axis=-1, keepdims=True)
    p = jnp.where(valid, p, -1.0)
    m1 = jnp.max(p, axis=-1, keepdims=True)
    i1 = jnp.min(jnp.where(p == m1, lane, LANES), axis=-1, keepdims=True)
    p2 = jnp.where(lane == i1, -1.0, p)
    m2 = jnp.max(p2, axis=-1, keepdims=True)
    i2 = jnp.min(jnp.where(p2 == m2, lane, LANES), axis=-1, keepdims=True)
    den = m1 + m2
    route_ref[...] = (jnp.where(lane == R_IDX, i1.astype(f32), 0.0) + jnp.where(lane == R_IDX + 1, i2.astype(f32), 0.0)
                      + jnp.where(lane == R_GATE, m1 / den, 0.0) + jnp.where(lane == R_GATE + 1, m2 / den, 0.0))


def _norm_io(path, g, mod3, layer, rows, part_scale, part_shift, tile, router, h_dtype):
    d = D_MODEL
    in_specs = [pl.BlockSpec((1, d), lambda *idx: (0, 0)),
                _mod_spec(path, layer, rows, part_scale, tile, d),
                _mod_spec(path, layer, rows, part_shift, tile, d)]
    args = [g.reshape(1, d), mod3, mod3]
    out_specs = [pl.BlockSpec((tile, d), lambda *idx: (idx[0], 0))]
    out_shape = [jax.ShapeDtypeStruct((path.tokens, d), h_dtype)]
    if router is not None:
        in_specs += [pl.BlockSpec((d, LANES), lambda *idx: (0, 0)), pl.BlockSpec((1, LANES), lambda *idx: (0, 0))]
        args += list(router)
        out_specs.append(pl.BlockSpec((tile, LANES), lambda *idx: (idx[0], 0)))
        out_shape.append(jax.ShapeDtypeStruct((path.tokens, LANES), f32))
    return in_specs, args, out_specs, out_shape


def _norm_mod_kernel(*refs, with_pos, with_router, tile, tiles_per_seq):
    it = iter(refs)
    x_ref, g_ref, scale_ref, shift_ref = next(it), next(it), next(it), next(it)
    pos_ref = next(it) if with_pos else None
    wr_ref, br_ref = (next(it), next(it)) if with_router else (None, None)
    h_ref = next(it)
    xo_ref = next(it) if with_pos else None
    gates_ref = next(it) if with_router else None

    x = x_ref[...]
    if with_pos:
        rows_per_tile = tile // GRID_W
        row0 = (pl.program_id(0) % tiles_per_seq) * rows_per_tile
        half = D_MODEL // 2
        col_part = pos_ref[...]
        pieces = []
        for r in range(rows_per_tile):
            row_part = jnp.broadcast_to(pos_ref[pl.ds(row0 + r, 1), :], (GRID_W, half))
            pieces.append(jnp.concatenate([row_part, col_part], axis=1))
        x = x + jnp.concatenate(pieces, axis=0)
        xo_ref[...] = x
    _emit_modulated_norm(x, g_ref, scale_ref, shift_ref, h_ref, wr_ref, br_ref, gates_ref)


def _norm_mod(path, x, g, mod3, layer, rows, part_scale, part_shift, pos=None, router=None, h_dtype=bf16):
    tile = SEQ_TILE
    n = path.tokens // tile
    d = D_MODEL
    in_specs = [
        pl.BlockSpec((tile, d), lambda i: (i, 0)),
        pl.BlockSpec((1, d), lambda i: (0, 0)),
        _mod_spec(path, layer, rows, part_scale, tile, d),
        _mod_spec(path, layer, rows, part_shift, tile, d),
    ]
    args = [x, g.reshape(1, d), mod3, mod3]
    out_specs = [pl.BlockSpec((tile, d), lambda i: (i, 0))]
    out_shape = [jax.ShapeDtypeStruct((path.tokens, d), h_dtype)]
    if pos is not None:
        in_specs.append(pl.BlockSpec(pos.shape, lambda i: (0, 0)))
        args.append(pos)
        out_specs.append(pl.BlockSpec((tile, d), lambda i: (i, 0)))
        out_shape.append(jax.ShapeDtypeStruct((path.tokens, d), f32))
    if router is not None:
        wr, br = router
        in_specs += [pl.BlockSpec((d, LANES), lambda i: (0, 0)), pl.BlockSpec((1, LANES), lambda i: (0, 0))]
        args += [wr, br]
        out_specs.append(pl.BlockSpec((tile, LANES), lambda i: (i, 0)))
        out_shape.append(jax.ShapeDtypeStruct((path.tokens, LANES), f32))
    kern = functools.partial(_norm_mod_kernel, with_pos=pos is not None, with_router=router is not None,
                             tile=tile, tiles_per_seq=path.seq_len // tile)
    return pl.pallas_call(kern, grid=(n,), in_specs=in_specs, out_specs=out_specs, out_shape=out_shape,
                          compiler_params=_cparams(("arbitrary",)), name="norm_mod")(*args)


def _final_norm_kernel(x_ref, g_ref, o_ref):
    o_ref[...] = _rms(x_ref[...], g_ref[...])


def _final_norm(x, g):
    t, d = x.shape
    tile = 512
    return pl.pallas_call(
        _final_norm_kernel, grid=(t // tile,),
        in_specs=[pl.BlockSpec((tile, d), lambda i: (i, 0)), pl.BlockSpec((1, d), lambda i: (0, 0))],
        out_specs=pl.BlockSpec((tile, d), lambda i: (i, 0)),
        out_shape=jax.ShapeDtypeStruct((t, d), f32),
        compiler_params=_cparams(("arbitrary",)), name="final_norm")(x, g.reshape(1, d))


def _mm_kernel(a_ref, w_ref, o_ref):
    o_ref[...] = _dot(a_ref[...], w_ref[...]).astype(o_ref.dtype)


def _matmul(a, w, tm, tn, out_dtype):
    m, k = a.shape
    n = w.shape[1]
    return pl.pallas_call(
        _mm_kernel, grid=(m // tm, n // tn),
        in_specs=[pl.BlockSpec((tm, k), lambda i, j: (i, 0)), pl.BlockSpec((k, tn), lambda i, j: (0, j))],
        out_specs=pl.BlockSpec((tm, tn), lambda i, j: (i, j)),
        out_shape=jax.ShapeDtypeStruct((m, n), out_dtype),
        compiler_params=_cparams(("arbitrary", "arbitrary"), big_tiles=True), name="matmul")(a, w)


def _out_proj_kernel(*refs, with_router):
    a1_ref, a2_ref, w1_ref, w2_ref, x_ref, gate_ref, g_ref, scale_ref, shift_ref = refs[:9]
    wr_ref, br_ref = refs[9:11] if with_router else (None, None)
    outs = refs[11:] if with_router else refs[9:]
    xo_ref, h_ref = outs[:2]
    route_ref = outs[2] if with_router else None
    acc = _dot(a1_ref[...], w1_ref[...]) + _dot(a2_ref[...], w2_ref[...])
    x = x_ref[...] + gate_ref[...] * acc
    xo_ref[...] = x
    _emit_modulated_norm(x, g_ref, scale_ref, shift_ref, h_ref, wr_ref, br_ref, route_ref)


def _out_proj(path, a1, a2, w, x, mod3, layer, rows, norm_g, router=None, h_dtype=bf16):
    tm = 512
    m, d = x.shape
    k1, k2 = a1.shape[1], a2.shape[1]
    n_in, n_args, n_out, n_shape = _norm_io(path, norm_g, mod3, layer, rows, 4, 3, tm, router, h_dtype)
    once = pl.Buffered(1)
    return pl.pallas_call(
        functools.partial(_out_proj_kernel, with_router=router is not None), grid=(m // tm,),
        in_specs=[
            pl.BlockSpec((tm, k1), lambda i: (i, 0)),
            pl.BlockSpec((tm, k2), lambda i: (i, 0)),
            pl.BlockSpec((k1, d), lambda i: (0, 0), pipeline_mode=once),
            pl.BlockSpec((k2, d), lambda i: (k1 // k2, 0), pipeline_mode=once),
            pl.BlockSpec((tm, d), lambda i: (i, 0)),
            _mod_spec(path, layer, rows, 2, tm, d),
        ] + n_in,
        out_specs=[pl.BlockSpec((tm, d), lambda i: (i, 0))] + n_out,
        out_shape=[jax.ShapeDtypeStruct((m, d), f32)] + n_shape,
        compiler_params=_cparams(("arbitrary",)), name="out_proj")(a1, a2, w, w, x, mod3, *n_args)


def _gate_up_kernel(h_ref, w1_ref, w3_ref, o_ref):
    h = h_ref[...]
    a = _dot(h, w1_ref[...])
    b = _dot(h, w3_ref[...])
    o_ref[...] = (_silu(a) * b).astype(o_ref.dtype)


def _gate_up(h, w1, w3, tm, tn):
    m, k = h.shape
    n = w1.shape[1]
    return pl.pallas_call(
        _gate_up_kernel, grid=(m // tm, n // tn),
        in_specs=[pl.BlockSpec((tm, k), lambda i, j: (i, 0)),
                  pl.BlockSpec((k, tn), lambda i, j: (0, j)),
                  pl.BlockSpec((k, tn), lambda i, j: (0, j))],
        out_specs=pl.BlockSpec((tm, tn), lambda i, j: (i, j)),
        out_shape=jax.ShapeDtypeStruct((m, n), bf16),
        compiler_params=_cparams(("arbitrary", "arbitrary"), big_tiles=True), name="gate_up")(h, w1, w3)


MOE_TILE = 512
DMA_UNROLL = 8
GATHER_ROWS = 2 * MOE_TILE


def _moe_route(route, n_tok):
    tm = MOE_TILE
    n_rows = 2 * n_tok + N_EXPERTS * tm
    e_idx = jnp.clip(route[:, R_IDX:R_IDX + 2].astype(jnp.int32), 0, N_EXPERTS - 1).reshape(-1)
    gate = route[:, R_GATE:R_GATE + 2].reshape(-1)
    one_hot = (e_idx[:, None] == jnp.arange(N_EXPERTS, dtype=jnp.int32)[None, :]).astype(jnp.int32)
    csum = jnp.cumsum(one_hot, axis=0)
    rank = jnp.take_along_axis(csum, e_idx[:, None], axis=1)[:, 0] - 1
    padded = ((csum[-1] + tm - 1) // tm) * tm
    ends = jnp.cumsum(padded)
    pos = (ends - padded)[e_idx] + rank
    tok = (jnp.arange(2 * n_tok, dtype=jnp.int32) // 2).astype(f32)
    table = jnp.zeros((n_rows, 2), f32).at[pos].set(jnp.stack([tok, gate], axis=1))
    tok_of_row = table[:, 0].astype(jnp.int32)
    gate_of_row = table[:, 1]
    tile_start = jnp.arange(n_rows // tm, dtype=jnp.int32) * tm
    expert_of_tile = jnp.minimum(jnp.sum(tile_start[:, None] >= ends[None, :], axis=1), N_EXPERTS - 1)
    n_used = (ends[-1:] // tm).astype(jnp.int32)
    return tok_of_row, gate_of_row.reshape(n_rows, 1), expert_of_tile.astype(jnp.int32), n_used, pos.astype(jnp.int32)


def _row_copy(src_hbm, src_row, dst_vmem, dst_row, sem):
    return pltpu.make_async_copy(src_hbm.at[pl.ds(src_row, 1), :], dst_vmem.at[pl.ds(dst_row, 1), :], sem)


def _row_gather_kernel(tok_ref, nused_ref, h_hbm, o_ref, sem):
    i = pl.program_id(0)
    tg = o_ref.shape[0]
    used = i * (tg // MOE_TILE) < nused_ref[0]

    @pl.when(used)
    def _():
        def issue(r, c):
            _row_copy(h_hbm, tok_ref[i * tg + r], o_ref, r, sem).start()
            return c
        lax.fori_loop(0, tg, issue, 0, unroll=DMA_UNROLL)
        pltpu.make_async_copy(h_hbm.at[pl.ds(0, tg), :], o_ref, sem).wait()

    @pl.when(jnp.logical_not(used))
    def _():
        o_ref[...] = jnp.zeros_like(o_ref)


def _row_gather(h, tok_of_row, n_used):
    n_rows = tok_of_row.shape[0]
    d = h.shape[1]
    assert n_rows % GATHER_ROWS == 0
    return pl.pallas_call(
        _row_gather_kernel,
        grid_spec=pltpu.PrefetchScalarGridSpec(
            num_scalar_prefetch=2, grid=(n_rows // GATHER_ROWS,),
            in_specs=[pl.BlockSpec(memory_space=pl.ANY)],
            out_specs=pl.BlockSpec((GATHER_ROWS, d), lambda i, tok, nu: (i, 0)),
            scratch_shapes=[pltpu.SemaphoreType.DMA]),
        out_shape=jax.ShapeDtypeStruct((n_rows, d), h.dtype),
        compiler_params=_cparams(("arbitrary",)), name="moe_row_gather")(tok_of_row, n_used, h)


def _moe_gate_up_kernel(eot_ref, nused_ref, x_ref, w1_ref, w3_ref, g_ref, o_ref):
    i = pl.program_id(1)

    @pl.when(i < nused_ref[0])
    def _():
        x = x_ref[...].astype(bf16)
        a = _dot(x, w1_ref[...])
        b = _dot(x, w3_ref[...])
        o_ref[...] = (g_ref[...] * (_silu(a) * b)).astype(o_ref.dtype)

    @pl.when(i >= nused_ref[0])
    def _():
        o_ref[...] = jnp.zeros_like(o_ref)


def _moe_gate_up(rows, w1, w3, gate_of_row, expert_of_tile, n_used, tn):
    tm = MOE_TILE
    n_rows, k = rows.shape
    f = w1.shape[2]
    return pl.pallas_call(
        _moe_gate_up_kernel,
        grid_spec=pltpu.PrefetchScalarGridSpec(
            num_scalar_prefetch=2, grid=(f // tn, n_rows // tm),
            in_specs=[pl.BlockSpec((tm, k), lambda j, i, eot, nu: (i, 0)),
                      pl.BlockSpec((None, k, tn), lambda j, i, eot, nu: (eot[i], 0, j)),
                      pl.BlockSpec((None, k, tn), lambda j, i, eot, nu: (eot[i], 0, j)),
                      pl.BlockSpec((tm, 1), lambda j, i, eot, nu: (i, 0))],
            out_specs=pl.BlockSpec((tm, tn), lambda j, i, eot, nu: (i, j))),
        out_shape=jax.ShapeDtypeStruct((n_rows, f), bf16),
        compiler_params=_cparams(("arbitrary", "arbitrary")), name="moe_gate_up")(
            expert_of_tile, n_used, rows, w1, w3, gate_of_row)


def _moe_down_kernel(eot_ref, nused_ref, a_ref, w_ref, o_ref):
    i = pl.program_id(0)

    @pl.when(i < nused_ref[0])
    def _():
        o_ref[...] = _dot(a_ref[...], w_ref[...])

    @pl.when(i >= nused_ref[0])
    def _():
        o_ref[...] = jnp.zeros_like(o_ref)


def _moe_down(act, w2, expert_of_tile, n_used):
    tm = MOE_TILE
    n_rows, f = act.shape
    d = w2.shape[2]
    return pl.pallas_call(
        _moe_down_kernel,
        grid_spec=pltpu.PrefetchScalarGridSpec(
            num_scalar_prefetch=2, grid=(n_rows // tm,),
            in_specs=[pl.BlockSpec((tm, f), lambda i, eot, nu: (i, 0)),
                      pl.BlockSpec((None, f, d), lambda i, eot, nu: (eot[i], 0, 0))],
            out_specs=pl.BlockSpec((tm, d), lambda i, eot, nu: (i, 0))),
        out_shape=jax.ShapeDtypeStruct((n_rows, d), f32),
        compiler_params=_cparams(("arbitrary",)), name="moe_down")(expert_of_tile, n_used, act, w2)


def _moe_combine_kernel(pos_ref, y_hbm, x_ref, gate_ref, *rest):
    fg_ref = rest[0] if len(rest) == 4 else None
    o_ref, buf, sem = rest[-3:]
    i = pl.program_id(0)
    tm = o_ref.shape[0]

    def issue_tile(t, slot):
        def issue(r, c):
            a = 2 * (t * tm + r)
            _row_copy(y_hbm, pos_ref[a], buf.at[slot, 0], r, sem.at[slot]).start()
            _row_copy(y_hbm, pos_ref[a + 1], buf.at[slot, 1], r, sem.at[slot]).start()
            return c
        lax.fori_loop(0, tm, issue, 0, unroll=DMA_UNROLL)

    @pl.when(i == 0)
    def _():
        issue_tile(0, 0)

    @pl.when(i + 1 < pl.num_programs(0))
    def _():
        issue_tile(i + 1, (i + 1) % 2)

    slot = i % 2
    for b in range(2):
        pltpu.make_async_copy(y_hbm.at[pl.ds(0, tm), :], buf.at[slot, b], sem.at[slot]).wait()
    y = x_ref[...] + gate_ref[...] * (buf[slot, 0] + buf[slot, 1])
    o_ref[...] = y if fg_ref is None else _rms(y, fg_ref[...])


def _moe_combine(path, y_rows, pos, x, mod3, layer, rows, final_g=None):
    tm = SEQ_TILE
    m, d = x.shape
    in_specs = [pl.BlockSpec(memory_space=pl.ANY),
                pl.BlockSpec((tm, d), lambda i, pos: (i, 0)),
                pl.BlockSpec((None, 1, d), lambda i, pos: ((layer * rows + path.mod_row(i, tm)) * 6 + 5, 0, 0))]
    args = [pos, y_rows, x, mod3]
    if final_g is not None:
        in_specs.append(pl.BlockSpec((1, d), lambda i, pos: (0, 0)))
        args.append(final_g.reshape(1, d))
    return pl.pallas_call(
        _moe_combine_kernel,
        grid_spec=pltpu.PrefetchScalarGridSpec(
            num_scalar_prefetch=1, grid=(m // tm,), in_specs=in_specs,
            out_specs=pl.BlockSpec((tm, d), lambda i, pos: (i, 0)),
            scratch_shapes=[pltpu.VMEM((2, 2, tm, d), f32), pltpu.SemaphoreType.DMA((2,))]),
        out_shape=jax.ShapeDtypeStruct((m, d), f32),
        compiler_params=_cparams(("arbitrary",)), name="moe_combine")(*args)


def _down_kernel(*refs, nk, with_norm):
    a_ref, w_ref, x_ref, gate_ref = refs[:4]
    g_ref, scale_ref, shift_ref = refs[4:7] if with_norm else (None, None, None)
    outs = refs[7:] if with_norm else refs[4:]
    o_ref = outs[0]
    h_ref = outs[1] if with_norm else None
    acc_ref = outs[-1]
    kk = pl.program_id(1)

    @pl.when(kk == 0)
    def _():
        acc_ref[...] = jnp.zeros_like(acc_ref)

    acc_ref[...] += _dot(a_ref[...], w_ref[...])

    @pl.when(kk == nk - 1)
    def _():
        x = x_ref[...] + gate_ref[...] * acc_ref[...]
        o_ref[...] = x
        if with_norm:
            _emit_modulated_norm(x, g_ref, scale_ref, shift_ref, h_ref)


def _down_proj(path, act, w2, x, mod3, layer, rows, tm, tk, next_norm_g=None):
    m, d = x.shape
    kdim = act.shape[1]
    nk = kdim // tk
    n_in, n_args, n_out, n_shape = ([], [], [], [])
    if next_norm_g is not None:
        n_in, n_args, n_out, n_shape = _norm_io(path, next_norm_g, mod3, layer + 1, rows, 1, 0, tm, None, bf16)
    return pl.pallas_call(
        functools.partial(_down_kernel, nk=nk, with_norm=next_norm_g is not None), grid=(m // tm, nk),
        in_specs=[pl.BlockSpec((tm, tk), lambda i, k: (i, k)),
                  pl.BlockSpec((tk, d), lambda i, k: (k, 0)),
                  pl.BlockSpec((tm, d), lambda i, k: (i, 0)),
                  _mod_spec(path, layer, rows, 5, tm, d)] + n_in,
        out_specs=[pl.BlockSpec((tm, d), lambda i, k: (i, 0))] + n_out,
        out_shape=[jax.ShapeDtypeStruct((m, d), f32)] + n_shape,
        scratch_shapes=[pltpu.VMEM((tm, d), f32)],
        compiler_params=_cparams(("arbitrary", "arbitrary"), big_tiles=True), name="down_proj")(act, w2, x, mod3, *n_args)


def _conv_window(win, w):
    n = win.shape[0] - 2 * SUBLANES
    acc = None
    for j in range(CONV_W):
        off = SUBLANES + j - CONV_LEFT
        term = win[off:off + n] * w[j:j + 1]
        acc = term if acc is None else acc + term
    return acc


def _qkv_prep_kernel(prev_ref, cur_ref, next_ref, w_ref, o_ref, *, tiles_per_seq):
    t = pl.program_id(0) % tiles_per_seq
    kind = pl.program_id(1)
    has_prev = (t > 0).astype(f32)
    has_next = (t < tiles_per_seq - 1).astype(f32)
    scale = jnp.where(kind == 0, HEAD_D ** -0.5, 1.0).astype(f32)
    tile = cur_ref.shape[0]
    for c in range(tile // CHUNK):
        r0 = c * CHUNK
        for hd in range(N_HEADS):
            cols = slice(hd * HEAD_D, (hd + 1) * HEAD_D)
            top = prev_ref[:, cols] * has_prev if c == 0 else cur_ref[r0 - SUBLANES:r0, cols]
            bot = (next_ref[:, cols] * has_next if r0 + CHUNK == tile
                   else cur_ref[r0 + CHUNK:r0 + CHUNK + SUBLANES, cols])
            win = jnp.concatenate([top, cur_ref[r0:r0 + CHUNK, cols], bot], axis=0)
            y = _silu(_conv_window(win, w_ref[:, cols]))
            nrm = lax.rsqrt(jnp.sum(y * y, axis=-1, keepdims=True) + EPS) * scale
            y = y * jnp.where(kind < 2, nrm, 1.0)
            o_ref[r0:r0 + CHUNK, cols] = y


def _qkv_prep(path, proj, conv_w):
    tile = SEQ_TILE
    n = path.tokens // tile
    tps = path.seq_len // tile
    rb = tile // SUBLANES
    last = path.tokens // SUBLANES - 1
    return pl.pallas_call(
        functools.partial(_qkv_prep_kernel, tiles_per_seq=tps), grid=(n, 3),
        in_specs=[
            pl.BlockSpec((SUBLANES, GDN_W), lambda i, j: (jnp.maximum(i * rb - 1, 0), j)),
            pl.BlockSpec((tile, GDN_W), lambda i, j: (i, j)),
            pl.BlockSpec((SUBLANES, GDN_W), lambda i, j: (jnp.minimum((i + 1) * rb, last), j)),
            pl.BlockSpec((CONV_W, GDN_W), lambda i, j: (0, j)),
        ],
        out_specs=pl.BlockSpec((tile, GDN_W), lambda i, j: (i, j)),
        out_shape=jax.ShapeDtypeStruct((path.tokens, 3 * GDN_W), f32),
        compiler_params=_cparams(("arbitrary", "arbitrary")), name="qkv_prep")(proj, proj, proj, conv_w)


INV_BLOCK = 16


def _bdot(a, b):
    return _dot(a.astype(bf16), b.astype(bf16))


def _unit_tri_inverse(a_list, eye, diag_blk):
    p = [-jnp.where(diag_blk, a, 0.0) for a in a_list]
    x = [eye + pi for pi in p]
    span = 2
    while span < INV_BLOCK:
        p = [_bdot(pi, pi) for pi in p]
        x = [xi + _bdot(xi, pi) for xi, pi in zip(x, p)]
        span *= 2
    ia = [(eye + a).astype(bf16) for a in a_list]
    span = INV_BLOCK
    while span < a_list[0].shape[0]:
        r = [eye - _dot(iai, xi.astype(bf16)) for iai, xi in zip(ia, x)]
        x = [xi + _bdot(xi, ri) for xi, ri in zip(x, r)]
        span *= 2
    return x


def _gdn_kernel(*refs, reverse, tiles_per_seq, zero_init, emit_state, combine):
    it = iter(refs)
    q_ref, k_ref, v_ref, ab_ref, prm_ref = next(it), next(it), next(it), next(it), next(it)
    s0_ref = None if zero_init else next(it)
    other_ref, z_ref, gn_ref = (next(it), next(it), next(it)) if combine else (None, None, None)
    o_ref = next(it)
    sfin_ref = next(it) if emit_state else None
    s_scr = next(it)

    t = pl.program_id(1)
    nchunks = q_ref.shape[0] // CHUNK

    @pl.when(t == 0)
    def _():
        if zero_init:
            s_scr[...] = jnp.zeros_like(s_scr)
        else:
            s_scr[...] = s0_ref[...]

    ri = lax.broadcasted_iota(jnp.int32, (CHUNK, CHUNK), 0)
    ci = lax.broadcasted_iota(jnp.int32, (CHUNK, CHUNK), 1)
    incl = (ri <= ci) if reverse else (ri >= ci)
    strict = (ri < ci) if reverse else (ri > ci)
    eye = (ri == ci).astype(f32)
    diag_blk = (ri // INV_BLOCK) == (ci // INV_BLOCK)
    tri = incl.astype(f32)
    last_row = 0 if reverse else CHUNK - 1
    dir_off = N_HEADS if reverse else 0
    neg_a = -jnp.exp(prm_ref[0:1, :])
    dt_bias = prm_ref[1:2, :]

    heads = range(N_HEADS)
    cols = [slice(hd * HEAD_D, (hd + 1) * HEAD_D) for hd in heads]
    ga = [dir_off + hd for hd in heads]

    def prepare(chunks):
        gate = []
        for cc in chunks:
            rows = slice(cc * CHUNK, (cc + 1) * CHUNK)
            ab = ab_ref[rows, :]
            g = neg_a * jax.nn.softplus(ab + dt_bias)
            beta = _sigmoid(ab)
            gc = jnp.dot(tri, g, precision=lax.Precision.HIGHEST, preferred_element_type=f32)
            g_last = gc[last_row:last_row + 1, :]
            gate.append({'beta': beta, 'gc': gc, 'gc_t': gc.T, 'e_gc': jnp.exp(gc),
                         'e_rest': jnp.exp(g_last - gc), 'e_last': jnp.exp(g_last)})
        probs = [(i, h) for i in range(len(chunks)) for h in heads]

        def rows_of(i):
            return slice(chunks[i] * CHUNK, (chunks[i] + 1) * CHUNK)

        def col(i, name, lane):
            return gate[i][name][:, lane:lane + 1]

        b_col = [col(i, 'beta', 2 * N_HEADS + ga[h]) for i, h in probs]
        eg_col = [col(i, 'e_gc', ga[h]) for i, h in probs]
        qh = [q_ref[rows_of(i), cols[h]] for i, h in probs]
        kh = [k_ref[rows_of(i), cols[h]] for i, h in probs]
        vh = [v_ref[rows_of(i), cols[h]] for i, h in probs]
        n = range(len(probs))
        gram = [_dot_nt(jnp.concatenate([qh[p], kh[p]], axis=0).astype(bf16), kh[p].astype(bf16)) for p in n]
        decay = [jnp.exp(jnp.where(incl, col(i, 'gc', ga[h]) - gate[i]['gc_t'][ga[h]:ga[h] + 1, :], -jnp.inf))
                 for i, h in probs]
        a_low = [jnp.where(strict, gram[p][CHUNK:] * b_col[p] * decay[p], 0.0) for p in n]
        qk = [(gram[p][:CHUNK] * decay[p]).astype(bf16) for p in n]
        t_inv = _unit_tri_inverse(a_low, eye, diag_blk)
        rhs = [jnp.concatenate([vh[p] * b_col[p], kh[p] * b_col[p] * eg_col[p]], axis=1) for p in n]
        uw = [_bdot(t_inv[p], rhs[p]) for p in n]
        w_qg = [jnp.concatenate([uw[p][:, HEAD_D:], qh[p] * eg_col[p]], axis=0).astype(bf16) for p in n]
        kd = [(kh[p] * col(i, 'e_rest', ga[h])).astype(bf16) for p, (i, h) in enumerate(probs)]
        e_last = [col(i, 'e_last', ga[h]) for i, h in probs]

        def per_chunk(lst):
            return [lst[i * N_HEADS:(i + 1) * N_HEADS] for i in range(len(chunks))]
        return {'u': per_chunk([x[:, :HEAD_D] for x in uw]), 'w_qg': per_chunk(w_qg), 'qk': per_chunk(qk),
                'kd': per_chunk(kd), 'e_last': per_chunk(e_last)}

    order = list(range(nchunks))[::-1] if reverse else list(range(nchunks))
    pre = prepare(order)
    state = [s_scr[h] for h in heads]
    for i, cc in enumerate(order):
        rows = slice(cc * CHUNK, (cc + 1) * CHUNK)
        s_b = [x.astype(bf16) for x in state]
        ws = [_dot(pre['w_qg'][i][h], s_b[h]) for h in heads]
        v_new = [(pre['u'][i][h] - ws[h][:CHUNK]).astype(bf16) for h in heads]
        for h in heads:
            o_h = ws[h][CHUNK:] + _dot(pre['qk'][i][h], v_new[h])
            if combine:
                y = _rms(other_ref[rows, cols[h]] + o_h, gn_ref[...])
                o_h = y * _silu(z_ref[rows, cols[h]])
            o_ref[rows, cols[h]] = o_h.astype(o_ref.dtype)
        state = [state[h] * pre['e_last'][i][h] + _dot_tn(pre['kd'][i][h], v_new[h]) for h in heads]
    for h in heads:
        s_scr[h] = state[h]

    if emit_state:
        @pl.when(t == tiles_per_seq - 1)
        def _():
            sfin_ref[...] = s_scr[...]


def _gdn_scan(path, qkv, proj, prm, s0, layer, reverse, emit_state, combine_with=None, norm_g=None):
    tile = SEQ_TILE
    tps = path.seq_len // tile
    d = 1 if reverse else 0

    def tok(s, t):
        return s * tps + ((tps - 1 - t) if reverse else t)

    in_specs = [
        pl.BlockSpec((tile, GDN_W), lambda s, t: (tok(s, t), 0)),
        pl.BlockSpec((tile, GDN_W), lambda s, t: (tok(s, t), 1)),
        pl.BlockSpec((tile, GDN_W), lambda s, t: (tok(s, t), 2)),
        pl.BlockSpec((tile, LANES), lambda s, t: (tok(s, t), P_AB // LANES)),
        pl.BlockSpec((2, LANES), lambda s, t: (0, 0)),
    ]
    args = [qkv, qkv, qkv, proj, prm]
    if s0 is not None:
        in_specs.append(pl.BlockSpec((None, None, None, N_HEADS, HEAD_D, HEAD_D),
                                     lambda s, t: (s, layer, d, 0, 0, 0)))
        args.append(s0)
    combine = combine_with is not None
    if combine:
        in_specs += [pl.BlockSpec((tile, GDN_W), lambda s, t: (tok(s, t), 0)),
                     pl.BlockSpec((tile, GDN_W), lambda s, t: (tok(s, t), P_Z // GDN_W)),
                     pl.BlockSpec((1, HEAD_D), lambda s, t: (0, 0))]
        args += [combine_with, proj, norm_g.reshape(1, HEAD_D)]
    out_specs = [pl.BlockSpec((tile, GDN_W), lambda s, t: (tok(s, t), 0))]
    out_shape = [jax.ShapeDtypeStruct((path.tokens, GDN_W), bf16 if combine else f32)]
    if emit_state:
        out_specs.append(pl.BlockSpec((None, N_HEADS, HEAD_D, HEAD_D), lambda s, t: (s, 0, 0, 0)))
        out_shape.append(jax.ShapeDtypeStruct((path.nseq, N_HEADS, HEAD_D, HEAD_D), f32))
    kern = functools.partial(_gdn_kernel, reverse=reverse, tiles_per_seq=tps, zero_init=s0 is None,
                             emit_state=emit_state, combine=combine)
    return pl.pallas_call(
        kern, grid=(path.nseq, tps), in_specs=in_specs, out_specs=out_specs, out_shape=out_shape,
        scratch_shapes=[pltpu.VMEM((N_HEADS, HEAD_D, HEAD_D), f32)],
        compiler_params=_cparams(("arbitrary", "arbitrary")), name="gdn_scan")(*args)


def _lru_kernel(*refs, zero_init, emit_state):
    it = iter(refs)
    lx_ref, lg_ref, cw_ref, cb_ref = next(it), next(it), next(it), next(it)
    wa_ref, wi_ref, ba_ref, bi_ref, lam_ref = next(it), next(it), next(it), next(it), next(it)
    h0_ref = None if zero_init else next(it)
    y_ref = next(it)
    hfin_ref = next(it) if emit_state else None
    a_scr, b_scr = next(it), next(it)

    seq = lx_ref.shape[0]
    blk = 256
    nblk = seq // blk
    w_cat = jnp.concatenate([wa_ref[0], wi_ref[0], wa_ref[1], wi_ref[1]], axis=1).astype(bf16)
    bias_cat = jnp.concatenate([ba_ref[0:1], bi_ref[0:1], ba_ref[1:2], bi_ref[1:2]], axis=1)
    sp = jax.nn.softplus(-lam_ref[...])
    cw = cw_ref[...]
    cb = cb_ref[...]

    def gates_body(i, carry):
        r0 = pl.multiple_of(i * blk, blk)
        r_top = pl.multiple_of(jnp.maximum(r0 - SUBLANES, 0), SUBLANES)
        r_bot = pl.multiple_of(jnp.minimum(r0 + blk, seq - SUBLANES), SUBLANES)
        top = lx_ref[pl.ds(r_top, SUBLANES), :] * jnp.where(i > 0, 1.0, 0.0)
        bot = lx_ref[pl.ds(r_bot, SUBLANES), :] * jnp.where(i < nblk - 1, 1.0, 0.0)
        win = jnp.concatenate([top, lx_ref[pl.ds(r0, blk), :], bot], axis=0)
        xl = _conv_window(win, cw) + cb
        pre = _dot(xl.astype(bf16), w_cat) + bias_cat
        for d in range(2):
            r = _sigmoid(pre[:, (2 * d) * LRU_BW:(2 * d + 1) * LRU_BW])
            ig = _sigmoid(pre[:, (2 * d + 1) * LRU_BW:(2 * d + 2) * LRU_BW])
            a = jnp.exp(-RG_C * r * sp[d:d + 1])
            a_scr[d, pl.ds(r0, blk), :] = a
            b_scr[d, pl.ds(r0, blk), :] = jnp.sqrt(1.0 - a * a) * (ig * xl)
        return carry

    lax.fori_loop(0, nblk, gates_body, 0)

    groups = CHUNK // SUBLANES
    row = lax.broadcasted_iota(jnp.int32, (groups, SUBLANES, LRU_BW), 1)
    nchunks = seq // CHUNK

    def chunk_scan(a, b, h_prev, reverse):
        a = a.reshape(groups, SUBLANES, LRU_BW)
        b = b.reshape(groups, SUBLANES, LRU_BW)
        s = 1
        while s < SUBLANES:
            if reverse:
                keep = row < SUBLANES - s
                a_sh = jnp.where(keep, pltpu.roll(a, SUBLANES - s, 1), 1.0)
                b_sh = jnp.where(keep, pltpu.roll(b, SUBLANES - s, 1), 0.0)
            else:
                keep = row >= s
                a_sh = jnp.where(keep, pltpu.roll(a, s, 1), 1.0)
                b_sh = jnp.where(keep, pltpu.roll(b, s, 1), 0.0)
            b = a * b_sh + b
            a = a * a_sh
            s *= 2
        out = [None] * groups
        for g in (range(groups - 1, -1, -1) if reverse else range(groups)):
            out[g] = b[g] + a[g] * h_prev
            h_prev = out[g][0:1, :] if reverse else out[g][SUBLANES - 1:SUBLANES, :]
        return jnp.concatenate(out, axis=0)

    def scan_body(c, carry):
        hf, hb = carry
        rf = pl.multiple_of(c * CHUNK, CHUNK)
        rb = pl.multiple_of((nchunks - 1 - c) * CHUNK, CHUNK)
        h_f = chunk_scan(a_scr[0, pl.ds(rf, CHUNK), :], b_scr[0, pl.ds(rf, CHUNK), :], hf, False)
        h_b = chunk_scan(a_scr[1, pl.ds(rb, CHUNK), :], b_scr[1, pl.ds(rb, CHUNK), :], hb, True)
        b_scr[0, pl.ds(rf, CHUNK), :] = h_f
        b_scr[1, pl.ds(rb, CHUNK), :] = h_b
        return h_f[CHUNK - 1:CHUNK, :], h_b[0:1, :]

    if zero_init:
        init = (jnp.zeros((1, LRU_BW), f32), jnp.zeros((1, LRU_BW), f32))
    else:
        init = (h0_ref[0:1, :], h0_ref[1:2, :])
    hf, hb = lax.fori_loop(0, nchunks, scan_body, init)
    if emit_state:
        hfin_ref[0:1, :] = hf
        hfin_ref[1:2, :] = hb

    def out_body(i, carry):
        r0 = pl.multiple_of(i * blk, blk)
        h = b_scr[0, pl.ds(r0, blk), :] + b_scr[1, pl.ds(r0, blk), :]
        y_ref[pl.ds(r0, blk), :] = (h * jax.nn.gelu(lg_ref[pl.ds(r0, blk), :])).astype(y_ref.dtype)
        return carry

    lax.fori_loop(0, nblk, out_body, 0)


def _lru(path, proj, lp, h0, layer, emit_state):
    seq = path.seq_len
    proj3 = proj.reshape(path.nseq, seq, P_COLS)
    bx, bg = P_LX // LRU_BW, P_LG // LRU_BW
    in_specs = [
        pl.BlockSpec((None, seq, LRU_BW), lambda s, b: (s, 0, bx + b)),
        pl.BlockSpec((None, seq, LRU_BW), lambda s, b: (s, 0, bg + b)),
        pl.BlockSpec((CONV_W, LRU_BW), lambda s, b: (0, b)),
        pl.BlockSpec((1, LRU_BW), lambda s, b: (0, b)),
        pl.BlockSpec((2, None, LRU_BW, LRU_BW), lambda s, b: (0, b, 0, 0)),
        pl.BlockSpec((2, None, LRU_BW, LRU_BW), lambda s, b: (0, b, 0, 0)),
        pl.BlockSpec((2, LRU_BW), lambda s, b: (0, b)),
        pl.BlockSpec((2, LRU_BW), lambda s, b: (0, b)),
        pl.BlockSpec((2, LRU_BW), lambda s, b: (0, b)),
    ]
    args = [proj3, proj3, lp['lru_conv_w'], lp['lru_conv_b'].reshape(1, LRU_W), lp['lru_wa'], lp['lru_wi'],
            lp['lru_ba'], lp['lru_bi'], lp['lru_lambda']]
    if h0 is not None:
        in_specs.append(pl.BlockSpec((None, None, 2, LRU_BW), lambda s, b: (s, layer, 0, b)))
        args.append(h0)
    out_specs = [pl.BlockSpec((None, seq, LRU_BW), lambda s, b: (s, 0, b))]
    out_shape = [jax.ShapeDtypeStruct((path.nseq, seq, LRU_W), bf16)]
    if emit_state:
        out_specs.append(pl.BlockSpec((None, 2, LRU_BW), lambda s, b: (s, 0, b)))
        out_shape.append(jax.ShapeDtypeStruct((path.nseq, 2, LRU_W), f32))
    kern = functools.partial(_lru_kernel, zero_init=h0 is None, emit_state=emit_state)
    outs = pl.pallas_call(
        kern, grid=(path.nseq, LRU_BLOCKS), in_specs=in_specs, out_specs=out_specs, out_shape=out_shape,
        scratch_shapes=[pltpu.VMEM((2, seq, LRU_BW), f32), pltpu.VMEM((2, seq, LRU_BW), f32)],
        compiler_params=_cparams(("arbitrary", "arbitrary")), name="rglru")(*args)
    y = outs[0].reshape(path.tokens, LRU_W)
    return (y, outs[1]) if emit_state else (y, None)


def _pad_cols(w, n):
    return jnp.pad(w, ((0, 0),) * (w.ndim - 1) + ((0, n - w.shape[-1]),))


def _pos_table():
    quarter = D_MODEL // 4
    freqs = jnp.exp(-math.log(10000.0) * jnp.arange(quarter, dtype=f32) / quarter)
    e = jnp.arange(GRID_W, dtype=f32)[:, None] * freqs
    return jnp.concatenate([jnp.sin(e), jnp.cos(e)], axis=-1)


def _layer(path, x, h, layer, rows, mod3, lp, ffn, s0_gdn, h0_lru, emit_state, next_norm_g):
    proj = _matmul(h, lp['w_in'], 2048, 896, f32)
    qkv = _qkv_prep(path, proj, lp['gdn_conv_w'])
    outs_f = _gdn_scan(path, qkv, proj, lp['gdn_prm'], s0_gdn, layer, False, emit_state)
    outs_b = _gdn_scan(path, qkv, proj, lp['gdn_prm'], s0_gdn, layer, True, emit_state,
                       combine_with=outs_f[0], norm_g=lp['gdn_norm_g'])
    o_mix = outs_b[0]
    y_lru, h_fin = _lru(path, proj, lp, h0_lru, layer, emit_state)
    h_next = None
    if ffn['kind'] == 'dense':
        x, h2 = _out_proj(path, o_mix, y_lru, lp['w_out'], x, mod3, layer, rows, lp['norm2_g'])
        act = _gate_up(h2, ffn['w1'], ffn['w3'], 1024, 512)
        outs = _down_proj(path, act, ffn['w2'], x, mod3, layer, rows, 512, ffn['w2'].shape[0] // 2, next_norm_g)
        x = outs[0]
        h_next = outs[1] if next_norm_g is not None else None
    else:
        x, h2, route = _out_proj(path, o_mix, y_lru, lp['w_out'], x, mod3, layer, rows, lp['norm2_g'],
                                 router=(ffn['wr'], ffn['br']), h_dtype=f32)
        tok_of_row, gate_of_row, expert_of_tile, n_used, pos = _moe_route(route, path.tokens)
        h_rows = _row_gather(h2, tok_of_row, n_used)
        act = _moe_gate_up(h_rows, ffn['w1'], ffn['w3'], gate_of_row, expert_of_tile, n_used,
                           ffn['w1'].shape[2] // 2)
        y_rows = _moe_down(act, ffn['w2'], expert_of_tile, n_used)
        x = _moe_combine(path, y_rows, pos, x, mod3, layer, rows, final_g=ffn.get('final_g'))
    s_fin = jnp.stack([outs_f[1], outs_b[1]], axis=1) if emit_state else None
    return x, h_next, s_fin, h_fin


def kernel(x_prompt, x_sample, state_gdn, state_lru, c, c_ctx, w_mod, b_mod, norm1_g, norm2_g, w_in, gdn_conv_w, gdn_A_log, gdn_dt_bias, gdn_norm_g, lru_conv_w, lru_conv_b, lru_wa, lru_ba, lru_wi, lru_bi, lru_lambda, w_out, ffd_w1, ffd_w3, ffd_w2, moe_wr, moe_br, moe_w1, moe_w3, moe_w2, final_g):
    depth = w_mod.shape[0]
    bp, sp_len, d = x_prompt.shape
    bs, ss_len, _ = x_sample.shape
    ctx = _Path(bp, sp_len, 0, False)
    lat = _Path(bs, ss_len, 1, True)

    rows = SUBLANES * ((1 + bs + SUBLANES - 1) // SUBLANES)
    cvec = jnp.zeros((rows, d), f32).at[0].set(c_ctx).at[1:1 + bs].set(c)
    mod3 = _modulation(cvec, w_mod, b_mod).reshape(depth * rows * 6, 1, d)

    pos = _pos_table()
    xp = x_prompt.reshape(ctx.tokens, d)
    xs = x_sample.reshape(lat.tokens, d)
    new_gdn, new_lru = [], []
    hp = hs = None
    for l in range(depth):
        w_in_l = w_in[l].astype(bf16)
        n_ab = 4 * N_HEADS
        w_in_p = jnp.concatenate([w_in_l[:, :P_AB + n_ab], jnp.zeros((d, LANES - n_ab), bf16),
                                  w_in_l[:, P_AB + n_ab:]], axis=1)
        prm = jnp.stack([_pad_cols(gdn_A_log[l].reshape(1, -1), LANES)[0],
                         _pad_cols(gdn_dt_bias[l].reshape(1, -1), LANES)[0]])
        lp = {
            'norm1_g': norm1_g[l], 'norm2_g': norm2_g[l], 'w_in': w_in_p, 'gdn_conv_w': gdn_conv_w[l],
            'gdn_prm': prm, 'gdn_norm_g': gdn_norm_g[l], 'lru_conv_w': lru_conv_w[l], 'lru_conv_b': lru_conv_b[l],
            'lru_wa': lru_wa[l], 'lru_ba': lru_ba[l], 'lru_wi': lru_wi[l], 'lru_bi': lru_bi[l],
            'lru_lambda': lru_lambda[l], 'w_out': w_out[l].astype(bf16),
        }
        j = l // 2
        if l % 2 == 0:
            f = ffd_w1.shape[2]
            fp = 512 * ((f + 511) // 512)
            ffn = {'kind': 'dense', 'w1': _pad_cols(ffd_w1[j].astype(bf16), fp),
                   'w3': _pad_cols(ffd_w3[j].astype(bf16), fp),
                   'w2': jnp.pad(ffd_w2[j].astype(bf16), ((0, fp - f), (0, 0)))}
        else:
            ffn = {'kind': 'moe', 'wr': _pad_cols(moe_wr[j], LANES), 'br': _pad_cols(moe_br[j].reshape(1, -1), LANES),
                   'w1': moe_w1[j].astype(bf16), 'w3': moe_w3[j].astype(bf16), 'w2': moe_w2[j].astype(bf16)}
            if l == depth - 1:
                ffn['final_g'] = final_g
        next_g = norm1_g[l + 1] if l + 1 < depth else None
        if hp is None:
            (hp,) = _norm_mod(ctx, xp, lp['norm1_g'], mod3, l, rows, 1, 0)
        if hs is None and l == 0:
            hs, xs = _norm_mod(lat, xs, lp['norm1_g'], mod3, l, rows, 1, 0, pos=pos)
        elif hs is None:
            (hs,) = _norm_mod(lat, xs, lp['norm1_g'], mod3, l, rows, 1, 0)
        xp, hp, s_g, s_l = _layer(ctx, xp, hp, l, rows, mod3, lp, ffn, None, None, True, next_g)
        new_gdn.append(s_g)
        new_lru.append(s_l)
        xs, hs, _, _ = _layer(lat, xs, hs, l, rows, mod3, lp, ffn, state_gdn, state_lru, False, next_g)
    if depth % 2 == 1:
        xp, xs = _final_norm(xp, final_g), _final_norm(xs, final_g)
    y_prompt = xp.reshape(bp, sp_len, d)
    y_sample = xs.reshape(bs, ss_len, d)
    return (y_prompt, y_sample, jnp.stack(new_gdn, axis=1), jnp.stack(new_lru, axis=1))
```
